```python
import math
import jax
import jax.numpy as jnp
from jax import lax
import numpy as np

D_MODEL = 2048
BATCH = 2
SEQ = 16384
DEPTH = 1
DEC_BATCH = 32
DEC_SEQ = 16
PAST_LEN = 4096

CHUNK = 64
Q_BLOCK = 128
D_ATT = D_MODEL // 2
N_HEADS = 8
HEAD_DIM = D_ATT // N_HEADS
D_SSM = D_MODEL - D_ATT
SSM_GROUP = 16
SSM_GROUPS = D_SSM // SSM_GROUP
SSM_STATE = 64
D_IN = 3 * D_ATT + N_HEADS + D_SSM
N_EXPERT_GROUPS = 4
EXPERTS_PER_GROUP = 8
N_EXPERTS = N_EXPERT_GROUPS * EXPERTS_PER_GROUP
TOP_K = 2
D_EXPERT = 1024
MOE_BLOCK = 128
EPS = 1e-6

kernel_name = 'hybrid_fox_s5_hmoe_stream_step'


def rmsnorm(x, g):
    xf = x.astype(jnp.float32)
    return xf * lax.rsqrt(jnp.mean(xf * xf, axis=-1, keepdims=True) + EPS) * g.astype(jnp.float32)


def modulate(h, shift, scale):
    return h * (1.0 + scale[:, None, :]) + shift[:, None, :]


def fox_attend(q, k, v, fq, fk, qpos, kpos):
    s = jnp.einsum('bqhd,bkhd->bhqk', q, k, preferred_element_type=jnp.float32) * (HEAD_DIM ** -0.5)
    s = s + (jnp.swapaxes(fq, 1, 2)[..., :, None] - jnp.swapaxes(fk, 1, 2)[..., None, :])
    s = jnp.where(kpos[None, :] <= qpos[:, None], s, -jnp.inf)
    p = jax.nn.softmax(s, axis=-1)
    return jnp.einsum('bhqk,bkhd->bqhd', p, v.astype(jnp.float32))


def fox_prompt(q, k, v, fcum):
    bsz, seq = q.shape[:2]
    n_qb = seq // Q_BLOCK
    pos = jnp.arange(seq)
    qb = jnp.swapaxes(q.reshape(bsz, n_qb, Q_BLOCK, N_HEADS, HEAD_DIM), 0, 1)
    fb = jnp.swapaxes(fcum.reshape(bsz, n_qb, Q_BLOCK, N_HEADS), 0, 1)
    pb = pos.reshape(n_qb, Q_BLOCK)
    out = lax.map(lambda t: fox_attend(t[0], k, v, t[1], fcum, t[2], pos), (qb, fb, pb))
    return jnp.swapaxes(out, 0, 1).reshape(bsz, seq, D_ATT)


def fox_sample(q, k, v, logf, cache_k, cache_v, cache_logf):
    bsz, s_new = q.shape[:2]
    past = cache_k.shape[1]
    k_all = jnp.concatenate([cache_k.astype(jnp.float32), k.astype(jnp.float32)], axis=1)
    v_all = jnp.concatenate([cache_v.astype(jnp.float32), v.astype(jnp.float32)], axis=1)
    f_all = jnp.cumsum(jnp.concatenate([cache_logf.astype(jnp.float32), logf], axis=1), axis=1)
    qpos = past + jnp.arange(s_new)
    kpos = jnp.arange(past + s_new)
    out = fox_attend(q, k_all, v_all, f_all[:, past:], f_all, qpos, kpos)
    return out.reshape(bsz, s_new, D_ATT)


def s5_discretise(lam_re, lam_im, log_dt, b_re, b_im, c_re, c_im):
    lam = lax.complex(lam_re.astype(jnp.float32), lam_im.astype(jnp.float32))
    dt = jnp.exp(log_dt.astype(jnp.float32))[:, None]
    lam_bar = jnp.exp(lam * dt)
    b = lax.complex(b_re.astype(jnp.float32), b_im.astype(jnp.float32))
    b_bar = ((lam_bar - 1.0) / lam)[:, :, None] * b
    c = lax.complex(c_re.astype(jnp.float32), c_im.astype(jnp.float32))
    return lam_bar, b_bar, c


def _linear_recurrence(e1, e2):
    a1, b1 = e1
    a2, b2 = e2
    return a1 * a2, a2 * b1 + b2


def s5_block(h0, u, lam_bar, b_bar, c):
    bu = jnp.einsum('blgh,gph->blgp', u.astype(jnp.float32).astype(jnp.complex64), b_bar)
    bu = bu.at[:, 0].add(lam_bar * h0)
    a = jnp.broadcast_to(lam_bar, bu.shape)
    _, h = lax.associative_scan(_linear_recurrence, (a, bu), axis=1)
    y = jnp.einsum('blgp,ghp->blgh', h, c).real
    return h[:, -1], y


def s5_prompt(u, lam_bar, b_bar, c):
    bsz, seq, _ = u.shape
    n_blk = seq // CHUNK
    ub = jnp.swapaxes(u.reshape(bsz, n_blk, CHUNK, SSM_GROUPS, SSM_GROUP), 0, 1)

    def step(h, u_blk):
        return s5_block(h, u_blk, lam_bar, b_bar, c)

    h0 = jnp.zeros((bsz, SSM_GROUPS, SSM_STATE), jnp.complex64)
    h_last, ys = lax.scan(step, h0, ub)
    return jnp.swapaxes(ys, 0, 1).reshape(bsz, seq, D_SSM), h_last


def s5_glu_out(u, y, d_skip, w_glu, b_glu):
    z = jax.nn.gelu(y + d_skip.astype(jnp.float32) * u)
    return z * jax.nn.sigmoid(z @ w_glu + b_glu)


def hmoe(x, w_rg, w_re, w1, w3, w2):
    bsz, seq, dm = x.shape
    xt = x.reshape(-1, dm)
    n_tok = xt.shape[0]
    tok_idx = jnp.arange(n_tok)
    g_logits = (xt @ w_rg).astype(jnp.float32)
    p_group = jax.nn.softmax(g_logits, axis=-1)
    g_sel = jnp.argmax(g_logits, axis=-1)
    e_logits = (xt @ w_re).astype(jnp.float32).reshape(n_tok, N_EXPERT_GROUPS, EXPERTS_PER_GROUP)
    e_sel = e_logits[tok_idx, g_sel]
    top_v, top_i = lax.top_k(e_sel, TOP_K)
    gate = p_group[tok_idx, g_sel][:, None] * jax.nn.softmax(top_v, axis=-1)
    eid = (g_sel[:, None] * EXPERTS_PER_GROUP + top_i).reshape(-1).astype(jnp.int32)
    w_flat = gate.reshape(-1)
    n_asg = n_tok * TOP_K
    asg_tok = jnp.arange(n_asg) // TOP_K
    order = jnp.argsort(eid, stable=True)
    e_sorted = eid[order]
    counts = jnp.zeros((N_EXPERTS,), jnp.int32).at[eid].add(1)
    padded = (counts + MOE_BLOCK - 1) // MOE_BLOCK * MOE_BLOCK
    starts = jnp.cumsum(counts) - counts
    pends = jnp.cumsum(padded)
    pstarts = pends - padded
    dest = pstarts[e_sorted] + (jnp.arange(n_asg) - starts[e_sorted])
    n_pad = (n_asg + N_EXPERTS * (MOE_BLOCK - 1) + MOE_BLOCK - 1) // MOE_BLOCK * MOE_BLOCK
    row_tok = jnp.full((n_pad,), n_tok, jnp.int32).at[dest].set(asg_tok[order])
    row_w = jnp.zeros((n_pad,), jnp.float32).at[dest].set(w_flat[order])
    n_blk = n_pad // MOE_BLOCK
    blk_e = jnp.minimum(jnp.searchsorted(pends, jnp.arange(n_blk) * MOE_BLOCK, side='right'), N_EXPERTS - 1)
    x_pad = jnp.concatenate([xt, jnp.zeros((1, dm), xt.dtype)], axis=0)
    xs = x_pad[row_tok].reshape(n_blk, MOE_BLOCK, dm)

    def expert_block(args):
        xb, e = args
        hid = jax.nn.silu(xb @ w1[e]) * (xb @ w3[e])
        return hid @ w2[e]

    ys = lax.map(expert_block, (xs, blk_e)).reshape(n_pad, dm).astype(jnp.float32)
    out = jnp.zeros((n_tok + 1, dm), jnp.float32).at[row_tok].add(ys * row_w[:, None])
    return out[:n_tok].reshape(bsz, seq, dm)


def trunk(x, c, mode, cache, params):
    (w_ada, b_ada, g_mix, w_in, b_f, lam_re, lam_im, log_dt, b_re, b_im, c_re, c_im,
     d_skip, w_glu, b_glu, g_att, g_ssm, w_out, g_ffn, w_rg, w_re, w1, w3, w2, g_final) = params
    x = x.astype(jnp.float32)
    bsz, seq = x.shape[:2]
    new_k, new_v, new_lf, new_re, new_im = [], [], [], [], []
    for l in range(DEPTH):
        mod = jax.nn.silu(c.astype(jnp.float32)) @ w_ada[l] + b_ada[l]
        sh1, sc1, gt1, sh2, sc2, gt2 = jnp.split(mod, 6, axis=-1)
        h = modulate(rmsnorm(x, g_mix[l]), sh1, sc1)
        proj = h @ w_in[l]
        q, k, v, fg, u = jnp.split(proj, [D_ATT, 2 * D_ATT, 3 * D_ATT, 3 * D_ATT + N_HEADS], axis=-1)
        q = q.reshape(bsz, seq, N_HEADS, HEAD_DIM)
        k = k.reshape(bsz, seq, N_HEADS, HEAD_DIM)
        v = v.reshape(bsz, seq, N_HEADS, HEAD_DIM)
        logf = jax.nn.log_sigmoid((fg + b_f[l]).astype(jnp.float32))
        lam_bar, b_bar, c_mat = s5_discretise(lam_re[l], lam_im[l], log_dt[l], b_re[l], b_im[l], c_re[l], c_im[l])
        if mode == 'prompt':
            att = fox_prompt(q, k, v, jnp.cumsum(logf, axis=1))
            ssm_y, h_last = s5_prompt(u, lam_bar, b_bar, c_mat)
        else:
            cache_k, cache_v, cache_logf, state_re, state_im = cache
            att = fox_sample(q, k, v, logf, cache_k[l], cache_v[l], cache_logf[l])
            h0 = lax.complex(state_re[l].astype(jnp.float32), state_im[l].astype(jnp.float32))
            h_last, ssm_y = s5_block(h0, u.reshape(bsz, seq, SSM_GROUPS, SSM_GROUP), lam_bar, b_bar, c_mat)
            ssm_y = ssm_y.reshape(bsz, seq, D_SSM)
        ssm = s5_glu_out(u, ssm_y, d_skip[l], w_glu[l], b_glu[l])
        mixed = jnp.concatenate([rmsnorm(att, g_att[l]), rmsnorm(ssm, g_ssm[l])], axis=-1) @ w_out[l]
        x = x + gt1[:, None, :] * mixed
        h2 = modulate(rmsnorm(x, g_ffn[l]), sh2, sc2)
        x = x + gt2[:, None, :] * hmoe(h2, w_rg[l], w_re[l], w1[l], w3[l], w2[l])
        new_k.append(k)
        new_v.append(v)
        new_lf.append(logf)
        new_re.append(h_last.real)
        new_im.append(h_last.imag)
    y = rmsnorm(x, g_final)
    return y, jnp.stack(new_k), jnp.stack(new_v), jnp.stack(new_lf), jnp.stack(new_re), jnp.stack(new_im)


def setup_inputs(seed: int = 0) -> dict:
    key = jax.random.key(seed)
    ks = jax.random.split(key, 40)
    f32 = jnp.float32

    def nrm(k, shape, s):
        return s * jax.random.normal(k, shape, f32)

    dm = D_MODEL
    col_scale = jnp.ones((D_IN,), f32).at[3 * D_ATT:3 * D_ATT + N_HEADS].set(0.1)
    n_idx = jnp.arange(SSM_STATE, dtype=f32)
    return {
        'x_prompt': nrm(ks[0], (BATCH, SEQ, dm), 1.0),
        'x_sample': nrm(ks[1], (DEC_BATCH, DEC_SEQ, dm), 1.0),
        'cache_k': nrm(ks[2], (DEPTH, DEC_BATCH, PAST_LEN, N_HEADS, HEAD_DIM), 1.0),
        'cache_v': nrm(ks[3], (DEPTH, DEC_BATCH, PAST_LEN, N_HEADS, HEAD_DIM), 1.0),
        'cache_logf': jax.nn.log_sigmoid(3.0 + jax.random.normal(ks[4], (DEPTH, DEC_BATCH, PAST_LEN, N_HEADS), f32)),
        'state_ssm_re': nrm(ks[5], (DEPTH, DEC_BATCH, SSM_GROUPS, SSM_STATE), 0.1),
        'state_ssm_im': nrm(ks[6], (DEPTH, DEC_BATCH, SSM_GROUPS, SSM_STATE), 0.1),
        'c_prompt': nrm(ks[7], (BATCH, dm), 1.0),
        'c_sample': nrm(ks[8], (DEC_BATCH, dm), 1.0),
        'w_ada': nrm(ks[9], (DEPTH, dm, 6 * dm), 0.5 * dm ** -0.5),
        'b_ada': nrm(ks[10], (DEPTH, 6 * dm), 0.02),
        'g_mix': 1.0 + nrm(ks[11], (DEPTH, dm), 0.02),
        'w_in': nrm(ks[12], (DEPTH, dm, D_IN), dm ** -0.5) * col_scale,
        'b_f': jax.random.uniform(ks[13], (DEPTH, N_HEADS), f32, 1.0, 5.0),
        'lam_re': -0.5 + nrm(ks[14], (DEPTH, SSM_GROUPS, SSM_STATE), 0.01),
        'lam_im': math.pi * n_idx + nrm(ks[15], (DEPTH, SSM_GROUPS, SSM_STATE), 0.01),
        'log_dt': jax.random.uniform(ks[16], (DEPTH, SSM_GROUPS), f32, math.log(1e-3), math.log(1e-1)),
        'b_re': nrm(ks[17], (DEPTH, SSM_GROUPS, SSM_STATE, SSM_GROUP), (0.5 / SSM_GROUP) ** 0.5),
        'b_im': nrm(ks[18], (DEPTH, SSM_GROUPS, SSM_STATE, SSM_GROUP), (0.5 / SSM_GROUP) ** 0.5),
        'c_re': nrm(ks[19], (DEPTH, SSM_GROUPS, SSM_GROUP, SSM_STATE), (0.5 / SSM_STATE) ** 0.5),
        'c_im': nrm(ks[20], (DEPTH, SSM_GROUPS, SSM_GROUP, SSM_STATE), (0.5 / SSM_STATE) ** 0.5),
        'd_skip': nrm(ks[21], (DEPTH, D_SSM), 1.0),
        'w_glu': nrm(ks[22], (DEPTH, D_SSM, D_SSM), D_SSM ** -0.5),
        'b_glu': nrm(ks[23], (DEPTH, D_SSM), 0.02),
        'g_att': 1.0 + nrm(ks[24], (DEPTH, D_ATT), 0.02),
        'g_ssm': 1.0 + nrm(ks[25], (DEPTH, D_SSM), 0.02),
        'w_out': nrm(ks[26], (DEPTH, dm, dm), dm ** -0.5),
        'g_ffn': 1.0 + nrm(ks[27], (DEPTH, dm), 0.02),
        'w_rg': nrm(ks[28], (DEPTH, dm, N_EXPERT_GROUPS), dm ** -0.5),
        'w_re': nrm(ks[29], (DEPTH, dm, N_EXPERTS), dm ** -0.5),
        'w1': nrm(ks[30], (DEPTH, N_EXPERTS, dm, D_EXPERT), dm ** -0.5),
        'w3': nrm(ks[31], (DEPTH, N_EXPERTS, dm, D_EXPERT), dm ** -0.5),
        'w2': nrm(ks[32], (DEPTH, N_EXPERTS, D_EXPERT, dm), D_EXPERT ** -0.5),
        'g_final': 1.0 + nrm(ks[33], (dm,), 0.02),
    }


def reference(x_prompt, x_sample, cache_k, cache_v, cache_logf, state_ssm_re, state_ssm_im,
              c_prompt, c_sample, w_ada, b_ada, g_mix, w_in, b_f, lam_re, lam_im, log_dt,
              b_re, b_im, c_re, c_im, d_skip, w_glu, b_glu, g_att, g_ssm, w_out, g_ffn,
              w_rg, w_re, w1, w3, w2, g_final):
    params = (w_ada, b_ada, g_mix, w_in, b_f, lam_re, lam_im, log_dt, b_re, b_im, c_re, c_im,
              d_skip, w_glu, b_glu, g_att, g_ssm, w_out, g_ffn, w_rg, w_re, w1, w3, w2, g_final)
    y_prompt, k_prompt, v_prompt, logf_prompt, ssm_re_prompt, ssm_im_prompt = trunk(
        x_prompt, c_prompt, 'prompt', None, params)
    y_sample, k_sample, v_sample, logf_sample, ssm_re_sample, ssm_im_sample = trunk(
        x_sample, c_sample, 'sample', (cache_k, cache_v, cache_logf, state_ssm_re, state_ssm_im), params)
    return (y_prompt, y_sample, k_prompt, v_prompt, logf_prompt, ssm_re_prompt, ssm_im_prompt,
            k_sample, v_sample, logf_sample, ssm_re_sample, ssm_im_sample)
```

```python
import functools
import math

import jax
import jax.numpy as jnp
from jax import lax
from jax.experimental import pallas as pl
from jax.experimental.pallas import tpu as pltpu

F32 = jnp.float32
BF16 = jnp.bfloat16
EPS = 1e-6
NEG = -1e30
LANES = 128
SUBLANES = 8
VMEM_LIMIT = 56 * 1024 * 1024
HIGHEST = lax.Precision.HIGHEST


def _cparams(*sem):
    return pltpu.CompilerParams(dimension_semantics=sem, vmem_limit_bytes=VMEM_LIMIT)


def _tile(n, pref):
    t = min(n, pref)
    assert n % t == 0, (n, pref)
    return t


def _rms(x, g):
    return x * lax.rsqrt(jnp.mean(x * x, axis=-1, keepdims=True) + EPS) * g


def _mod_operand(vec, seq_len, tm):
    n_seq, d = vec.shape
    if seq_len % tm == 0:
        per = seq_len // tm
        return vec[:, None, :], pl.BlockSpec((None, 1, d), lambda i, *_: (i // per, 0, 0))
    assert tm % seq_len == 0
    rows = jnp.repeat(vec, seq_len, axis=0).reshape(-1, tm, d)
    return rows, pl.BlockSpec((None, tm, d), lambda i, *_: (i, 0, 0))


def _ada_kernel(c_ref, w_ref, b_ref, o_ref):
    c = c_ref[...]
    a = (c * jax.nn.sigmoid(c)).astype(BF16)
    o_ref[...] = jnp.dot(a, w_ref[...].astype(BF16), preferred_element_type=F32) + b_ref[...]


def _ada(c, w, b):
    s, d = c.shape
    n = w.shape[1]
    tn = _tile(n, 1024)
    return pl.pallas_call(
        _ada_kernel,
        out_shape=jax.ShapeDtypeStruct((s, n), F32),
        grid=(n // tn,),
        in_specs=[pl.BlockSpec((s, d), lambda j: (0, 0)),
                  pl.BlockSpec((d, tn), lambda j: (0, j)),
                  pl.BlockSpec((1, tn), lambda j: (0, j))],
        out_specs=pl.BlockSpec((s, tn), lambda j: (0, j)),
        compiler_params=_cparams("arbitrary"),
        name="ada_mod",
    )(c, w, b.reshape(1, n))


def _inproj_kernel(x_ref, g_ref, sh_ref, sc_ref, w_ref, wf_ref, bf_ref,
                   q_ref, kf_ref, vf_ref, kb_ref, vb_ref, u_ref, lf_ref, h_scr, *, qscale):
    j = pl.program_id(1)

    @pl.when(j == 0)
    def _():
        h = _rms(x_ref[...], g_ref[...]) * (1.0 + sc_ref[...]) + sh_ref[...]
        hb = h.astype(BF16)
        h_scr[...] = hb
        fg = jnp.dot(hb, wf_ref[...], preferred_element_type=F32) + bf_ref[...]
        lf_ref[...] = jnp.minimum(fg, 0.0) - jnp.log1p(jnp.exp(-jnp.abs(fg)))

    p = jnp.dot(h_scr[...], w_ref[...], preferred_element_type=F32)

    @pl.when(j == 0)
    def _():
        q_ref[...] = (p * qscale).astype(BF16)

    @pl.when(j == 1)
    def _():
        kf_ref[...] = p
        kb_ref[...] = p.astype(BF16)

    @pl.when(j == 2)
    def _():
        vf_ref[...] = p
        vb_ref[...] = p.astype(BF16)

    @pl.when(j == 3)
    def _():
        u_ref[...] = p


def _inproj(x, g, sh, sc, w4, wf, bfp, seq_len, qscale):
    t, d = x.shape
    da = w4.shape[1] // 4
    tm = _tile(t, 512)
    sh_op, sh_spec = _mod_operand(sh, seq_len, tm)
    sc_op, sc_spec = _mod_operand(sc, seq_len, tm)
    row = lambda i, j: (i, 0)
    outs = pl.pallas_call(
        functools.partial(_inproj_kernel, qscale=qscale),
        out_shape=(jax.ShapeDtypeStruct((t, da), BF16),
                   jax.ShapeDtypeStruct((t, da), F32), jax.ShapeDtypeStruct((t, da), F32),
                   jax.ShapeDtypeStruct((t, da), BF16), jax.ShapeDtypeStruct((t, da), BF16),
                   jax.ShapeDtypeStruct((t, da), F32),
                   jax.ShapeDtypeStruct((t, LANES), F32)),
        grid=(t // tm, 4),
        in_specs=[pl.BlockSpec((tm, d), row),
                  pl.BlockSpec((1, d), lambda i, j: (0, 0)),
                  sh_spec, sc_spec,
                  pl.BlockSpec((d, da), lambda i, j: (0, j)),
                  pl.BlockSpec((d, LANES), lambda i, j: (0, 0)),
                  pl.BlockSpec((1, LANES), lambda i, j: (0, 0))],
        out_specs=(pl.BlockSpec((tm, da), row),) * 6 + (pl.BlockSpec((tm, LANES), row),),
        scratch_shapes=[pltpu.VMEM((tm, d), BF16)],
        compiler_params=_cparams("arbitrary", "arbitrary"),
        name="in_proj",
    )(x, g.reshape(1, d), sh_op, sc_op, w4, wf, bfp)
    return outs


def _cumsum_kernel(x_ref, o_ref):
    sb, nb, _ = x_ref.shape
    li = lax.broadcasted_iota(jnp.int32, (LANES, LANES), 0)
    lj = lax.broadcasted_iota(jnp.int32, (LANES, LANES), 1)
    upper = (li <= lj).astype(F32)
    ri = lax.broadcasted_iota(jnp.int32, (nb, nb), 0)
    rj = lax.broadcasted_iota(jnp.int32, (nb, nb), 1)
    strict = (rj < ri).astype(F32)
    for s in range(sb):
        within = jnp.dot(x_ref[s], upper, precision=HIGHEST, preferred_element_type=F32)
        tot = jnp.broadcast_to(within[:, LANES - 1:LANES], (nb, LANES))
        off = jnp.dot(strict, tot, precision=HIGHEST, preferred_element_type=F32)
        o_ref[s] = within + off


def _cumsum_rows(x):
    n_rows, n = x.shape
    nb = -(-n // (LANES * SUBLANES)) * SUBLANES
    xp = jnp.pad(x, ((0, 0), (0, nb * LANES - n))).reshape(n_rows, nb, LANES)
    sb = _tile(n_rows, 16)
    out = pl.pallas_call(
        _cumsum_kernel,
        out_shape=jax.ShapeDtypeStruct((n_rows, nb, LANES), F32),
        grid=(n_rows // sb,),
        in_specs=[pl.BlockSpec((sb, nb, LANES), lambda i: (i, 0, 0))],
        out_specs=pl.BlockSpec((sb, nb, LANES), lambda i: (i, 0, 0)),
        compiler_params=_cparams("arbitrary"),
        name="logf_cumsum",
    )(xp)
    return out.reshape(n_rows, nb * LANES)


def _fox_kernel(q_ref, k_ref, v_ref, fq_ref, fk_ref, o_ref, m_scr, l_scr, acc_scr, *, t):
    qi = pl.program_id(2)
    q = q_ref[...]
    fq = fq_ref[...]
    m_scr[...] = jnp.full(m_scr.shape, NEG, F32)
    l_scr[...] = jnp.zeros(l_scr.shape, F32)
    acc_scr[...] = jnp.zeros(acc_scr.shape, F32)

    def step(kt, masked):
        ks = pl.multiple_of(kt * t, t)
        k = k_ref[pl.ds(ks, t), :]
        v = v_ref[pl.ds(ks, t), :]
        s = lax.dot_general(q, k, (((1,), (1,)), ((), ())), preferred_element_type=F32)
        s = s + (fq - fk_ref[kt])
        if masked:
            row = lax.broadcasted_iota(jnp.int32, (t, t), 0)
            col = lax.broadcasted_iota(jnp.int32, (t, t), 1)
            s = jnp.where(col <= row, s, NEG)
        m_prev = m_scr[...]
        m_new = jnp.maximum(m_prev, jnp.max(s, axis=-1, keepdims=True))
        alpha = jnp.exp(m_prev - m_new)
        p = jnp.exp(s - m_new)
        l_scr[...] = alpha * l_scr[...] + jnp.sum(p, axis=-1, keepdims=True)
        acc_scr[...] = alpha * acc_scr[...] + jnp.dot(p.astype(BF16), v, preferred_element_type=F32)
        m_scr[...] = m_new

    def full_step(kt, carry):
        step(kt, False)
        return carry

    lax.fori_loop(0, qi, full_step, 0)
    step(qi, True)
    o_ref[...] = acc_scr[...] / l_scr[...]


def _fox_prompt(qb, kb, vb, fcum, bsz, seq, n_heads, hd):
    t = _tile(seq, 512)
    nq = seq // t
    fq = fcum.reshape(bsz, n_heads, seq, 1)
    fk = fcum.reshape(bsz, n_heads, nq, 1, t)
    return pl.pallas_call(
        functools.partial(_fox_kernel, t=t),
        out_shape=jax.ShapeDtypeStruct((bsz * seq, n_heads * hd), F32),
        grid=(bsz, n_heads, nq),
        in_specs=[pl.BlockSpec((t, hd), lambda b, h, i: (b * nq + i, h)),
                  pl.BlockSpec((seq, hd), lambda b, h, i: (b, h)),
                  pl.BlockSpec((seq, hd), lambda b, h, i: (b, h)),
                  pl.BlockSpec((None, None, t, 1), lambda b, h, i: (b, h, i, 0)),
                  pl.BlockSpec((None, None, nq, 1, t), lambda b, h, i: (b, h, 0, 0, 0))],
        out_specs=pl.BlockSpec((t, hd), lambda b, h, i: (b * nq + i, h)),
        scratch_shapes=[pltpu.VMEM((t, 1), F32), pltpu.VMEM((t, 1), F32), pltpu.VMEM((t, hd), F32)],
        compiler_params=_cparams("arbitrary", "arbitrary", "arbitrary"),
        name="fox_prompt",
    )(qb, kb, vb, fq, fk)


def _col_from_row(row):
    n = row.shape[1]
    eye = lax.broadcasted_iota(jnp.int32, (n, n), 0) == lax.broadcasted_iota(jnp.int32, (n, n), 1)
    return jnp.sum(jnp.where(eye, jnp.broadcast_to(row, (n, n)), 0.0), axis=1, keepdims=True)


def _fox_sample_kernel(qbd_ref, ck_ref, cv_ref, kn_ref, vn_ref, fq_ref, fkc_ref, fkn_ref, qidx_ref,
                       o_ref, m_scr, l_scr, acc_scr, *, n_heads, hd, s_new):
    kt = pl.program_id(1)

    @pl.when(kt == 0)
    def _():
        m_scr[...] = jnp.full(m_scr.shape, NEG, F32)
        l_scr[...] = jnp.zeros(l_scr.shape, F32)
        acc_scr[...] = jnp.zeros(acc_scr.shape, F32)

    qbd = qbd_ref[...]
    fq = fq_ref[...]

    def update(k, v, fk, mask):
        st = jnp.dot(k.astype(BF16), qbd, preferred_element_type=F32) + (fq - fk)
        if mask is not None:
            st = jnp.where(mask, st, NEG)
        m_prev = m_scr[...]
        m_new = jnp.maximum(m_prev, jnp.max(st, axis=0, keepdims=True))
        alpha = jnp.exp(m_prev - m_new)
        p = jnp.exp(st - m_new)
        l_scr[...] = alpha * l_scr[...] + jnp.sum(p, axis=0, keepdims=True)
        pv = jnp.dot(p.T.astype(BF16), v.astype(BF16), preferred_element_type=F32)
        acc_scr[...] = _col_from_row(alpha) * acc_scr[...] + pv
        m_scr[...] = m_new

    update(ck_ref[...], cv_ref[...], fkc_ref[...], None)

    @pl.when(kt == pl.num_programs(1) - 1)
    def _():
        key = lax.broadcasted_iota(jnp.int32, (s_new, LANES), 0)
        update(kn_ref[...], vn_ref[...], fkn_ref[...], key <= qidx_ref[...])
        out = acc_scr[...] / _col_from_row(l_scr[...])
        for h in range(n_heads):
            o_ref[:, h * hd:(h + 1) * hd] = out[h * s_new:(h + 1) * s_new, h * hd:(h + 1) * hd]


def _fox_sample(q, k_new, v_new, f_all, cache_k, cache_v, n_heads, hd):
    bsz, s_new, da = q.shape
    past = cache_k.shape[1]
    assert n_heads * s_new <= LANES
    tk = _tile(past, 1024)
    q4 = q.reshape(bsz, s_new, n_heads, hd)
    eye = jnp.eye(n_heads, dtype=F32)
    qbd = jnp.einsum("bihd,hg->bhdgi", q4, eye).reshape(bsz, da, n_heads * s_new)
    qbd = jnp.pad(qbd, ((0, 0), (0, 0), (0, LANES - n_heads * s_new))).astype(BF16)

    def lanes(x):
        y = jnp.repeat(jnp.swapaxes(x, 1, 2), s_new, axis=2)
        return jnp.pad(y, ((0, 0), (0, 0), (0, LANES - n_heads * s_new)))

    fkc = lanes(f_all[:, :, :past])
    fkn = lanes(f_all[:, :, past:])
    fq = jnp.pad(f_all[:, :, past:].reshape(bsz, 1, n_heads * s_new),
                 ((0, 0), (0, 0), (0, LANES - n_heads * s_new)))
    qidx = (jnp.arange(LANES, dtype=jnp.int32) % s_new).reshape(1, LANES)
    return pl.pallas_call(
        functools.partial(_fox_sample_kernel, n_heads=n_heads, hd=hd, s_new=s_new),
        out_shape=jax.ShapeDtypeStruct((bsz, s_new, da), F32),
        grid=(bsz, past // tk),
        in_specs=[pl.BlockSpec((None, da, LANES), lambda b, j: (b, 0, 0)),
                  pl.BlockSpec((None, tk, da), lambda b, j: (b, j, 0)),
                  pl.BlockSpec((None, tk, da), lambda b, j: (b, j, 0)),
                  pl.BlockSpec((None, s_new, da), lambda b, j: (b, 0, 0)),
                  pl.BlockSpec((None, s_new, da), lambda b, j: (b, 0, 0)),
                  pl.BlockSpec((None, 1, LANES), lambda b, j: (b, 0, 0)),
                  pl.BlockSpec((None, tk, LANES), lambda b, j: (b, j, 0)),
                  pl.BlockSpec((None, s_new, LANES), lambda b, j: (b, 0, 0)),
                  pl.BlockSpec((1, LANES), lambda b, j: (0, 0))],
        out_specs=pl.BlockSpec((None, s_new, da), lambda b, j: (b, 0, 0)),
        scratch_shapes=[pltpu.VMEM((1, LANES), F32), pltpu.VMEM((1, LANES), F32),
                        pltpu.VMEM((LANES, da), F32)],
        compiler_params=_cparams("arbitrary", "arbitrary"),
        name="fox_sample",
    )(qbd, cache_k, cache_v, k_new, v_new, fq, fkc, fkn, qidx)


def _s5_operators(lam_re, lam_im, log_dt, b_re, b_im, c_re, c_im, lc, n_steps):
    g, p = lam_re.shape
    hc = b_re.shape[2]
    dt = jnp.exp(log_dt.astype(F32))[:, None]

    def power(k):
        mag = jnp.exp(lam_re * dt * k)
        return mag * jnp.cos(lam_im * dt * k), mag * jnp.sin(lam_im * dt * k)

    lbr, lbi = power(1.0)
    den = lam_re * lam_re + lam_im * lam_im
    fr = ((lbr - 1.0) * lam_re + lbi * lam_im) / den
    fi = (lbi * lam_re - (lbr - 1.0) * lam_im) / den
    bbr = fr[:, :, None] * b_re - fi[:, :, None] * b_im
    bbi = fr[:, :, None] * b_im + fi[:, :, None] * b_re
    ks = jnp.arange(lc + 1, dtype=F32)[None, :, None]
    mag = jnp.exp(lam_re[:, None, :] * dt[:, None, :] * ks)
    ang = lam_im[:, None, :] * dt[:, None, :] * ks
    pwr, pwi = mag * jnp.cos(ang), mag * jnp.sin(ang)
    cr, ci = jnp.swapaxes(c_re, 1, 2), jnp.swapaxes(c_im, 1, 2)
    d_r = bbr[:, :, :, None] * cr[:, :, None, :] - bbi[:, :, :, None] * ci[:, :, None, :]
    d_i = bbr[:, :, :, None] * ci[:, :, None, :] + bbi[:, :, :, None] * cr[:, :, None, :]
    kern = (jnp.einsum("gtp,gpab->gtab", pwr[:, :lc], d_r, precision=HIGHEST)
            - jnp.einsum("gtp,gpab->gtab", pwi[:, :lc], d_i, precision=HIGHEST))
    s_idx = jnp.arange(lc)[:, None]
    t_idx = jnp.arange(lc)[None, :]
    lag = jnp.clip(t_idx - s_idx, 0, lc - 1)
    toe = jnp.where((t_idx >= s_idx)[None, :, :, None, None], kern[:, lag], 0.0)
    toe = jnp.transpose(toe, (0, 1, 3, 2, 4)).reshape(g, lc * hc, lc * hc)
    rev_r, rev_i = pwr[:, lc - 1::-1][:, :lc], pwi[:, lc - 1::-1][:, :lc]
    w_r = rev_r[:, :, None, :] * jnp.swapaxes(bbr, 1, 2)[:, None] - rev_i[:, :, None, :] * jnp.swapaxes(bbi, 1, 2)[:, None]
    w_i = rev_r[:, :, None, :] * jnp.swapaxes(bbi, 1, 2)[:, None] + rev_i[:, :, None, :] * jnp.swapaxes(bbr, 1, 2)[:, None]
    w_r = w_r.reshape(g, lc * hc, p)
    w_i = w_i.reshape(g, lc * hc, p)
    tw = jnp.concatenate([toe, w_r, w_i, w_i, w_r], axis=-1).astype(BF16)
    nr, ni = pwr[:, 1:], pwi[:, 1:]
    v_r = cr[:, :, None, :] * jnp.swapaxes(nr, 1, 2)[:, :, :, None] - ci[:, :, None, :] * jnp.swapaxes(ni, 1, 2)[:, :, :, None]
    v_i = cr[:, :, None, :] * jnp.swapaxes(ni, 1, 2)[:, :, :, None] + ci[:, :, None, :] * jnp.swapaxes(nr, 1, 2)[:, :, :, None]
    vout = jnp.concatenate([v_r, -v_i], axis=1).reshape(g, 2 * p, lc * hc).astype(BF16)
    ar, ai = pwr[:, lc], pwi[:, lc]
    jr, ji = power(float(lc * n_steps))
    coef = jnp.stack([jnp.concatenate([ar, ar], -1), jnp.concatenate([-ai, ai], -1),
                      jnp.concatenate([jr, jr], -1), jnp.concatenate([-ji, ji], -1)], axis=1)
    coef = jnp.pad(coef, ((0, 0), (0, SUBLANES - 4), (0, 0)))
    return tw, vout, coef


def _s5_kernel(u_ref, tw_ref, v_ref, coef_ref, h0_ref, y_ref, hl_ref, s_scr, hz_scr,
               *, n_steps, rows, n_seg, p2):
    lh = u_ref.shape[1]
    both = jnp.dot(u_ref[...], tw_ref[...], preferred_element_type=F32)
    y_ref[...] = both[:, :lh]
    s_scr[...] = both[:, lh:]
    ca = jnp.broadcast_to(coef_ref[0:1, :], (rows, p2))
    cb = jnp.broadcast_to(coef_ref[1:2, :], (rows, p2))
    caj = coef_ref[2:3, :]
    cbj = coef_ref[3:4, :]

    def scan_zero(j, carry):
        x, xs = carry
        r0 = pl.multiple_of(j * rows, rows)
        hz_scr[pl.ds(r0, rows), :] = x
        s = s_scr[pl.ds(r0, rows), :]
        return ca * x + cb * xs + s[:, :p2], ca * xs - cb * x + s[:, p2:]

    zero = jnp.zeros((rows, p2), F32)
    x_end, xs_end = lax.fori_loop(0, n_steps, scan_zero, (zero, zero))

    h0 = h0_ref[...]
    h0s = pltpu.roll(h0, p2 // 2, axis=1)
    if n_seg == 1:
        e, es = h0, h0s
        hl_ref[...] = caj * e + cbj * es + x_end
    else:
        e_rows, es_rows = [], []
        for b in range(rows // n_seg):
            eb, ebs = h0[b:b + 1], h0s[b:b + 1]
            for sg in range(n_seg):
                r = b * n_seg + sg
                e_rows.append(eb)
                es_rows.append(ebs)
                eb, ebs = (caj * eb + cbj * ebs + x_end[r:r + 1],
                           caj * ebs - cbj * eb + xs_end[r:r + 1])
            hl_ref[b:b + 1, :] = eb
        e = jnp.concatenate(e_rows, axis=0)
        es = jnp.concatenate(es_rows, axis=0)

    def scan_fix(j, carry):
        f, fs = carry
        r0 = pl.multiple_of(j * rows, rows)
        hz_scr[pl.ds(r0, rows), :] = hz_scr[pl.ds(r0, rows), :] + f
        return ca * f + cb * fs, ca * fs - cb * f

    lax.fori_loop(0, n_steps, scan_fix, (e, es))
    y_ref[...] += jnp.dot(hz_scr[...].astype(BF16), v_ref[...], preferred_element_type=F32)


def _s5(u, h0_re, h0_im, ops, lc, n_seg, n_groups):
    tw, vout, coef = ops
    bsz, seq, ds = u.shape
    hc = ds // n_groups
    p2 = vout.shape[1]
    rows = bsz * n_seg
    assert rows % SUBLANES == 0 and seq % (n_seg * lc) == 0
    n_steps = seq // (n_seg * lc)
    m = n_steps * rows
    lh = lc * hc
    ug = u.astype(BF16).reshape(bsz, n_seg, n_steps, lc, n_groups, hc)
    ug = jnp.transpose(ug, (4, 2, 0, 1, 3, 5)).reshape(n_groups, m, lh)
    h0 = jnp.swapaxes(jnp.concatenate([h0_re, h0_im], axis=-1), 0, 1)
    y, hl = pl.pallas_call(
        functools.partial(_s5_kernel, n_steps=n_steps, rows=rows, n_seg=n_seg, p2=p2),
        out_shape=(jax.ShapeDtypeStruct((n_groups, m, lh), F32),
                   jax.ShapeDtypeStruct((n_groups, bsz, p2), F32)),
        grid=(n_groups,),
        in_specs=[pl.BlockSpec((None, m, lh), lambda g: (g, 0, 0)),
                  pl.BlockSpec((None, lh, lh + 2 * p2), lambda g: (g, 0, 0)),
                  pl.BlockSpec((None, p2, lh), lambda g: (g, 0, 0)),
                  pl.BlockSpec((None, SUBLANES, p2), lambda g: (g, 0, 0)),
                  pl.BlockSpec((None, bsz, p2), lambda g: (g, 0, 0))],
        out_specs=(pl.BlockSpec((None, m, lh), lambda g: (g, 0, 0)),
                   pl.BlockSpec((None, bsz, p2), lambda g: (g, 0, 0))),
        scratch_shapes=[pltpu.VMEM((m, 2 * p2), F32), pltpu.VMEM((m, p2), F32)],
        compiler_params=_cparams("arbitrary"),
        name="s5_scan",
    )(ug, tw, vout, coef, h0)
    y = y.reshape(n_groups, n_steps, bsz, n_seg, lc, hc)
    y = jnp.transpose(y, (2, 3, 1, 4, 0, 5)).reshape(bsz, seq, ds)
    hl = jnp.swapaxes(hl, 0, 1)
    return y, hl[:, :, :p2 // 2], hl[:, :, p2 // 2:]


def _glu_kernel(y_ref, u_ref, d_ref, w_ref, b_ref, g_ref, o_ref):
    z = jax.nn.gelu(y_ref[...] + d_ref[...] * u_ref[...])
    gate = jax.nn.sigmoid(jnp.dot(z.astype(BF16), w_ref[...], preferred_element_type=F32) + b_ref[...])
    o_ref[...] = _rms(z * gate, g_ref[...]).astype(BF16)


def _glu(y, u, d_skip, w_glu, b_glu, g_ssm):
    t, ds = y.shape
    tm = _tile(t, 512)
    vec = pl.BlockSpec((1, ds), lambda i: (0, 0))
    row = pl.BlockSpec((tm, ds), lambda i: (i, 0))
    return pl.pallas_call(
        _glu_kernel,
        out_shape=jax.ShapeDtypeStruct((t, ds), BF16),
        grid=(t // tm,),
        in_specs=[row, row, vec, pl.BlockSpec((ds, ds), lambda i: (0, 0)), vec, vec],
        out_specs=row,
        compiler_params=_cparams("arbitrary"),
        name="s5_glu",
    )(y, u, d_skip.reshape(1, ds), w_glu, b_glu.reshape(1, ds), g_ssm.reshape(1, ds))


def _outproj_kernel(att_ref, ssm_ref, x_ref, wa_ref, ws_ref, ga_ref, gt_ref, gf_ref, sh_ref, sc_ref, wr_ref,
                    x1_ref, h2_ref, lg_ref):
    a = _rms(att_ref[...], ga_ref[...]).astype(BF16)
    mixed = (jnp.dot(a, wa_ref[...], preferred_element_type=F32)
             + jnp.dot(ssm_ref[...], ws_ref[...], preferred_element_type=F32))
    x1 = x_ref[...] + gt_ref[...] * mixed
    x1_ref[...] = x1
    h2 = _rms(x1, gf_ref[...]) * (1.0 + sc_ref[...]) + sh_ref[...]
    hi = h2.astype(BF16)
    h2_ref[...] = hi
    lo = (h2 - hi.astype(F32)).astype(BF16)
    r = (jnp.dot(hi, wr_ref[...], preferred_element_type=F32)
         + jnp.dot(lo, wr_ref[...], preferred_element_type=F32))
    lg_ref[...] = r[:, :LANES] + r[:, LANES:]


def _outproj(att, ssm_n, x, wa, ws, g_att, gt1, g_ffn, sh2, sc2, wr, seq_len):
    t, d = x.shape
    da, ds = att.shape[1], ssm_n.shape[1]
    tm = _tile(t, 256)
    gt_op, gt_spec = _mod_operand(gt1, seq_len, tm)
    sh_op, sh_spec = _mod_operand(sh2, seq_len, tm)
    sc_op, sc_spec = _mod_operand(sc2, seq_len, tm)
    row = lambda n: pl.BlockSpec((tm, n), lambda i: (i, 0))
    const = lambda a, b: pl.BlockSpec((a, b), lambda i: (0, 0))
    return pl.pallas_call(
        _outproj_kernel,
        out_shape=(jax.ShapeDtypeStruct((t, d), F32), jax.ShapeDtypeStruct((t, d), BF16),
                   jax.ShapeDtypeStruct((t, LANES), F32)),
        grid=(t // tm,),
        in_specs=[row(da), row(ds), row(d), const(da, d), const(ds, d), const(1, da),
                  gt_spec, const(1, d), sh_spec, sc_spec, const(d, 2 * LANES)],
        out_specs=(row(d), row(d), row(LANES)),
        compiler_params=_cparams("arbitrary"),
        name="out_proj",
    )(att, ssm_n, x, wa, ws, g_att.reshape(1, da), gt_op, g_ffn.reshape(1, d), sh_op, sc_op, wr)


def _expert_kernel(te_ref, nu_ref, xs_ref, rw_ref, w1_ref, w3_ref, w2_ref, o_ref):
    i = pl.program_id(0)

    @pl.when(i < nu_ref[0])
    def _():
        x = xs_ref[...]
        h1 = jnp.dot(x, w1_ref[...], preferred_element_type=F32)
        h3 = jnp.dot(x, w3_ref[...], preferred_element_type=F32)
        hid = (h1 * jax.nn.sigmoid(h1) * h3).astype(BF16)
        o_ref[...] = jnp.dot(hid, w2_ref[...], preferred_element_type=F32) * rw_ref[...]

    @pl.when(i >= nu_ref[0])
    def _():
        o_ref[...] = jnp.zeros(o_ref.shape, F32)


def _route(logits, n_groups, n_experts, top_k, tm):
    t = logits.shape[0]
    epg = n_experts // n_groups
    tok = jnp.arange(t)
    g_logits = logits[:, :n_groups]
    p_group = jax.nn.softmax(g_logits, axis=-1)
    g_sel = jnp.argmax(g_logits, axis=-1)
    e_sel = logits[:, n_groups:n_groups + n_experts].reshape(t, n_groups, epg)[tok, g_sel]
    top_v, top_i = lax.top_k(e_sel, top_k)
    gate = p_group[tok, g_sel][:, None] * jax.nn.softmax(top_v, axis=-1)
    eid = (g_sel[:, None] * epg + top_i).reshape(-1).astype(jnp.int32)
    n_asg = t * top_k
    onehot = (eid[:, None] == jnp.arange(n_experts, dtype=jnp.int32)[None, :]).astype(jnp.int32)
    csum = jnp.cumsum(onehot, axis=0)
    rank = jnp.take_along_axis(csum, eid[:, None], axis=1)[:, 0] - 1
    counts = csum[-1]
    padded = (counts + tm - 1) // tm * tm
    pends = jnp.cumsum(padded)
    dest = (pends - padded)[eid] + rank
    n_pad = (n_asg + n_experts * (tm - 1) + tm - 1) // tm * tm
    row_tok = jnp.zeros((n_pad,), jnp.int32).at[dest].set(jnp.arange(n_asg, dtype=jnp.int32) // top_k)
    row_w = jnp.zeros((n_pad,), F32).at[dest].set(gate.reshape(-1))
    n_tiles = n_pad // tm
    tile_e = jnp.minimum(jnp.searchsorted(pends, jnp.arange(n_tiles) * tm, side="right"),
                         n_experts - 1).astype(jnp.int32)
    n_used = (pends[-1] // tm).astype(jnp.int32).reshape(1)
    return row_tok, row_w, tile_e, n_used, dest.reshape(t, top_k)


def _experts(h2, logits, w1, w3, w2, n_groups, top_k):
    t, d = h2.shape
    n_experts, _, de = w1.shape
    tm = 256
    row_tok, row_w, tile_e, n_used, pos = _route(logits, n_groups, n_experts, top_k, tm)
    n_pad = row_tok.shape[0]
    xs = h2[row_tok]
    ys = pl.pallas_call(
        _expert_kernel,
        out_shape=jax.ShapeDtypeStruct((n_pad, d), F32),
        grid_spec=pltpu.PrefetchScalarGridSpec(
            num_scalar_prefetch=2,
            grid=(n_pad // tm,),
            in_specs=[pl.BlockSpec((tm, d), lambda i, te, nu: (i, 0)),
                      pl.BlockSpec((tm, 1), lambda i, te, nu: (i, 0)),
                      pl.BlockSpec((None, d, de), lambda i, te, nu: (te[i], 0, 0)),
                      pl.BlockSpec((None, d, de), lambda i, te, nu: (te[i], 0, 0)),
                      pl.BlockSpec((None, de, d), lambda i, te, nu: (te[i], 0, 0))],
            out_specs=pl.BlockSpec((tm, d), lambda i, te, nu: (i, 0))),
        compiler_params=_cparams("arbitrary"),
        name="moe_experts",
    )(tile_e, n_used, xs, row_w.reshape(n_pad, 1), w1, w3, w2)
    return [ys[pos[:, k]] for k in range(top_k)]


def _final_kernel(x_ref, gt_ref, g_ref, *rest):
    *y_refs, o_ref = rest
    moe = y_refs[0][...]
    for r in y_refs[1:]:
        moe = moe + r[...]
    o_ref[...] = _rms(x_ref[...] + gt_ref[...] * moe, g_ref[...])


def _final(x1, gt2, g_final, ys, seq_len):
    t, d = x1.shape
    tm = _tile(t, 256)
    gt_op, gt_spec = _mod_operand(gt2, seq_len, tm)
    row = pl.BlockSpec((tm, d), lambda i: (i, 0))
    return pl.pallas_call(
        _final_kernel,
        out_shape=jax.ShapeDtypeStruct((t, d), F32),
        grid=(t // tm,),
        in_specs=[row, gt_spec, pl.BlockSpec((1, d), lambda i: (0, 0))] + [row] * len(ys),
        out_specs=row,
        compiler_params=_cparams("arbitrary"),
        name="moe_combine_norm",
    )(x1, gt_op, g_final.reshape(1, d), *ys)


S5_CHUNK = 16
S5_SEGMENTS = 4
TOP_K = 2


def _trunk(x, mod, cache, wts, s5_ops):
    (g_mix, w4, wf, bfp, d_skip, w_glu, b_glu, g_att, g_ssm, wa, ws, g_ffn, wr, w1, w3, w2,
     g_final, n_heads, hd, n_groups_ssm, n_expert_groups) = wts
    bsz, seq, d = x.shape
    da = n_heads * hd
    t = bsz * seq
    xt = x.reshape(t, d)
    sh1, sc1, gt1, sh2, sc2, gt2 = jnp.split(mod, 6, axis=-1)
    qb, kf, vf, kb, vb, u, lfp = _inproj(xt, g_mix, sh1, sc1, w4, wf, bfp, seq, hd ** -0.5)
    logf = lfp[:, :n_heads].reshape(bsz, seq, n_heads)
    if cache is None:
        fcum = _cumsum_rows(jnp.swapaxes(logf, 1, 2).reshape(bsz * n_heads, seq))[:, :seq]
        att = _fox_prompt(qb, kb, vb, fcum.reshape(bsz, n_heads, seq), bsz, seq, n_heads, hd)
        h0 = jnp.zeros((bsz, n_groups_ssm, s5_ops[1].shape[1] // 2), F32)
        ssm_y, h_re, h_im = _s5(u.reshape(bsz, seq, -1), h0, h0, s5_ops, S5_CHUNK, S5_SEGMENTS, n_groups_ssm)
    else:
        cache_k, cache_v, cache_logf, st_re, st_im = cache
        past = cache_k.shape[1]
        lf_all = jnp.concatenate([cache_logf.astype(F32), logf], axis=1)
        f_all = _cumsum_rows(jnp.swapaxes(lf_all, 1, 2).reshape(bsz * n_heads, past + seq))
        f_all = f_all[:, :past + seq].reshape(bsz, n_heads, past + seq)
        att = _fox_sample(qb.astype(F32).reshape(bsz, seq, da), kf.reshape(bsz, seq, da), vf.reshape(bsz, seq, da),
                          f_all, cache_k.reshape(bsz, past, da), cache_v.reshape(bsz, past, da), n_heads, hd)
        att = att.reshape(t, da)
        ssm_y, h_re, h_im = _s5(u.reshape(bsz, seq, -1), st_re.astype(F32), st_im.astype(F32), s5_ops,
                                seq, 1, n_groups_ssm)
    ssm_n = _glu(ssm_y.reshape(t, -1), u, d_skip, w_glu, b_glu, g_ssm)
    x1, h2, logits = _outproj(att, ssm_n, xt, wa, ws, g_att, gt1, g_ffn, sh2, sc2, wr, seq)
    ys = _experts(h2, logits, w1, w3, w2, n_expert_groups, TOP_K)
    y = _final(x1, gt2, g_final, ys, seq)
    return (y.reshape(bsz, seq, d), kf.reshape(bsz, seq, n_heads, hd), vf.reshape(bsz, seq, n_heads, hd),
            logf, h_re, h_im)


def kernel(x_prompt, x_sample, cache_k, cache_v, cache_logf, state_ssm_re, state_ssm_im, c_prompt, c_sample, w_ada, b_ada, g_mix, w_in, b_f, lam_re, lam_im, log_dt, b_re, b_im, c_re, c_im, d_skip, w_glu, b_glu, g_att, g_ssm, w_out, g_ffn, w_rg, w_re, w1, w3, w2, g_final):
    depth = w_ada.shape[0]
    assert depth == 1, "the residual stream of a deeper stack would have to be threaded through the layers"
    n_heads, hd = cache_k.shape[3], cache_k.shape[4]
    da = n_heads * hd
    d = x_prompt.shape[-1]
    ds = d - da
    assert da == ds
    n_groups_ssm = state_ssm_re.shape[2]
    n_expert_groups = w_rg.shape[-1]
    n_experts = w_re.shape[-1]
    assert n_expert_groups + n_experts <= LANES and n_heads <= LANES
    l = 0
    bp = x_prompt.shape[0]
    mod = _ada(jnp.concatenate([c_prompt, c_sample], axis=0).astype(F32), w_ada[l], b_ada[l])
    wi = w_in[l]
    w4 = jnp.concatenate([wi[:, :3 * da], wi[:, 3 * da + n_heads:]], axis=1).astype(BF16)
    wf = jnp.pad(wi[:, 3 * da:3 * da + n_heads], ((0, 0), (0, LANES - n_heads))).astype(BF16)
    bfp = jnp.pad(b_f[l], (0, LANES - n_heads)).reshape(1, LANES).astype(F32)
    wr = jnp.pad(jnp.concatenate([w_rg[l], w_re[l]], axis=1),
                 ((0, 0), (0, LANES - n_expert_groups - n_experts))).astype(F32)
    wr_hi = wr.astype(BF16)
    wr_lo = (wr - wr_hi.astype(F32)).astype(BF16)
    wr2 = jnp.concatenate([wr_hi, wr_lo], axis=1)
    wo = w_out[l].astype(BF16)
    wts = (g_mix[l], w4, wf, bfp, d_skip[l], w_glu[l].astype(BF16), b_glu[l], g_att[l], g_ssm[l],
           wo[:da], wo[da:], g_ffn[l], wr2, w1[l].astype(BF16), w3[l].astype(BF16), w2[l].astype(BF16),
           g_final, n_heads, hd, n_groups_ssm, n_expert_groups)
    s5_args = (lam_re[l].astype(F32), lam_im[l].astype(F32), log_dt[l], b_re[l].astype(F32), b_im[l].astype(F32),
               c_re[l].astype(F32), c_im[l].astype(F32))
    seq_p = x_prompt.shape[1]
    ops_p = _s5_operators(*s5_args, S5_CHUNK, seq_p // (S5_SEGMENTS * S5_CHUNK))
    ops_s = _s5_operators(*s5_args, x_sample.shape[1], 1)
    yp, kp, vp, lfp, rep, imp = _trunk(x_prompt.astype(F32), mod[:bp], None, wts, ops_p)
    cache = (cache_k[l], cache_v[l], cache_logf[l], state_ssm_re[l], state_ssm_im[l])
    ysm, ksm, vsm, lfs, res, ims = _trunk(x_sample.astype(F32), mod[bp:], cache, wts, ops_s)
    return (yp, ysm, kp[None], vp[None], lfp[None], rep[None], imp[None],
            ksm[None], vsm[None], lfs[None], res[None], ims[None])
```

```python
import functools
import math

import jax
import jax.numpy as jnp
from jax import lax
from jax.experimental import pallas as pl
from jax.experimental.pallas import tpu as pltpu

F32 = jnp.float32
BF16 = jnp.bfloat16
EPS = 1e-6
NEG = -1e30
LOG2E = math.log2(math.e)
LANES = 128
SUBLANES = 8
VMEM_LIMIT = 56 * 1024 * 1024
HIGHEST = lax.Precision.HIGHEST
NT_DIMS = (((1,), (1,)), ((), ()))


def _cparams(*sem):
    return pltpu.CompilerParams(dimension_semantics=sem, vmem_limit_bytes=VMEM_LIMIT)


def _tile(n, pref):
    t = min(n, pref)
    assert n % t == 0, (n, pref)
    return t


def _rms(x, g):
    return x * lax.rsqrt(jnp.mean(x * x, axis=-1, keepdims=True) + EPS) * g


def _mod_operand(vec, seq_len, tm):
    n_seq, d = vec.shape
    if seq_len % tm == 0:
        per = seq_len // tm
        return vec[:, None, :], pl.BlockSpec((None, 1, d), lambda i, *_: (i // per, 0, 0))
    assert tm % seq_len == 0
    rows = jnp.repeat(vec, seq_len, axis=0).reshape(-1, tm, d)
    return rows, pl.BlockSpec((None, tm, d), lambda i, *_: (i, 0, 0))


def _cast_kernel(x_ref, o_ref):
    o_ref[...] = x_ref[...].astype(BF16)


def _to_bf16(w):
    n = w.shape[-1]
    w2 = w.reshape(-1, n)
    rows = w2.shape[0]
    tr = _tile(rows, max(SUBLANES, (1 << 20) // n))
    out = pl.pallas_call(
        _cast_kernel,
        out_shape=jax.ShapeDtypeStruct((rows, n), BF16),
        grid=(rows // tr,),
        in_specs=[pl.BlockSpec((tr, n), lambda i: (i, 0))],
        out_specs=pl.BlockSpec((tr, n), lambda i: (i, 0)),
        compiler_params=_cparams("arbitrary"),
        name="cast_bf16",
    )(w2)
    return out.reshape(w.shape)


def _ada_kernel(c_ref, w_ref, b_ref, o_ref):
    c = c_ref[...]
    a = (c * jax.nn.sigmoid(c)).astype(BF16)
    o_ref[...] = jnp.dot(a, w_ref[...].astype(BF16), preferred_element_type=F32) + b_ref[...]


def _ada(c, w, b):
    s, d = c.shape
    n = w.shape[1]
    tn = _tile(n, 1024)
    return pl.pallas_call(
        _ada_kernel,
        out_shape=jax.ShapeDtypeStruct((s, n), F32),
        grid=(n // tn,),
        in_specs=[pl.BlockSpec((s, d), lambda j: (0, 0)),
                  pl.BlockSpec((d, tn), lambda j: (0, j)),
                  pl.BlockSpec((1, tn), lambda j: (0, j))],
        out_specs=pl.BlockSpec((s, tn), lambda j: (0, j)),
        compiler_params=_cparams("arbitrary"),
        name="ada_mod",
    )(c, w, b.reshape(1, n))


def _inproj_kernel(x_ref, g_ref, sh_ref, sc_ref, w_ref, wf_ref, bf_ref,
                   q_ref, kf_ref, vf_ref, kb_ref, vb_ref, u_ref, lf_ref, h_scr, *, qscale):
    j = pl.program_id(1)

    @pl.when(j == 0)
    def _():
        h = _rms(x_ref[...], g_ref[...]) * (1.0 + sc_ref[...]) + sh_ref[...]
        hb = h.astype(BF16)
        h_scr[...] = hb
        fg = jnp.dot(hb, wf_ref[...], preferred_element_type=F32) + bf_ref[...]
        lf_ref[...] = jnp.minimum(fg, 0.0) - jnp.log1p(jnp.exp(-jnp.abs(fg)))

    p = jnp.dot(h_scr[...], w_ref[...], preferred_element_type=F32)

    @pl.when(j == 0)
    def _():
        q_ref[...] = (p * qscale).astype(BF16)

    @pl.when(j == 1)
    def _():
        kf_ref[...] = p
        kb_ref[...] = p.astype(BF16)

    @pl.when(j == 2)
    def _():
        vf_ref[...] = p
        vb_ref[...] = p.astype(BF16)

    @pl.when(j == 3)
    def _():
        u_ref[...] = p


def _inproj(x, g, sh, sc, w4, wf, bfp, seq_len, qscale):
    t, d = x.shape
    da = w4.shape[1] // 4
    tm = _tile(t, 512)
    sh_op, sh_spec = _mod_operand(sh, seq_len, tm)
    sc_op, sc_spec = _mod_operand(sc, seq_len, tm)
    row = lambda i, j: (i, 0)
    outs = pl.pallas_call(
        functools.partial(_inproj_kernel, qscale=qscale),
        out_shape=(jax.ShapeDtypeStruct((t, da), BF16),
                   jax.ShapeDtypeStruct((t, da), F32), jax.ShapeDtypeStruct((t, da), F32),
                   jax.ShapeDtypeStruct((t, da), BF16), jax.ShapeDtypeStruct((t, da), BF16),
                   jax.ShapeDtypeStruct((t, da), F32),
                   jax.ShapeDtypeStruct((t, LANES), F32)),
        grid=(t // tm, 4),
        in_specs=[pl.BlockSpec((tm, d), row),
                  pl.BlockSpec((1, d), lambda i, j: (0, 0)),
                  sh_spec, sc_spec,
                  pl.BlockSpec((d, da), lambda i, j: (0, j)),
                  pl.BlockSpec((d, LANES), lambda i, j: (0, 0)),
                  pl.BlockSpec((1, LANES), lambda i, j: (0, 0))],
        out_specs=(pl.BlockSpec((tm, da), row),) * 6 + (pl.BlockSpec((tm, LANES), row),),
        scratch_shapes=[pltpu.VMEM((tm, d), BF16)],
        compiler_params=_cparams("arbitrary", "arbitrary"),
        name="in_proj",
    )(x, g.reshape(1, d), sh_op, sc_op, w4, wf, bfp)
    return outs


def _cumsum_kernel(x_ref, o_ref):
    sb, nb, _ = x_ref.shape
    li = lax.broadcasted_iota(jnp.int32, (LANES, LANES), 0)
    lj = lax.broadcasted_iota(jnp.int32, (LANES, LANES), 1)
    upper = (li <= lj).astype(F32)
    ri = lax.broadcasted_iota(jnp.int32, (nb, nb), 0)
    rj = lax.broadcasted_iota(jnp.int32, (nb, nb), 1)
    strict = (rj < ri).astype(F32)
    for s in range(sb):
        within = jnp.dot(x_ref[s], upper, precision=HIGHEST, preferred_element_type=F32)
        tot = jnp.broadcast_to(within[:, LANES - 1:LANES], (nb, LANES))
        off = jnp.dot(strict, tot, precision=HIGHEST, preferred_element_type=F32)
        o_ref[s] = within + off


def _cumsum_rows(x):
    n_rows, n = x.shape
    nb = -(-n // (LANES * SUBLANES)) * SUBLANES
    xp = jnp.pad(x, ((0, 0), (0, nb * LANES - n))).reshape(n_rows, nb, LANES)
    sb = _tile(n_rows, 16)
    out = pl.pallas_call(
        _cumsum_kernel,
        out_shape=jax.ShapeDtypeStruct((n_rows, nb, LANES), F32),
        grid=(n_rows // sb,),
        in_specs=[pl.BlockSpec((sb, nb, LANES), lambda i: (i, 0, 0))],
        out_specs=pl.BlockSpec((sb, nb, LANES), lambda i: (i, 0, 0)),
        compiler_params=_cparams("arbitrary"),
        name="logf_cumsum",
    )(xp)
    return out.reshape(n_rows, nb * LANES)


FOX_TQ = 1024
FOX_SUB = 512


def _col_from_row(row):
    n = row.shape[1]
    eye = lax.broadcasted_iota(jnp.int32, (n, n), 0) == lax.broadcasted_iota(jnp.int32, (n, n), 1)
    return jnp.sum(jnp.where(eye, jnp.broadcast_to(row, (n, n)), 0.0), axis=1, keepdims=True)


def _lane_tiles(x):
    return [x[:, j * LANES:(j + 1) * LANES] for j in range(x.shape[1] // LANES)]


def _fox_kernel(q_ref, k_ref, v_ref, f_ref, o_ref, m_scr, l_scr, acc_scr, *, tq, sub):
    qi = pl.program_id(2)
    nsub = tq // sub
    fqb = [jnp.broadcast_to(_col_from_row(f_ref[qi * nsub + a]), (sub, LANES)) for a in range(nsub)]
    m_scr[...] = jnp.full(m_scr.shape, NEG, F32)
    l_scr[...] = jnp.zeros(l_scr.shape, F32)
    acc_scr[...] = jnp.zeros(acc_scr.shape, F32)

    def chain(a, k, v, fk, diagonal):
        rows = pl.ds(a * sub, sub)
        t1 = lax.dot_general(q_ref[rows, :], k, NT_DIMS, preferred_element_type=F32) - fk
        if diagonal:
            row = lax.broadcasted_iota(jnp.int32, t1.shape, 0)
            col = lax.broadcasted_iota(jnp.int32, t1.shape, 1)
            t1 = jnp.where(col <= row, t1, NEG)
        tiles = _lane_tiles(t1)
        part = functools.reduce(jnp.maximum, tiles)
        m_prev = m_scr[rows, :]
        m_new = jnp.maximum(m_prev, jnp.max(part, axis=-1, keepdims=True) + fqb[a])
        c = m_new - fqb[a]
        p = [jnp.exp2(t - c) for t in tiles]
        alpha = jnp.exp2(m_prev - m_new)
        l_scr[rows, :] = alpha * l_scr[rows, :] + functools.reduce(jnp.add, p)
        pv = jnp.dot(jnp.concatenate(p, axis=1).astype(BF16), v, preferred_element_type=F32)
        acc_scr[rows, :] = alpha * acc_scr[rows, :] + pv
        m_scr[rows, :] = m_new

    def full_step(kt, carry):
        ks = pl.multiple_of(kt * tq, tq)
        k = k_ref[pl.ds(ks, tq), :]
        v = v_ref[pl.ds(ks, tq), :]
        fk = jnp.concatenate([f_ref[kt * nsub + j] for j in range(nsub)], axis=1)
        for a in range(nsub):
            chain(a, k, v, fk, False)
        return carry

    lax.fori_loop(0, qi, full_step, 0)
    for a in range(nsub):
        for j in range(a + 1):
            ks = pl.multiple_of((qi * nsub + j) * sub, sub)
            chain(a, k_ref[pl.ds(ks, sub), :], v_ref[pl.ds(ks, sub), :], f_ref[qi * nsub + j], j == a)
    o_ref[...] = acc_scr[...] / jnp.sum(l_scr[...], axis=-1, keepdims=True)


def _fox_prompt(qb, kb, vb, fcum2, bsz, seq, n_heads, hd):
    assert hd == LANES
    tq = _tile(seq, FOX_TQ)
    sub = _tile(tq, FOX_SUB)
    nq = seq // tq
    f = fcum2.reshape(bsz, n_heads, seq // sub, 1, sub)
    return pl.pallas_call(
        functools.partial(_fox_kernel, tq=tq, sub=sub),
        out_shape=jax.ShapeDtypeStruct((bsz * seq, n_heads * hd), F32),
        grid=(bsz, n_heads, nq),
        in_specs=[pl.BlockSpec((tq, hd), lambda b, h, i: (b * nq + i, h)),
                  pl.BlockSpec((seq, hd), lambda b, h, i: (b, h)),
                  pl.BlockSpec((seq, hd), lambda b, h, i: (b, h)),
                  pl.BlockSpec((None, None, seq // sub, 1, sub), lambda b, h, i: (b, h, 0, 0, 0))],
        out_specs=pl.BlockSpec((tq, hd), lambda b, h, i: (b * nq + i, h)),
        scratch_shapes=[pltpu.VMEM((tq, LANES), F32), pltpu.VMEM((tq, LANES), F32), pltpu.VMEM((tq, hd), F32)],
        compiler_params=_cparams("arbitrary", "arbitrary", "arbitrary"),
        name="fox_prompt",
    )(qb, kb, vb, f)


def _fox_sample_kernel(q_ref, ck_ref, cv_ref, kn_ref, vn_ref, fq_ref, fkc_ref, fkn_ref,
                       rh_ref, ri_ref, lh_ref, lhn_ref, kin_ref, o_ref, m_scr, l_scr, acc_scr):
    kt = pl.program_id(1)

    @pl.when(kt == 0)
    def _():
        m_scr[...] = jnp.full(m_scr.shape, NEG, F32)
        l_scr[...] = jnp.zeros(l_scr.shape, F32)
        acc_scr[...] = jnp.zeros(acc_scr.shape, F32)

    q = q_ref[...]
    fq = fq_ref[...]
    rh = rh_ref[...]

    def update(k4, v4, fk, lane_head):
        n, h, d = k4.shape
        k2 = k4.reshape(n * h, d).astype(BF16)
        v2 = v4.reshape(n * h, d).astype(BF16)
        s = lax.dot_general(q, k2, NT_DIMS, preferred_element_type=F32) + (fq - fk)
        s = jnp.where(rh == lane_head, s, NEG)
        m_prev = m_scr[...]
        m_new = jnp.maximum(m_prev, jnp.max(s, axis=-1, keepdims=True))
        alpha = jnp.exp2(m_prev - m_new)
        p = jnp.exp2(s - m_new)
        l_scr[...] = alpha * l_scr[...] + jnp.sum(p, axis=-1, keepdims=True)
        acc_scr[...] = alpha * acc_scr[...] + jnp.dot(p.astype(BF16), v2, preferred_element_type=F32)
        m_scr[...] = m_new

    update(ck_ref[...], cv_ref[...], fkc_ref[...], lh_ref[...])

    @pl.when(kt == pl.num_programs(1) - 1)
    def _():
        causal_head = jnp.where(kin_ref[...] <= ri_ref[...], lhn_ref[...], -1)
        update(kn_ref[...], vn_ref[...], fkn_ref[...], causal_head)
        o_ref[...] = acc_scr[...] / l_scr[...]


def _fox_sample(qb, k_new, v_new, f_all2, cache_k, cache_v):
    bsz, s_new, n_heads, hd = qb.shape
    past = cache_k.shape[1]
    tk = _tile(past, 1024)
    n_rows = -(-n_heads * s_new // 16) * 16
    pad_rows = lambda x: jnp.pad(x, ((0, 0), (0, n_rows - n_heads * s_new), (0, 0)))
    q2 = pad_rows(jnp.swapaxes(qb, 1, 2).reshape(bsz, n_heads * s_new, hd))
    fq = pad_rows(f_all2[:, :, past:].reshape(bsz, n_heads * s_new, 1))
    fkc = jnp.swapaxes(f_all2[:, :, :past], 1, 2).reshape(bsz, 1, past * n_heads)
    fkn = jnp.swapaxes(f_all2[:, :, past:], 1, 2).reshape(bsz, 1, s_new * n_heads)
    r = jnp.arange(n_rows, dtype=jnp.int32).reshape(n_rows, 1)
    lane = lambda n: jnp.arange(n * n_heads, dtype=jnp.int32).reshape(1, n * n_heads)
    const = lambda a: pl.BlockSpec(a.shape, lambda b, j: (0,) * a.ndim)
    consts = (r // s_new, r % s_new, lane(tk) % n_heads, lane(s_new) % n_heads, lane(s_new) // n_heads)
    out = pl.pallas_call(
        _fox_sample_kernel,
        out_shape=jax.ShapeDtypeStruct((bsz, n_rows, hd), F32),
        grid=(bsz, past // tk),
        in_specs=[pl.BlockSpec((None, n_rows, hd), lambda b, j: (b, 0, 0)),
                  pl.BlockSpec((None, tk, n_heads, hd), lambda b, j: (b, j, 0, 0)),
                  pl.BlockSpec((None, tk, n_heads, hd), lambda b, j: (b, j, 0, 0)),
                  pl.BlockSpec((None, s_new, n_heads, hd), lambda b, j: (b, 0, 0, 0)),
                  pl.BlockSpec((None, s_new, n_heads, hd), lambda b, j: (b, 0, 0, 0)),
                  pl.BlockSpec((None, n_rows, 1), lambda b, j: (b, 0, 0)),
                  pl.BlockSpec((None, 1, tk * n_heads), lambda b, j: (b, 0, j)),
                  pl.BlockSpec((None, 1, s_new * n_heads), lambda b, j: (b, 0, 0))]
                 + [const(a) for a in consts],
        out_specs=pl.BlockSpec((None, n_rows, hd), lambda b, j: (b, 0, 0)),
        scratch_shapes=[pltpu.VMEM((n_rows, 1), F32), pltpu.VMEM((n_rows, 1), F32), pltpu.VMEM((n_rows, hd), F32)],
        compiler_params=_cparams("arbitrary", "arbitrary"),
        name="fox_sample",
    )(q2, cache_k, cache_v, k_new, v_new, fq, fkc, fkn, *consts)
    out = out[:, :n_heads * s_new].reshape(bsz, n_heads, s_new, hd)
    return jnp.swapaxes(out, 1, 2)


def _s5_operators(lam_re, lam_im, log_dt, b_re, b_im, c_re, c_im, lc, n_steps):
    g, p = lam_re.shape
    hc = b_re.shape[2]
    dt = jnp.exp(log_dt.astype(F32))[:, None]

    def power(k):
        mag = jnp.exp(lam_re * dt * k)
        return mag * jnp.cos(lam_im * dt * k), mag * jnp.sin(lam_im * dt * k)

    lbr, lbi = power(1.0)
    den = lam_re * lam_re + lam_im * lam_im
    fr = ((lbr - 1.0) * lam_re + lbi * lam_im) / den
    fi = (lbi * lam_re - (lbr - 1.0) * lam_im) / den
    bbr = fr[:, :, None] * b_re - fi[:, :, None] * b_im
    bbi = fr[:, :, None] * b_im + fi[:, :, None] * b_re
    ks = jnp.arange(lc + 1, dtype=F32)[None, :, None]
    mag = jnp.exp(lam_re[:, None, :] * dt[:, None, :] * ks)
    ang = lam_im[:, None, :] * dt[:, None, :] * ks
    pwr, pwi = mag * jnp.cos(ang), mag * jnp.sin(ang)
    cr, ci = jnp.swapaxes(c_re, 1, 2), jnp.swapaxes(c_im, 1, 2)
    d_r = bbr[:, :, :, None] * cr[:, :, None, :] - bbi[:, :, :, None] * ci[:, :, None, :]
    d_i = bbr[:, :, :, None] * ci[:, :, None, :] + bbi[:, :, :, None] * cr[:, :, None, :]
    kern = (jnp.einsum("gtp,gpab->gtab", pwr[:, :lc], d_r, precision=HIGHEST)
            - jnp.einsum("gtp,gpab->gtab", pwi[:, :lc], d_i, precision=HIGHEST))
    s_idx = jnp.arange(lc)[:, None]
    t_idx = jnp.arange(lc)[None, :]
    lag = jnp.clip(t_idx - s_idx, 0, lc - 1)
    toe = jnp.where((t_idx >= s_idx)[None, :, :, None, None], kern[:, lag], 0.0)
    toe = jnp.transpose(toe, (0, 1, 3, 2, 4)).reshape(g, lc * hc, lc * hc)
    rev_r, rev_i = pwr[:, lc - 1::-1][:, :lc], pwi[:, lc - 1::-1][:, :lc]
    w_r = rev_r[:, :, None, :] * jnp.swapaxes(bbr, 1, 2)[:, None] - rev_i[:, :, None, :] * jnp.swapaxes(bbi, 1, 2)[:, None]
    w_i = rev_r[:, :, None, :] * jnp.swapaxes(bbi, 1, 2)[:, None] + rev_i[:, :, None, :] * jnp.swapaxes(bbr, 1, 2)[:, None]
    w_r = w_r.reshape(g, lc * hc, p)
    w_i = w_i.reshape(g, lc * hc, p)
    tw = jnp.concatenate([toe, w_r, w_i, w_i, w_r], axis=-1).astype(BF16)
    nr, ni = pwr[:, 1:], pwi[:, 1:]
    v_r = cr[:, :, None, :] * jnp.swapaxes(nr, 1, 2)[:, :, :, None] - ci[:, :, None, :] * jnp.swapaxes(ni, 1, 2)[:, :, :, None]
    v_i = cr[:, :, None, :] * jnp.swapaxes(ni, 1, 2)[:, :, :, None] + ci[:, :, None, :] * jnp.swapaxes(nr, 1, 2)[:, :, :, None]
    vout = jnp.concatenate([v_r, -v_i], axis=1).reshape(g, 2 * p, lc * hc).astype(BF16)
    ar, ai = pwr[:, lc], pwi[:, lc]
    jr, ji = power(float(lc * n_steps))
    coef = jnp.stack([jnp.concatenate([ar, ar], -1), jnp.concatenate([-ai, ai], -1),
                      jnp.concatenate([jr, jr], -1), jnp.concatenate([-ji, ji], -1)], axis=1)
    coef = jnp.pad(coef, ((0, 0), (0, SUBLANES - 4), (0, 0)))
    return tw, vout, coef


def _s5_kernel(u_ref, tw_ref, v_ref, coef_ref, h0_ref, y_ref, hl_ref, s_scr, hz_scr,
               *, n_steps, rows, n_seg, p2):
    lh = u_ref.shape[1]
    both = jnp.dot(u_ref[...], tw_ref[...], preferred_element_type=F32)
    y_ref[...] = both[:, :lh]
    s_scr[...] = both[:, lh:]
    ca = jnp.broadcast_to(coef_ref[0:1, :], (rows, p2))
    cb = jnp.broadcast_to(coef_ref[1:2, :], (rows, p2))
    caj = coef_ref[2:3, :]
    cbj = coef_ref[3:4, :]

    def scan_zero(j, carry):
        x, xs = carry
        r0 = pl.multiple_of(j * rows, rows)
        hz_scr[pl.ds(r0, rows), :] = x
        s = s_scr[pl.ds(r0, rows), :]
        return ca * x + cb * xs + s[:, :p2], ca * xs - cb * x + s[:, p2:]

    zero = jnp.zeros((rows, p2), F32)
    x_end, xs_end = lax.fori_loop(0, n_steps, scan_zero, (zero, zero))

    h0 = h0_ref[...]
    h0s = pltpu.roll(h0, p2 // 2, axis=1)
    if n_seg == 1:
        e, es = h0, h0s
        hl_ref[...] = caj * e + cbj * es + x_end
    else:
        e_rows, es_rows = [], []
        for b in range(rows // n_seg):
            eb, ebs = h0[b:b + 1], h0s[b:b + 1]
            for sg in range(n_seg):
                r = b * n_seg + sg
                e_rows.append(eb)
                es_rows.append(ebs)
                eb, ebs = (caj * eb + cbj * ebs + x_end[r:r + 1],
                           caj * ebs - cbj * eb + xs_end[r:r + 1])
            hl_ref[b:b + 1, :] = eb
        e = jnp.concatenate(e_rows, axis=0)
        es = jnp.concatenate(es_rows, axis=0)

    def scan_fix(j, carry):
        f, fs = carry
        r0 = pl.multiple_of(j * rows, rows)
        hz_scr[pl.ds(r0, rows), :] = hz_scr[pl.ds(r0, rows), :] + f
        return ca * f + cb * fs, ca * fs - cb * f

    lax.fori_loop(0, n_steps, scan_fix, (e, es))
    y_ref[...] += jnp.dot(hz_scr[...].astype(BF16), v_ref[...], preferred_element_type=F32)


def _s5(u, h0_re, h0_im, ops, lc, n_seg, n_groups):
    tw, vout, coef = ops
    bsz, seq, ds = u.shape
    hc = ds // n_groups
    p2 = vout.shape[1]
    rows = bsz * n_seg
    assert rows % SUBLANES == 0 and seq % (n_seg * lc) == 0
    n_steps = seq // (n_seg * lc)
    m = n_steps * rows
    lh = lc * hc
    ug = u.astype(BF16).reshape(bsz, n_seg, n_steps, lc, n_groups, hc)
    ug = jnp.transpose(ug, (4, 2, 0, 1, 3, 5)).reshape(n_groups, m, lh)
    h0 = jnp.swapaxes(jnp.concatenate([h0_re, h0_im], axis=-1), 0, 1)
    y, hl = pl.pallas_call(
        functools.partial(_s5_kernel, n_steps=n_steps, rows=rows, n_seg=n_seg, p2=p2),
        out_shape=(jax.ShapeDtypeStruct((n_groups, m, lh), F32),
                   jax.ShapeDtypeStruct((n_groups, bsz, p2), F32)),
        grid=(n_groups,),
        in_specs=[pl.BlockSpec((None, m, lh), lambda g: (g, 0, 0)),
                  pl.BlockSpec((None, lh, lh + 2 * p2), lambda g: (g, 0, 0)),
                  pl.BlockSpec((None, p2, lh), lambda g: (g, 0, 0)),
                  pl.BlockSpec((None, SUBLANES, p2), lambda g: (g, 0, 0)),
                  pl.BlockSpec((None, bsz, p2), lambda g: (g, 0, 0))],
        out_specs=(pl.BlockSpec((None, m, lh), lambda g: (g, 0, 0)),
                   pl.BlockSpec((None, bsz, p2), lambda g: (g, 0, 0))),
        scratch_shapes=[pltpu.VMEM((m, 2 * p2), F32), pltpu.VMEM((m, p2), F32)],
        compiler_params=_cparams("arbitrary"),
        name="s5_scan",
    )(ug, tw, vout, coef, h0)
    y = y.reshape(n_groups, n_steps, bsz, n_seg, lc, hc)
    y = jnp.transpose(y, (2, 3, 1, 4, 0, 5)).reshape(bsz, seq, ds)
    hl = jnp.swapaxes(hl, 0, 1)
    return y, hl[:, :, :p2 // 2], hl[:, :, p2 // 2:]


def _glu_kernel(y_ref, u_ref, d_ref, w_ref, b_ref, g_ref, o_ref):
    z = jax.nn.gelu(y_ref[...] + d_ref[...] * u_ref[...])
    gate = jax.nn.sigmoid(jnp.dot(z.astype(BF16), w_ref[...], preferred_element_type=F32) + b_ref[...])
    o_ref[...] = _rms(z * gate, g_ref[...]).astype(BF16)


def _glu(y, u, d_skip, w_glu, b_glu, g_ssm):
    t, ds = y.shape
    tm = _tile(t, 512)
    vec = pl.BlockSpec((1, ds), lambda i: (0, 0))
    row = pl.BlockSpec((tm, ds), lambda i: (i, 0))
    return pl.pallas_call(
        _glu_kernel,
        out_shape=jax.ShapeDtypeStruct((t, ds), BF16),
        grid=(t // tm,),
        in_specs=[row, row, vec, pl.BlockSpec((ds, ds), lambda i: (0, 0)), vec, vec],
        out_specs=row,
        compiler_params=_cparams("arbitrary"),
        name="s5_glu",
    )(y, u, d_skip.reshape(1, ds), w_glu, b_glu.reshape(1, ds), g_ssm.reshape(1, ds))


def _outproj_kernel(att_ref, ssm_ref, x_ref, wa_ref, ws_ref, ga_ref, gt_ref, gf_ref, sh_ref, sc_ref, wr_ref,
                    x1_ref, h2_ref, lg_ref):
    a = _rms(att_ref[...], ga_ref[...]).astype(BF16)
    mixed = (jnp.dot(a, wa_ref[...], preferred_element_type=F32)
             + jnp.dot(ssm_ref[...], ws_ref[...], preferred_element_type=F32))
    x1 = x_ref[...] + gt_ref[...] * mixed
    x1_ref[...] = x1
    h2 = _rms(x1, gf_ref[...]) * (1.0 + sc_ref[...]) + sh_ref[...]
    hi = h2.astype(BF16)
    h2_ref[...] = hi
    lo = (h2 - hi.astype(F32)).astype(BF16)
    r = (jnp.dot(hi, wr_ref[...], preferred_element_type=F32)
         + jnp.dot(lo, wr_ref[...], preferred_element_type=F32))
    lg_ref[...] = r[:, :LANES] + r[:, LANES:]


def _outproj(att, ssm_n, x, wa, ws, g_att, gt1, g_ffn, sh2, sc2, wr, seq_len):
    t, d = x.shape
    da, ds = att.shape[1], ssm_n.shape[1]
    tm = _tile(t, 256)
    gt_op, gt_spec = _mod_operand(gt1, seq_len, tm)
    sh_op, sh_spec = _mod_operand(sh2, seq_len, tm)
    sc_op, sc_spec = _mod_operand(sc2, seq_len, tm)
    row = lambda n: pl.BlockSpec((tm, n), lambda i: (i, 0))
    const = lambda a, b: pl.BlockSpec((a, b), lambda i: (0, 0))
    return pl.pallas_call(
        _outproj_kernel,
        out_shape=(jax.ShapeDtypeStruct((t, d), F32), jax.ShapeDtypeStruct((t, d), BF16),
                   jax.ShapeDtypeStruct((t, LANES), F32)),
        grid=(t // tm,),
        in_specs=[row(da), row(ds), row(d), const(da, d), const(ds, d), const(1, da),
                  gt_spec, const(1, d), sh_spec, sc_spec, const(d, 2 * LANES)],
        out_specs=(row(d), row(d), row(LANES)),
        compiler_params=_cparams("arbitrary"),
        name="out_proj",
    )(att, ssm_n, x, wa, ws, g_att.reshape(1, da), gt_op, g_ffn.reshape(1, d), sh_op, sc_op, wr)


def _expert_kernel(te_ref, nu_ref, xs_ref, rw_ref, w1_ref, w3_ref, w2_ref, o_ref):
    i = pl.program_id(0)

    @pl.when(i < nu_ref[0])
    def _():
        x = xs_ref[...]
        h1 = jnp.dot(x, w1_ref[...], preferred_element_type=F32)
        h3 = jnp.dot(x, w3_ref[...], preferred_element_type=F32)
        hid = (h1 * jax.nn.sigmoid(h1) * h3).astype(BF16)
        o_ref[...] = jnp.dot(hid, w2_ref[...], preferred_element_type=F32) * rw_ref[...]

    @pl.when(i >= nu_ref[0])
    def _():
        o_ref[...] = jnp.zeros(o_ref.shape, F32)


def _route(logits, n_groups, n_experts, top_k, tm):
    t = logits.shape[0]
    epg = n_experts // n_groups
    tok = jnp.arange(t)
    g_logits = logits[:, :n_groups]
    p_group = jax.nn.softmax(g_logits, axis=-1)
    g_sel = jnp.argmax(g_logits, axis=-1)
    e_sel = logits[:, n_groups:n_groups + n_experts].reshape(t, n_groups, epg)[tok, g_sel]
    top_v, top_i = lax.top_k(e_sel, top_k)
    gate = p_group[tok, g_sel][:, None] * jax.nn.softmax(top_v, axis=-1)
    eid = (g_sel[:, None] * epg + top_i).reshape(-1).astype(jnp.int32)
    n_asg = t * top_k
    onehot = (eid[:, None] == jnp.arange(n_experts, dtype=jnp.int32)[None, :]).astype(jnp.int32)
    csum = jnp.cumsum(onehot, axis=0)
    rank = jnp.take_along_axis(csum, eid[:, None], axis=1)[:, 0] - 1
    counts = csum[-1]
    padded = (counts + tm - 1) // tm * tm
    pends = jnp.cumsum(padded)
    dest = (pends - padded)[eid] + rank
    n_pad = (n_asg + n_experts * (tm - 1) + tm - 1) // tm * tm
    row_asg = jnp.full((n_pad,), -1, jnp.int32).at[dest].set(jnp.arange(n_asg, dtype=jnp.int32))
    live = row_asg >= 0
    row_tok = jnp.where(live, row_asg // top_k, 0)
    row_w = jnp.where(live, gate.reshape(-1)[jnp.maximum(row_asg, 0)], 0.0)
    n_tiles = n_pad // tm
    tile_e = jnp.minimum(jnp.searchsorted(pends, jnp.arange(n_tiles) * tm, side="right"),
                         n_experts - 1).astype(jnp.int32)
    n_used = (pends[-1] // tm).astype(jnp.int32).reshape(1)
    return row_tok, row_w, tile_e, n_used, dest.reshape(t, top_k)


def _experts(h2, logits, w1, w3, w2, n_groups, top_k):
    t, d = h2.shape
    n_experts, _, de = w1.shape
    tm = 256
    row_tok, row_w, tile_e, n_used, pos = _route(logits, n_groups, n_experts, top_k, tm)
    n_pad = row_tok.shape[0]
    xs = h2[row_tok]
    ys = pl.pallas_call(
        _expert_kernel,
        out_shape=jax.ShapeDtypeStruct((n_pad, d), F32),
        grid_spec=pltpu.PrefetchScalarGridSpec(
            num_scalar_prefetch=2,
            grid=(n_pad // tm,),
            in_specs=[pl.BlockSpec((tm, d), lambda i, te, nu: (i, 0)),
                      pl.BlockSpec((tm, 1), lambda i, te, nu: (i, 0)),
                      pl.BlockSpec((None, d, de), lambda i, te, nu: (te[i], 0, 0)),
                      pl.BlockSpec((None, d, de), lambda i, te, nu: (te[i], 0, 0)),
                      pl.BlockSpec((None, de, d), lambda i, te, nu: (te[i], 0, 0))],
            out_specs=pl.BlockSpec((tm, d), lambda i, te, nu: (i, 0))),
        compiler_params=_cparams("arbitrary"),
        name="moe_experts",
    )(tile_e, n_used, xs, row_w.reshape(n_pad, 1), w1, w3, w2)
    return [ys[pos[:, k]] for k in range(top_k)]


def _final_kernel(x_ref, gt_ref, g_ref, *rest):
    *y_refs, o_ref = rest
    moe = y_refs[0][...]
    for r in y_refs[1:]:
        moe = moe + r[...]
    o_ref[...] = _rms(x_ref[...] + gt_ref[...] * moe, g_ref[...])


def _final(x1, gt2, g_final, ys, seq_len):
    t, d = x1.shape
    tm = _tile(t, 256)
    gt_op, gt_spec = _mod_operand(gt2, seq_len, tm)
    row = pl.BlockSpec((tm, d), lambda i: (i, 0))
    return pl.pallas_call(
        _final_kernel,
        out_shape=jax.ShapeDtypeStruct((t, d), F32),
        grid=(t // tm,),
        in_specs=[row, gt_spec, pl.BlockSpec((1, d), lambda i: (0, 0))] + [row] * len(ys),
        out_specs=row,
        compiler_params=_cparams("arbitrary"),
        name="moe_combine_norm",
    )(x1, gt_op, g_final.reshape(1, d), *ys)


S5_CHUNK = 16
S5_SEGMENTS = 4
TOP_K = 2


def _trunk(x, mod, cache, wts, s5_ops):
    (g_mix, w4, wf, bfp, d_skip, w_glu, b_glu, g_att, g_ssm, wa, ws, g_ffn, wr, w1, w3, w2,
     g_final, n_heads, hd, n_groups_ssm, n_expert_groups) = wts
    bsz, seq, d = x.shape
    da = n_heads * hd
    t = bsz * seq
    xt = x.reshape(t, d)
    sh1, sc1, gt1, sh2, sc2, gt2 = jnp.split(mod, 6, axis=-1)
    qb, kf, vf, kb, vb, u, lfp = _inproj(xt, g_mix, sh1, sc1, w4, wf, bfp, seq, hd ** -0.5 * LOG2E)
    logf = lfp[:, :n_heads].reshape(bsz, seq, n_heads)
    if cache is None:
        fcum = _cumsum_rows(jnp.swapaxes(logf, 1, 2).reshape(bsz * n_heads, seq))[:, :seq]
        att = _fox_prompt(qb, kb, vb, (fcum * LOG2E).reshape(bsz, n_heads, seq), bsz, seq, n_heads, hd)
        h0 = jnp.zeros((bsz, n_groups_ssm, s5_ops[1].shape[1] // 2), F32)
        ssm_y, h_re, h_im = _s5(u.reshape(bsz, seq, -1), h0, h0, s5_ops, S5_CHUNK, S5_SEGMENTS, n_groups_ssm)
    else:
        cache_k, cache_v, cache_logf, st_re, st_im = cache
        past = cache_k.shape[1]
        lf_all = jnp.concatenate([cache_logf.astype(F32), logf], axis=1)
        f_all = _cumsum_rows(jnp.swapaxes(lf_all, 1, 2).reshape(bsz * n_heads, past + seq))
        f_all = (f_all[:, :past + seq] * LOG2E).reshape(bsz, n_heads, past + seq)
        att = _fox_sample(qb.reshape(bsz, seq, n_heads, hd), kf.reshape(bsz, seq, n_heads, hd),
                          vf.reshape(bsz, seq, n_heads, hd), f_all, cache_k, cache_v)
        att = att.reshape(t, da)
        ssm_y, h_re, h_im = _s5(u.reshape(bsz, seq, -1), st_re.astype(F32), st_im.astype(F32), s5_ops,
                                seq, 1, n_groups_ssm)
    ssm_n = _glu(ssm_y.reshape(t, -1), u, d_skip, w_glu, b_glu, g_ssm)
    x1, h2, logits = _outproj(att, ssm_n, xt, wa, ws, g_att, gt1, g_ffn, sh2, sc2, wr, seq)
    ys = _experts(h2, logits, w1, w3, w2, n_expert_groups, TOP_K)
    y = _final(x1, gt2, g_final, ys, seq)
    return (y.reshape(bsz, seq, d), kf.reshape(bsz, seq, n_heads, hd), vf.reshape(bsz, seq, n_heads, hd),
            logf, h_re, h_im)


def kernel(x_prompt, x_sample, cache_k, cache_v, cache_logf, state_ssm_re, state_ssm_im, c_prompt, c_sample, w_ada, b_ada, g_mix, w_in, b_f, lam_re, lam_im, log_dt, b_re, b_im, c_re, c_im, d_skip, w_glu, b_glu, g_att, g_ssm, w_out, g_ffn, w_rg, w_re, w1, w3, w2, g_final):
    depth = w_ada.shape[0]
    assert depth == 1, "the residual stream of a deeper stack would have to be threaded through the layers"
    n_heads, hd = cache_k.shape[3], cache_k.shape[4]
    da = n_heads * hd
    d = x_prompt.shape[-1]
    ds = d - da
    assert da == ds
    n_groups_ssm = state_ssm_re.shape[2]
    n_expert_groups = w_rg.shape[-1]
    n_experts = w_re.shape[-1]
    assert n_expert_groups + n_experts <= LANES and n_heads <= LANES
    l = 0
    bp = x_prompt.shape[0]
    mod = _ada(jnp.concatenate([c_prompt, c_sample], axis=0).astype(F32), w_ada[l], b_ada[l])
    wi = w_in[l]
    w4 = jnp.concatenate([wi[:, :3 * da], wi[:, 3 * da + n_heads:]], axis=1).astype(BF16)
    wf = jnp.pad(wi[:, 3 * da:3 * da + n_heads], ((0, 0), (0, LANES - n_heads))).astype(BF16)
    bfp = jnp.pad(b_f[l], (0, LANES - n_heads)).reshape(1, LANES).astype(F32)
    wr = jnp.pad(jnp.concatenate([w_rg[l], w_re[l]], axis=1),
                 ((0, 0), (0, LANES - n_expert_groups - n_experts))).astype(F32)
    wr_hi = wr.astype(BF16)
    wr_lo = (wr - wr_hi.astype(F32)).astype(BF16)
    wr2 = jnp.concatenate([wr_hi, wr_lo], axis=1)
    wo = _to_bf16(w_out[l])
    wts = (g_mix[l], w4, wf, bfp, d_skip[l], _to_bf16(w_glu[l]), b_glu[l], g_att[l], g_ssm[l],
           wo[:da], wo[da:], g_ffn[l], wr2, _to_bf16(w1[l]), _to_bf16(w3[l]), _to_bf16(w2[l]),
           g_final, n_heads, hd, n_groups_ssm, n_expert_groups)
    s5_args = (lam_re[l].astype(F32), lam_im[l].astype(F32), log_dt[l], b_re[l].astype(F32), b_im[l].astype(F32),
               c_re[l].astype(F32), c_im[l].astype(F32))
    seq_p = x_prompt.shape[1]
    ops_p = _s5_operators(*s5_args, S5_CHUNK, seq_p // (S5_SEGMENTS * S5_CHUNK))
    ops_s = _s5_operators(*s5_args, x_sample.shape[1], 1)
    yp, kp, vp, lfp, rep, imp = _trunk(x_prompt.astype(F32), mod[:bp], None, wts, ops_p)
    cache = (cache_k[l], cache_v[l], cache_logf[l], state_ssm_re[l], state_ssm_im[l])
    ysm, ksm, vsm, lfs, res, ims = _trunk(x_sample.astype(F32), mod[bp:], cache, wts, ops_s)
    return (yp, ysm, kp[None], vp[None], lfp[None], rep[None], imp[None],
            ksm[None], vsm[None], lfs[None], res[None], ims[None])
```

```python
import functools
import math

import jax
import jax.numpy as jnp
from jax import lax
from jax.experimental import pallas as pl
from jax.experimental.pallas import tpu as pltpu

F32 = jnp.float32
BF16 = jnp.bfloat16
EPS = 1e-6
NEG = -1e30
LOG2E = math.log2(math.e)
LANES = 128
SUBLANES = 8
VMEM_LIMIT = 56 * 1024 * 1024
HIGHEST = lax.Precision.HIGHEST
NT_DIMS = (((1,), (1,)), ((), ()))


def _cparams(*sem):
    return pltpu.CompilerParams(dimension_semantics=sem, vmem_limit_bytes=VMEM_LIMIT)


def _tile(n, pref):
    t = min(n, pref)
    assert n % t == 0, (n, pref)
    return t


def _rms(x, g):
    return x * lax.rsqrt(jnp.mean(x * x, axis=-1, keepdims=True) + EPS) * g


def _mod_operand(vec, seq_len, tm):
    n_seq, d = vec.shape
    if seq_len % tm == 0:
        per = seq_len // tm
        return vec[:, None, :], pl.BlockSpec((None, 1, d), lambda i, *_: (i // per, 0, 0))
    assert tm % seq_len == 0
    rows = jnp.repeat(vec, seq_len, axis=0).reshape(-1, tm, d)
    return rows, pl.BlockSpec((None, tm, d), lambda i, *_: (i, 0, 0))


def _cast_kernel(x_ref, o_ref):
    o_ref[...] = x_ref[...].astype(BF16)


def _to_bf16(w):
    n = w.shape[-1]
    w2 = w.reshape(-1, n)
    rows = w2.shape[0]
    tr = _tile(rows, max(SUBLANES, (1 << 20) // n))
    out = pl.pallas_call(
        _cast_kernel,
        out_shape=jax.ShapeDtypeStruct((rows, n), BF16),
        grid=(rows // tr,),
        in_specs=[pl.BlockSpec((tr, n), lambda i: (i, 0))],
        out_specs=pl.BlockSpec((tr, n), lambda i: (i, 0)),
        compiler_params=_cparams("arbitrary"),
        name="cast_bf16",
    )(w2)
    return out.reshape(w.shape)


def _ada_kernel(c_ref, w_ref, b_ref, o_ref):
    c = c_ref[...]
    a = (c * jax.nn.sigmoid(c)).astype(BF16)
    o_ref[...] = jnp.dot(a, w_ref[...].astype(BF16), preferred_element_type=F32) + b_ref[...]


def _ada(c, w, b):
    s, d = c.shape
    n = w.shape[1]
    tn = _tile(n, 1024)
    return pl.pallas_call(
        _ada_kernel,
        out_shape=jax.ShapeDtypeStruct((s, n), F32),
        grid=(n // tn,),
        in_specs=[pl.BlockSpec((s, d), lambda j: (0, 0)),
                  pl.BlockSpec((d, tn), lambda j: (0, j)),
                  pl.BlockSpec((1, tn), lambda j: (0, j))],
        out_specs=pl.BlockSpec((s, tn), lambda j: (0, j)),
        compiler_params=_cparams("arbitrary"),
        name="ada_mod",
    )(c, w, b.reshape(1, n))


def _inproj_kernel(x_ref, g_ref, sh_ref, sc_ref, w_ref, wf_ref, bf_ref,
                   q_ref, kf_ref, vf_ref, kb_ref, vb_ref, u_ref, lf_ref, h_scr, *, qscale):
    j = pl.program_id(1)

    @pl.when(j == 0)
    def _():
        h = _rms(x_ref[...], g_ref[...]) * (1.0 + sc_ref[...]) + sh_ref[...]
        hb = h.astype(BF16)
        h_scr[...] = hb
        fg = jnp.dot(hb, wf_ref[...], preferred_element_type=F32) + bf_ref[...]
        lf_ref[...] = jnp.minimum(fg, 0.0) - jnp.log1p(jnp.exp(-jnp.abs(fg)))

    p = jnp.dot(h_scr[...], w_ref[...], preferred_element_type=F32)

    @pl.when(j == 0)
    def _():
        q_ref[...] = (p * qscale).astype(BF16)

    @pl.when(j == 1)
    def _():
        kf_ref[...] = p
        kb_ref[...] = p.astype(BF16)

    @pl.when(j == 2)
    def _():
        vf_ref[...] = p
        vb_ref[...] = p.astype(BF16)

    @pl.when(j == 3)
    def _():
        u_ref[...] = p


def _inproj(x, g, sh, sc, w4, wf, bfp, seq_len, qscale):
    t, d = x.shape
    da = w4.shape[1] // 4
    tm = _tile(t, 512)
    sh_op, sh_spec = _mod_operand(sh, seq_len, tm)
    sc_op, sc_spec = _mod_operand(sc, seq_len, tm)
    row = lambda i, j: (i, 0)
    outs = pl.pallas_call(
        functools.partial(_inproj_kernel, qscale=qscale),
        out_shape=(jax.ShapeDtypeStruct((t, da), BF16),
                   jax.ShapeDtypeStruct((t, da), F32), jax.ShapeDtypeStruct((t, da), F32),
                   jax.ShapeDtypeStruct((t, da), BF16), jax.ShapeDtypeStruct((t, da), BF16),
                   jax.ShapeDtypeStruct((t, da), F32),
                   jax.ShapeDtypeStruct((t, LANES), F32)),
        grid=(t // tm, 4),
        in_specs=[pl.BlockSpec((tm, d), row),
                  pl.BlockSpec((1, d), lambda i, j: (0, 0)),
                  sh_spec, sc_spec,
                  pl.BlockSpec((d, da), lambda i, j: (0, j)),
                  pl.BlockSpec((d, LANES), lambda i, j: (0, 0)),
                  pl.BlockSpec((1, LANES), lambda i, j: (0, 0))],
        out_specs=(pl.BlockSpec((tm, da), row),) * 6 + (pl.BlockSpec((tm, LANES), row),),
        scratch_shapes=[pltpu.VMEM((tm, d), BF16)],
        compiler_params=_cparams("arbitrary", "arbitrary"),
        name="in_proj",
    )(x, g.reshape(1, d), sh_op, sc_op, w4, wf, bfp)
    return outs


def _cumsum_kernel(x_ref, o_ref):
    sb, nb, _ = x_ref.shape
    li = lax.broadcasted_iota(jnp.int32, (LANES, LANES), 0)
    lj = lax.broadcasted_iota(jnp.int32, (LANES, LANES), 1)
    upper = (li <= lj).astype(F32)
    ri = lax.broadcasted_iota(jnp.int32, (nb, nb), 0)
    rj = lax.broadcasted_iota(jnp.int32, (nb, nb), 1)
    strict = (rj < ri).astype(F32)
    for s in range(sb):
        within = jnp.dot(x_ref[s], upper, precision=HIGHEST, preferred_element_type=F32)
        tot = jnp.broadcast_to(within[:, LANES - 1:LANES], (nb, LANES))
        off = jnp.dot(strict, tot, precision=HIGHEST, preferred_element_type=F32)
        o_ref[s] = within + off


def _cumsum_rows(x):
    n_rows, n = x.shape
    nb = -(-n // (LANES * SUBLANES)) * SUBLANES
    xp = jnp.pad(x, ((0, 0), (0, nb * LANES - n))).reshape(n_rows, nb, LANES)
    sb = _tile(n_rows, 16)
    out = pl.pallas_call(
        _cumsum_kernel,
        out_shape=jax.ShapeDtypeStruct((n_rows, nb, LANES), F32),
        grid=(n_rows // sb,),
        in_specs=[pl.BlockSpec((sb, nb, LANES), lambda i: (i, 0, 0))],
        out_specs=pl.BlockSpec((sb, nb, LANES), lambda i: (i, 0, 0)),
        compiler_params=_cparams("arbitrary"),
        name="logf_cumsum",
    )(xp)
    return out.reshape(n_rows, nb * LANES)


FOX_TQ = 1024
FOX_SUB = 512


def _col_from_row(row):
    n = row.shape[1]
    eye = lax.broadcasted_iota(jnp.int32, (n, n), 0) == lax.broadcasted_iota(jnp.int32, (n, n), 1)
    return jnp.sum(jnp.where(eye, jnp.broadcast_to(row, (n, n)), 0.0), axis=1, keepdims=True)


def _lane_tiles(x):
    return [x[:, j * LANES:(j + 1) * LANES] for j in range(x.shape[1] // LANES)]


def _fox_kernel(q_ref, k_ref, v_ref, f_ref, o_ref, m_scr, l_scr, acc_scr, *, tq, sub):
    qi = pl.program_id(2)
    nsub = tq // sub
    fqb = [jnp.broadcast_to(_col_from_row(f_ref[qi * nsub + a]), (sub, LANES)) for a in range(nsub)]
    m_scr[...] = jnp.full(m_scr.shape, NEG, F32)
    l_scr[...] = jnp.zeros(l_scr.shape, F32)
    acc_scr[...] = jnp.zeros(acc_scr.shape, F32)

    def chain(a, k, v, fk, diagonal):
        rows = pl.ds(a * sub, sub)
        t1 = lax.dot_general(q_ref[rows, :], k, NT_DIMS, preferred_element_type=F32) - fk
        if diagonal:
            row = lax.broadcasted_iota(jnp.int32, t1.shape, 0)
            col = lax.broadcasted_iota(jnp.int32, t1.shape, 1)
            t1 = jnp.where(col <= row, t1, NEG)
        tiles = _lane_tiles(t1)
        part = functools.reduce(jnp.maximum, tiles)
        m_prev = m_scr[rows, :]
        m_new = jnp.maximum(m_prev, jnp.max(part, axis=-1, keepdims=True) + fqb[a])
        c = m_new - fqb[a]
        p = [jnp.exp2(t - c) for t in tiles]
        alpha = jnp.exp2(m_prev - m_new)
        l_scr[rows, :] = alpha * l_scr[rows, :] + functools.reduce(jnp.add, p)
        pv = jnp.dot(jnp.concatenate(p, axis=1).astype(BF16), v, preferred_element_type=F32)
        acc_scr[rows, :] = alpha * acc_scr[rows, :] + pv
        m_scr[rows, :] = m_new

    def full_step(kt, carry):
        ks = pl.multiple_of(kt * tq, tq)
        k = k_ref[pl.ds(ks, tq), :]
        v = v_ref[pl.ds(ks, tq), :]
        fk = jnp.concatenate([f_ref[kt * nsub + j] for j in range(nsub)], axis=1)
        for a in range(nsub):
            chain(a, k, v, fk, False)
        return carry

    lax.fori_loop(0, qi, full_step, 0)
    for a in range(nsub):
        for j in range(a + 1):
            ks = pl.multiple_of((qi * nsub + j) * sub, sub)
            chain(a, k_ref[pl.ds(ks, sub), :], v_ref[pl.ds(ks, sub), :], f_ref[qi * nsub + j], j == a)
    o_ref[...] = acc_scr[...] / jnp.sum(l_scr[...], axis=-1, keepdims=True)


def _fox_prompt(qb, kb, vb, fcum2, bsz, seq, n_heads, hd):
    assert hd == LANES
    tq = _tile(seq, FOX_TQ)
    sub = _tile(tq, FOX_SUB)
    nq = seq // tq
    f = fcum2.reshape(bsz, n_heads, seq // sub, 1, sub)
    return pl.pallas_call(
        functools.partial(_fox_kernel, tq=tq, sub=sub),
        out_shape=jax.ShapeDtypeStruct((bsz * seq, n_heads * hd), F32),
        grid=(bsz, n_heads, nq),
        in_specs=[pl.BlockSpec((tq, hd), lambda b, h, i: (b * nq + i, h)),
                  pl.BlockSpec((seq, hd), lambda b, h, i: (b, h)),
                  pl.BlockSpec((seq, hd), lambda b, h, i: (b, h)),
                  pl.BlockSpec((None, None, seq // sub, 1, sub), lambda b, h, i: (b, h, 0, 0, 0))],
        out_specs=pl.BlockSpec((tq, hd), lambda b, h, i: (b * nq + i, h)),
        scratch_shapes=[pltpu.VMEM((tq, LANES), F32), pltpu.VMEM((tq, LANES), F32), pltpu.VMEM((tq, hd), F32)],
        compiler_params=_cparams("arbitrary", "arbitrary", "arbitrary"),
        name="fox_prompt",
    )(qb, kb, vb, f)


def _fox_sample_kernel(q_ref, ck_ref, cv_ref, kn_ref, vn_ref, fq_ref, fkc_ref, fkn_ref,
                       rh_ref, ri_ref, lh_ref, lhn_ref, kin_ref, o_ref, m_scr, l_scr, acc_scr):
    kt = pl.program_id(1)

    @pl.when(kt == 0)
    def _():
        m_scr[...] = jnp.full(m_scr.shape, NEG, F32)
        l_scr[...] = jnp.zeros(l_scr.shape, F32)
        acc_scr[...] = jnp.zeros(acc_scr.shape, F32)

    q = q_ref[...]
    fq = fq_ref[...]
    n_heads = ck_ref.shape[1]
    s_new = q.shape[0] // n_heads

    def update(k4, v4, t1_of):
        n, h, d = k4.shape
        k2 = k4.reshape(n * h, d).astype(BF16)
        v2 = v4.reshape(n * h, d).astype(BF16)
        t1 = t1_of(lax.dot_general(q, k2, NT_DIMS, preferred_element_type=F32))
        m_prev = m_scr[...]
        m_new = jnp.maximum(m_prev, jnp.max(t1, axis=-1, keepdims=True) + fq)
        alpha = jnp.exp2(m_prev - m_new)
        p = jnp.exp2(t1 - (m_new - fq))
        l_scr[...] = alpha * l_scr[...] + jnp.sum(p, axis=-1, keepdims=True)
        acc_scr[...] = alpha * acc_scr[...] + jnp.dot(p.astype(BF16), v2, preferred_element_type=F32)
        m_scr[...] = m_new

    def cache_t1(s):
        head = lax.broadcasted_iota(jnp.int32, (n_heads, 1), 0)
        fkm = jnp.where(lh_ref[...] == head, fkc_ref[...], -NEG)
        return jnp.concatenate([s[h * s_new:(h + 1) * s_new] - fkm[h:h + 1] for h in range(n_heads)], axis=0)

    update(ck_ref[...], cv_ref[...], cache_t1)

    @pl.when(kt == pl.num_programs(1) - 1)
    def _():
        causal_head = jnp.where(kin_ref[...] <= ri_ref[...], lhn_ref[...], -1)
        update(kn_ref[...], vn_ref[...],
               lambda s: jnp.where(rh_ref[...] == causal_head, s - fkn_ref[...], NEG))
        o_ref[...] = acc_scr[...] / l_scr[...]


def _fox_sample(qb, k_new, v_new, f_all2, cache_k, cache_v):
    bsz, s_new, n_heads, hd = qb.shape
    past = cache_k.shape[1]
    tk = _tile(past, 1024)
    n_rows = n_heads * s_new
    assert n_rows % 16 == 0 and s_new % SUBLANES == 0
    q2 = jnp.swapaxes(qb, 1, 2).reshape(bsz, n_rows, hd)
    fq = f_all2[:, :, past:].reshape(bsz, n_rows, 1)
    fkc = jnp.swapaxes(f_all2[:, :, :past], 1, 2).reshape(bsz, 1, past * n_heads)
    fkn = jnp.swapaxes(f_all2[:, :, past:], 1, 2).reshape(bsz, 1, s_new * n_heads)
    r = jnp.arange(n_rows, dtype=jnp.int32).reshape(n_rows, 1)
    lane = lambda n: jnp.arange(n * n_heads, dtype=jnp.int32).reshape(1, n * n_heads)
    const = lambda a: pl.BlockSpec(a.shape, lambda b, j: (0,) * a.ndim)
    consts = (r // s_new, r % s_new, lane(tk) % n_heads, lane(s_new) % n_heads, lane(s_new) // n_heads)
    out = pl.pallas_call(
        _fox_sample_kernel,
        out_shape=jax.ShapeDtypeStruct((bsz, n_rows, hd), F32),
        grid=(bsz, past // tk),
        in_specs=[pl.BlockSpec((None, n_rows, hd), lambda b, j: (b, 0, 0)),
                  pl.BlockSpec((None, tk, n_heads, hd), lambda b, j: (b, j, 0, 0)),
                  pl.BlockSpec((None, tk, n_heads, hd), lambda b, j: (b, j, 0, 0)),
                  pl.BlockSpec((None, s_new, n_heads, hd), lambda b, j: (b, 0, 0, 0)),
                  pl.BlockSpec((None, s_new, n_heads, hd), lambda b, j: (b, 0, 0, 0)),
                  pl.BlockSpec((None, n_rows, 1), lambda b, j: (b, 0, 0)),
                  pl.BlockSpec((None, 1, tk * n_heads), lambda b, j: (b, 0, j)),
                  pl.BlockSpec((None, 1, s_new * n_heads), lambda b, j: (b, 0, 0))]
                 + [const(a) for a in consts],
        out_specs=pl.BlockSpec((None, n_rows, hd), lambda b, j: (b, 0, 0)),
        scratch_shapes=[pltpu.VMEM((n_rows, 1), F32), pltpu.VMEM((n_rows, 1), F32), pltpu.VMEM((n_rows, hd), F32)],
        compiler_params=_cparams("arbitrary", "arbitrary"),
        name="fox_sample",
    )(q2, cache_k, cache_v, k_new, v_new, fq, fkc, fkn, *consts)
    return jnp.swapaxes(out.reshape(bsz, n_heads, s_new, hd), 1, 2)


def _s5_operators(lam_re, lam_im, log_dt, b_re, b_im, c_re, c_im, lc, n_steps):
    g, p = lam_re.shape
    hc = b_re.shape[2]
    dt = jnp.exp(log_dt.astype(F32))[:, None]

    def power(k):
        mag = jnp.exp(lam_re * dt * k)
        return mag * jnp.cos(lam_im * dt * k), mag * jnp.sin(lam_im * dt * k)

    lbr, lbi = power(1.0)
    den = lam_re * lam_re + lam_im * lam_im
    fr = ((lbr - 1.0) * lam_re + lbi * lam_im) / den
    fi = (lbi * lam_re - (lbr - 1.0) * lam_im) / den
    bbr = fr[:, :, None] * b_re - fi[:, :, None] * b_im
    bbi = fr[:, :, None] * b_im + fi[:, :, None] * b_re
    ks = jnp.arange(lc + 1, dtype=F32)[None, :, None]
    mag = jnp.exp(lam_re[:, None, :] * dt[:, None, :] * ks)
    ang = lam_im[:, None, :] * dt[:, None, :] * ks
    pwr, pwi = mag * jnp.cos(ang), mag * jnp.sin(ang)
    cr, ci = jnp.swapaxes(c_re, 1, 2), jnp.swapaxes(c_im, 1, 2)
    d_r = bbr[:, :, :, None] * cr[:, :, None, :] - bbi[:, :, :, None] * ci[:, :, None, :]
    d_i = bbr[:, :, :, None] * ci[:, :, None, :] + bbi[:, :, :, None] * cr[:, :, None, :]
    kern = (jnp.einsum("gtp,gpab->gtab", pwr[:, :lc], d_r, precision=HIGHEST)
            - jnp.einsum("gtp,gpab->gtab", pwi[:, :lc], d_i, precision=HIGHEST))
    s_idx = jnp.arange(lc)[:, None]
    t_idx = jnp.arange(lc)[None, :]
    lag = jnp.clip(t_idx - s_idx, 0, lc - 1)
    toe = jnp.where((t_idx >= s_idx)[None, :, :, None, None], kern[:, lag], 0.0)
    toe = jnp.transpose(toe, (0, 1, 3, 2, 4)).reshape(g, lc * hc, lc * hc)
    rev_r, rev_i = pwr[:, lc - 1::-1][:, :lc], pwi[:, lc - 1::-1][:, :lc]
    w_r = rev_r[:, :, None, :] * jnp.swapaxes(bbr, 1, 2)[:, None] - rev_i[:, :, None, :] * jnp.swapaxes(bbi, 1, 2)[:, None]
    w_i = rev_r[:, :, None, :] * jnp.swapaxes(bbi, 1, 2)[:, None] + rev_i[:, :, None, :] * jnp.swapaxes(bbr, 1, 2)[:, None]
    w_r = w_r.reshape(g, lc * hc, p)
    w_i = w_i.reshape(g, lc * hc, p)
    tw = jnp.concatenate([toe, w_r, w_i, w_i, w_r], axis=-1).astype(BF16)
    nr, ni = pwr[:, 1:], pwi[:, 1:]
    v_r = cr[:, :, None, :] * jnp.swapaxes(nr, 1, 2)[:, :, :, None] - ci[:, :, None, :] * jnp.swapaxes(ni, 1, 2)[:, :, :, None]
    v_i = cr[:, :, None, :] * jnp.swapaxes(ni, 1, 2)[:, :, :, None] + ci[:, :, None, :] * jnp.swapaxes(nr, 1, 2)[:, :, :, None]
    vout = jnp.concatenate([v_r, -v_i], axis=1).reshape(g, 2 * p, lc * hc).astype(BF16)
    ar, ai = pwr[:, lc], pwi[:, lc]
    jr, ji = power(float(lc * n_steps))
    coef = jnp.stack([jnp.concatenate([ar, ar], -1), jnp.concatenate([-ai, ai], -1),
                      jnp.concatenate([jr, jr], -1), jnp.concatenate([-ji, ji], -1)], axis=1)
    coef = jnp.pad(coef, ((0, 0), (0, SUBLANES - 4), (0, 0)))
    return tw, vout, coef


def _s5_kernel(u_ref, tw_ref, v_ref, coef_ref, h0_ref, y_ref, hl_ref, s_scr, hz_scr,
               *, n_steps, rows, n_seg, p2):
    lh = u_ref.shape[1]
    both = jnp.dot(u_ref[...], tw_ref[...], preferred_element_type=F32)
    y_ref[...] = both[:, :lh]
    s_scr[...] = both[:, lh:]
    ca = jnp.broadcast_to(coef_ref[0:1, :], (rows, p2))
    cb = jnp.broadcast_to(coef_ref[1:2, :], (rows, p2))
    caj = coef_ref[2:3, :]
    cbj = coef_ref[3:4, :]

    def scan_zero(j, carry):
        x, xs = carry
        r0 = pl.multiple_of(j * rows, rows)
        hz_scr[pl.ds(r0, rows), :] = x
        s = s_scr[pl.ds(r0, rows), :]
        return ca * x + cb * xs + s[:, :p2], ca * xs - cb * x + s[:, p2:]

    zero = jnp.zeros((rows, p2), F32)
    x_end, xs_end = lax.fori_loop(0, n_steps, scan_zero, (zero, zero))

    h0 = h0_ref[...]
    h0s = pltpu.roll(h0, p2 // 2, axis=1)
    if n_seg == 1:
        e, es = h0, h0s
        hl_ref[...] = caj * e + cbj * es + x_end
    else:
        e_rows, es_rows = [], []
        for b in range(rows // n_seg):
            eb, ebs = h0[b:b + 1], h0s[b:b + 1]
            for sg in range(n_seg):
                r = b * n_seg + sg
                e_rows.append(eb)
                es_rows.append(ebs)
                eb, ebs = (caj * eb + cbj * ebs + x_end[r:r + 1],
                           caj * ebs - cbj * eb + xs_end[r:r + 1])
            hl_ref[b:b + 1, :] = eb
        e = jnp.concatenate(e_rows, axis=0)
        es = jnp.concatenate(es_rows, axis=0)

    def scan_fix(j, carry):
        f, fs = carry
        r0 = pl.multiple_of(j * rows, rows)
        hz_scr[pl.ds(r0, rows), :] = hz_scr[pl.ds(r0, rows), :] + f
        return ca * f + cb * fs, ca * fs - cb * f

    lax.fori_loop(0, n_steps, scan_fix, (e, es))
    y_ref[...] += jnp.dot(hz_scr[...].astype(BF16), v_ref[...], preferred_element_type=F32)


def _s5(u, h0_re, h0_im, ops, lc, n_seg, n_groups):
    tw, vout, coef = ops
    bsz, seq, ds = u.shape
    hc = ds // n_groups
    p2 = vout.shape[1]
    rows = bsz * n_seg
    assert rows % SUBLANES == 0 and seq % (n_seg * lc) == 0
    n_steps = seq // (n_seg * lc)
    m = n_steps * rows
    lh = lc * hc
    ug = u.astype(BF16).reshape(bsz, n_seg, n_steps, lc, n_groups, hc)
    ug = jnp.transpose(ug, (4, 2, 0, 1, 3, 5)).reshape(n_groups, m, lh)
    h0 = jnp.swapaxes(jnp.concatenate([h0_re, h0_im], axis=-1), 0, 1)
    y, hl = pl.pallas_call(
        functools.partial(_s5_kernel, n_steps=n_steps, rows=rows, n_seg=n_seg, p2=p2),
        out_shape=(jax.ShapeDtypeStruct((n_groups, m, lh), F32),
                   jax.ShapeDtypeStruct((n_groups, bsz, p2), F32)),
        grid=(n_groups,),
        in_specs=[pl.BlockSpec((None, m, lh), lambda g: (g, 0, 0)),
                  pl.BlockSpec((None, lh, lh + 2 * p2), lambda g: (g, 0, 0)),
                  pl.BlockSpec((None, p2, lh), lambda g: (g, 0, 0)),
                  pl.BlockSpec((None, SUBLANES, p2), lambda g: (g, 0, 0)),
                  pl.BlockSpec((None, bsz, p2), lambda g: (g, 0, 0))],
        out_specs=(pl.BlockSpec((None, m, lh), lambda g: (g, 0, 0)),
                   pl.BlockSpec((None, bsz, p2), lambda g: (g, 0, 0))),
        scratch_shapes=[pltpu.VMEM((m, 2 * p2), F32), pltpu.VMEM((m, p2), F32)],
        compiler_params=_cparams("arbitrary"),
        name="s5_scan",
    )(ug, tw, vout, coef, h0)
    y = y.reshape(n_groups, n_steps, bsz, n_seg, lc, hc)
    y = jnp.transpose(y, (2, 3, 1, 4, 0, 5)).reshape(bsz, seq, ds)
    hl = jnp.swapaxes(hl, 0, 1)
    return y, hl[:, :, :p2 // 2], hl[:, :, p2 // 2:]


def _glu_kernel(y_ref, u_ref, d_ref, w_ref, b_ref, g_ref, o_ref):
    z = jax.nn.gelu(y_ref[...] + d_ref[...] * u_ref[...])
    gate = jax.nn.sigmoid(jnp.dot(z.astype(BF16), w_ref[...], preferred_element_type=F32) + b_ref[...])
    o_ref[...] = _rms(z * gate, g_ref[...]).astype(BF16)


def _glu(y, u, d_skip, w_glu, b_glu, g_ssm):
    t, ds = y.shape
    tm = _tile(t, 512)
    vec = pl.BlockSpec((1, ds), lambda i: (0, 0))
    row = pl.BlockSpec((tm, ds), lambda i: (i, 0))
    return pl.pallas_call(
        _glu_kernel,
        out_shape=jax.ShapeDtypeStruct((t, ds), BF16),
        grid=(t // tm,),
        in_specs=[row, row, vec, pl.BlockSpec((ds, ds), lambda i: (0, 0)), vec, vec],
        out_specs=row,
        compiler_params=_cparams("arbitrary"),
        name="s5_glu",
    )(y, u, d_skip.reshape(1, ds), w_glu, b_glu.reshape(1, ds), g_ssm.reshape(1, ds))


def _outproj_kernel(att_ref, ssm_ref, x_ref, wa_ref, ws_ref, ga_ref, gt_ref, gf_ref, sh_ref, sc_ref, wr_ref,
                    *rest):
    x1_ref, h2_ref, lg_ref = rest[-3:]
    a = _rms(att_ref[...], ga_ref[...]).astype(BF16)
    mixed = (jnp.dot(a, wa_ref[...], preferred_element_type=F32)
             + jnp.dot(ssm_ref[...], ws_ref[...], preferred_element_type=F32))
    x1 = x_ref[...] + gt_ref[...] * mixed
    x1_ref[...] = x1
    h2 = _rms(x1, gf_ref[...]) * (1.0 + sc_ref[...]) + sh_ref[...]
    hi = h2.astype(BF16)
    h2_ref[...] = hi
    lo = (h2 - hi.astype(F32)).astype(BF16)
    r = (jnp.dot(hi, wr_ref[...], preferred_element_type=F32)
         + jnp.dot(lo, wr_ref[...], preferred_element_type=F32))
    lg_ref[...] = r[:, :LANES] + r[:, LANES:]


OUTPROJ_TM = 256


def _outproj(att, ssm_n, x, wa, ws, g_att, gt1, g_ffn, sh2, sc2, wr, seq_len, t_all, row_off, shared):
    t, d = x.shape
    da, ds = att.shape[1], ssm_n.shape[1]
    tm = OUTPROJ_TM
    assert t % tm == 0 and row_off % tm == 0 and t_all % tm == 0
    off = row_off // tm
    gt_op, gt_spec = _mod_operand(gt1, seq_len, tm)
    sh_op, sh_spec = _mod_operand(sh2, seq_len, tm)
    sc_op, sc_spec = _mod_operand(sc2, seq_len, tm)
    row = lambda n: pl.BlockSpec((tm, n), lambda i: (i, 0))
    row_shared = lambda n: pl.BlockSpec((tm, n), lambda i: (i + off, 0))
    const = lambda a, b: pl.BlockSpec((a, b), lambda i: (0, 0))
    operands = [att, ssm_n, x, wa, ws, g_att.reshape(1, da), gt_op, g_ffn.reshape(1, d), sh_op, sc_op, wr]
    in_specs = [row(da), row(ds), row(d), const(da, d), const(ds, d), const(1, da),
                gt_spec, const(1, d), sh_spec, sc_spec, const(d, 2 * LANES)]
    aliases = {}
    if shared is not None:
        aliases = {len(operands): 1, len(operands) + 1: 2}
        operands += list(shared)
        in_specs += [pl.BlockSpec(memory_space=pl.ANY)] * 2
    x1, h2, lg = pl.pallas_call(
        _outproj_kernel,
        out_shape=(jax.ShapeDtypeStruct((t, d), F32), jax.ShapeDtypeStruct((t_all, d), BF16),
                   jax.ShapeDtypeStruct((t_all, LANES), F32)),
        grid=(t // tm,),
        in_specs=in_specs,
        out_specs=(row(d), row_shared(d), row_shared(LANES)),
        input_output_aliases=aliases,
        compiler_params=_cparams("arbitrary"),
        name="out_proj",
    )(*operands)
    return x1, (h2, lg)


def _expert_kernel(te_ref, nu_ref, xs_ref, rw_ref, w1_ref, w3_ref, w2_ref, o_ref, w1_scr, w3_scr, w2_scr):
    i = pl.program_id(0)
    live = i < nu_ref[0]

    @pl.when(live & ((i == 0) | (te_ref[i] != te_ref[jnp.maximum(i - 1, 0)])))
    def _():
        w1_scr[...] = w1_ref[...].astype(BF16)
        w3_scr[...] = w3_ref[...].astype(BF16)
        w2_scr[...] = w2_ref[...].astype(BF16)

    @pl.when(live)
    def _():
        x = xs_ref[...]
        h1 = jnp.dot(x, w1_scr[...], preferred_element_type=F32)
        h3 = jnp.dot(x, w3_scr[...], preferred_element_type=F32)
        hid = (h1 * jax.nn.sigmoid(h1) * h3).astype(BF16)
        o_ref[...] = jnp.dot(hid, w2_scr[...], preferred_element_type=F32) * rw_ref[...]

    @pl.when(i >= nu_ref[0])
    def _():
        o_ref[...] = jnp.zeros(o_ref.shape, F32)


def _route(logits, n_groups, n_experts, top_k, tm):
    t = logits.shape[0]
    epg = n_experts // n_groups
    tok = jnp.arange(t)
    g_logits = logits[:, :n_groups]
    p_group = jax.nn.softmax(g_logits, axis=-1)
    g_sel = jnp.argmax(g_logits, axis=-1)
    e_sel = logits[:, n_groups:n_groups + n_experts].reshape(t, n_groups, epg)[tok, g_sel]
    top_v, top_i = lax.top_k(e_sel, top_k)
    gate = p_group[tok, g_sel][:, None] * jax.nn.softmax(top_v, axis=-1)
    eid = (g_sel[:, None] * epg + top_i).reshape(-1).astype(jnp.int32)
    n_asg = t * top_k
    onehot = (eid[:, None] == jnp.arange(n_experts, dtype=jnp.int32)[None, :]).astype(jnp.int32)
    csum = jnp.cumsum(onehot, axis=0)
    rank = jnp.take_along_axis(csum, eid[:, None], axis=1)[:, 0] - 1
    counts = csum[-1]
    padded = (counts + tm - 1) // tm * tm
    pends = jnp.cumsum(padded)
    dest = (pends - padded)[eid] + rank
    n_pad = (n_asg + n_experts * (tm - 1) + tm - 1) // tm * tm
    row_asg = jnp.full((n_pad,), -1, jnp.int32).at[dest].set(jnp.arange(n_asg, dtype=jnp.int32))
    live = row_asg >= 0
    row_tok = jnp.where(live, row_asg // top_k, 0)
    row_w = jnp.where(live, gate.reshape(-1)[jnp.maximum(row_asg, 0)], 0.0)
    n_tiles = n_pad // tm
    tile_e = jnp.sum(pends[None, :] <= (jnp.arange(n_tiles, dtype=jnp.int32) * tm)[:, None], axis=1)
    tile_e = jnp.minimum(tile_e, n_experts - 1).astype(jnp.int32)
    n_used = (pends[-1] // tm).astype(jnp.int32).reshape(1)
    return row_tok, row_w, tile_e, n_used, dest.reshape(t, top_k)


def _experts(h2, logits, w1, w3, w2, n_groups, top_k):
    t, d = h2.shape
    n_experts, _, de = w1.shape
    tm = 256
    row_tok, row_w, tile_e, n_used, pos = _route(logits, n_groups, n_experts, top_k, tm)
    n_pad = row_tok.shape[0]
    xs = h2[row_tok]
    single = pl.Buffered(1)
    ys = pl.pallas_call(
        _expert_kernel,
        out_shape=jax.ShapeDtypeStruct((n_pad, d), F32),
        grid_spec=pltpu.PrefetchScalarGridSpec(
            num_scalar_prefetch=2,
            grid=(n_pad // tm,),
            in_specs=[pl.BlockSpec((tm, d), lambda i, te, nu: (i, 0)),
                      pl.BlockSpec((tm, 1), lambda i, te, nu: (i, 0)),
                      pl.BlockSpec((None, d, de), lambda i, te, nu: (te[i], 0, 0), pipeline_mode=single),
                      pl.BlockSpec((None, d, de), lambda i, te, nu: (te[i], 0, 0), pipeline_mode=single),
                      pl.BlockSpec((None, de, d), lambda i, te, nu: (te[i], 0, 0), pipeline_mode=single)],
            out_specs=pl.BlockSpec((tm, d), lambda i, te, nu: (i, 0)),
            scratch_shapes=[pltpu.VMEM((d, de), BF16), pltpu.VMEM((d, de), BF16), pltpu.VMEM((de, d), BF16)]),
        compiler_params=_cparams("arbitrary"),
        name="moe_experts",
    )(tile_e, n_used, xs, row_w.reshape(n_pad, 1), w1, w3, w2)
    return ys, pos


def _final_kernel(x_ref, gt_ref, g_ref, *rest):
    *y_refs, o_ref = rest
    moe = y_refs[0][...]
    for r in y_refs[1:]:
        moe = moe + r[...]
    o_ref[...] = _rms(x_ref[...] + gt_ref[...] * moe, g_ref[...])


def _final(x1, gt2, g_final, ys, seq_len):
    t, d = x1.shape
    tm = _tile(t, 256)
    gt_op, gt_spec = _mod_operand(gt2, seq_len, tm)
    row = pl.BlockSpec((tm, d), lambda i: (i, 0))
    return pl.pallas_call(
        _final_kernel,
        out_shape=jax.ShapeDtypeStruct((t, d), F32),
        grid=(t // tm,),
        in_specs=[row, gt_spec, pl.BlockSpec((1, d), lambda i: (0, 0))] + [row] * len(ys),
        out_specs=row,
        compiler_params=_cparams("arbitrary"),
        name="moe_combine_norm",
    )(x1, gt_op, g_final.reshape(1, d), *ys)


S5_CHUNK = 16
S5_SEGMENTS = 4
TOP_K = 2


def _mixers(x, mod, cache, wts, s5_ops, t_all, row_off, shared):
    (g_mix, w4, wf, bfp, d_skip, w_glu, b_glu, g_att, g_ssm, wa, ws, g_ffn, wr,
     n_heads, hd, n_groups_ssm) = wts
    bsz, seq, d = x.shape
    da = n_heads * hd
    t = bsz * seq
    xt = x.reshape(t, d)
    sh1, sc1, gt1, sh2, sc2, gt2 = jnp.split(mod, 6, axis=-1)
    qb, kf, vf, kb, vb, u, lfp = _inproj(xt, g_mix, sh1, sc1, w4, wf, bfp, seq, hd ** -0.5 * LOG2E)
    logf = lfp[:, :n_heads].reshape(bsz, seq, n_heads)
    if cache is None:
        fcum = _cumsum_rows(jnp.swapaxes(logf, 1, 2).reshape(bsz * n_heads, seq))[:, :seq]
        att = _fox_prompt(qb, kb, vb, (fcum * LOG2E).reshape(bsz, n_heads, seq), bsz, seq, n_heads, hd)
        h0 = jnp.zeros((bsz, n_groups_ssm, s5_ops[1].shape[1] // 2), F32)
        ssm_y, h_re, h_im = _s5(u.reshape(bsz, seq, -1), h0, h0, s5_ops, S5_CHUNK, S5_SEGMENTS, n_groups_ssm)
    else:
        cache_k, cache_v, cache_logf, st_re, st_im = cache
        past = cache_k.shape[1]
        lf_all = jnp.concatenate([cache_logf.astype(F32), logf], axis=1)
        f_all = _cumsum_rows(jnp.swapaxes(lf_all, 1, 2).reshape(bsz * n_heads, past + seq))
        f_all = (f_all[:, :past + seq] * LOG2E).reshape(bsz, n_heads, past + seq)
        att = _fox_sample(qb.reshape(bsz, seq, n_heads, hd), kf.reshape(bsz, seq, n_heads, hd),
                          vf.reshape(bsz, seq, n_heads, hd), f_all, cache_k, cache_v)
        att = att.reshape(t, da)
        ssm_y, h_re, h_im = _s5(u.reshape(bsz, seq, -1), st_re.astype(F32), st_im.astype(F32), s5_ops,
                                seq, 1, n_groups_ssm)
    ssm_n = _glu(ssm_y.reshape(t, -1), u, d_skip, w_glu, b_glu, g_ssm)
    x1, shared = _outproj(att, ssm_n, xt, wa, ws, g_att, gt1, g_ffn, sh2, sc2, wr, seq, t_all, row_off, shared)
    new_cache = (kf.reshape(bsz, seq, n_heads, hd), vf.reshape(bsz, seq, n_heads, hd), logf, h_re, h_im)
    return x1, gt2, shared, new_cache


def kernel(x_prompt, x_sample, cache_k, cache_v, cache_logf, state_ssm_re, state_ssm_im, c_prompt, c_sample, w_ada, b_ada, g_mix, w_in, b_f, lam_re, lam_im, log_dt, b_re, b_im, c_re, c_im, d_skip, w_glu, b_glu, g_att, g_ssm, w_out, g_ffn, w_rg, w_re, w1, w3, w2, g_final):
    depth = w_ada.shape[0]
    assert depth == 1, "the residual stream of a deeper stack would have to be threaded through the layers"
    n_heads, hd = cache_k.shape[3], cache_k.shape[4]
    da = n_heads * hd
    d = x_prompt.shape[-1]
    ds = d - da
    assert da == ds
    n_groups_ssm = state_ssm_re.shape[2]
    n_expert_groups = w_rg.shape[-1]
    n_experts = w_re.shape[-1]
    assert n_expert_groups + n_experts <= LANES and n_heads <= LANES
    l = 0
    bp = x_prompt.shape[0]
    mod = _ada(jnp.concatenate([c_prompt, c_sample], axis=0).astype(F32), w_ada[l], b_ada[l])
    wi = w_in[l]
    w4 = jnp.concatenate([wi[:, :3 * da], wi[:, 3 * da + n_heads:]], axis=1).astype(BF16)
    wf = jnp.pad(wi[:, 3 * da:3 * da + n_heads], ((0, 0), (0, LANES - n_heads))).astype(BF16)
    bfp = jnp.pad(b_f[l], (0, LANES - n_heads)).reshape(1, LANES).astype(F32)
    wr = jnp.pad(jnp.concatenate([w_rg[l], w_re[l]], axis=1),
                 ((0, 0), (0, LANES - n_expert_groups - n_experts))).astype(F32)
    wr_hi = wr.astype(BF16)
    wr_lo = (wr - wr_hi.astype(F32)).astype(BF16)
    wr2 = jnp.concatenate([wr_hi, wr_lo], axis=1)
    wo = _to_bf16(w_out[l])
    wts = (g_mix[l], w4, wf, bfp, d_skip[l], _to_bf16(w_glu[l]), b_glu[l], g_att[l], g_ssm[l],
           wo[:da], wo[da:], g_ffn[l], wr2, n_heads, hd, n_groups_ssm)
    s5_args = (lam_re[l].astype(F32), lam_im[l].astype(F32), log_dt[l], b_re[l].astype(F32), b_im[l].astype(F32),
               c_re[l].astype(F32), c_im[l].astype(F32))
    seq_p = x_prompt.shape[1]
    ops_p = _s5_operators(*s5_args, S5_CHUNK, seq_p // (S5_SEGMENTS * S5_CHUNK))
    ops_s = _s5_operators(*s5_args, x_sample.shape[1], 1)
    bs, seq_s = x_sample.shape[:2]
    t_p, t_s = bp * seq_p, bs * seq_s
    x1p, gt2p, shared, (kp, vp, lfp, rep, imp) = _mixers(
        x_prompt.astype(F32), mod[:bp], None, wts, ops_p, t_p + t_s, 0, None)
    cache = (cache_k[l], cache_v[l], cache_logf[l], state_ssm_re[l], state_ssm_im[l])
    x1s, gt2s, (h2, logits), (ksm, vsm, lfs, res, ims) = _mixers(
        x_sample.astype(F32), mod[bp:], cache, wts, ops_s, t_p + t_s, t_p, shared)
    ys, pos = _experts(h2, logits, w1[l], w3[l], w2[l], n_expert_groups, TOP_K)
    yp = _final(x1p, gt2p, g_final, [ys[pos[:t_p, k]] for k in range(TOP_K)], seq_p).reshape(x_prompt.shape)
    ysm = _final(x1s, gt2s, g_final, [ys[pos[t_p:, k]] for k in range(TOP_K)], seq_s).reshape(x_sample.shape)
    return (yp, ysm, kp[None], vp[None], lfp[None], rep[None], imp[None],
            ksm[None], vsm[None], lfs[None], res[None], ims[None])
```

```python
import functools
import math

import jax
import jax.numpy as jnp
from jax import lax
from jax.experimental import pallas as pl
from jax.experimental.pallas import tpu as pltpu

F32 = jnp.float32
BF16 = jnp.bfloat16
EPS = 1e-6
NEG = -1e30
LOG2E = math.log2(math.e)
LANES = 128
SUBLANES = 8
VMEM_LIMIT = 56 * 1024 * 1024
HIGHEST = lax.Precision.HIGHEST
NT_DIMS = (((1,), (1,)), ((), ()))


def _cparams(*sem):
    return pltpu.CompilerParams(dimension_semantics=sem, vmem_limit_bytes=VMEM_LIMIT)


def _tile(n, pref):
    t = min(n, pref)
    assert n % t == 0, (n, pref)
    return t


def _rms(x, g):
    return x * lax.rsqrt(jnp.mean(x * x, axis=-1, keepdims=True) + EPS) * g


def _mod_operand(vec, seq_len, tm):
    n_seq, d = vec.shape
    if seq_len % tm == 0:
        per = seq_len // tm
        return vec[:, None, :], pl.BlockSpec((None, 1, d), lambda i, *_: (i // per, 0, 0))
    assert tm % seq_len == 0
    rows = jnp.repeat(vec, seq_len, axis=0).reshape(-1, tm, d)
    return rows, pl.BlockSpec((None, tm, d), lambda i, *_: (i, 0, 0))


def _cast_kernel(x_ref, o_ref):
    o_ref[...] = x_ref[...].astype(BF16)


def _to_bf16(w):
    n = w.shape[-1]
    w2 = w.reshape(-1, n)
    rows = w2.shape[0]
    tr = _tile(rows, max(SUBLANES, (1 << 20) // n))
    out = pl.pallas_call(
        _cast_kernel,
        out_shape=jax.ShapeDtypeStruct((rows, n), BF16),
        grid=(rows // tr,),
        in_specs=[pl.BlockSpec((tr, n), lambda i: (i, 0))],
        out_specs=pl.BlockSpec((tr, n), lambda i: (i, 0)),
        compiler_params=_cparams("arbitrary"),
        name="cast_bf16",
    )(w2)
    return out.reshape(w.shape)


def _ada_kernel(c_ref, w_ref, b_ref, o_ref):
    c = c_ref[...]
    a = (c * jax.nn.sigmoid(c)).astype(BF16)
    o_ref[...] = jnp.dot(a, w_ref[...].astype(BF16), preferred_element_type=F32) + b_ref[...]


def _ada(c, w, b):
    s, d = c.shape
    n = w.shape[1]
    tn = _tile(n, 1024)
    return pl.pallas_call(
        _ada_kernel,
        out_shape=jax.ShapeDtypeStruct((s, n), F32),
        grid=(n // tn,),
        in_specs=[pl.BlockSpec((s, d), lambda j: (0, 0)),
                  pl.BlockSpec((d, tn), lambda j: (0, j)),
                  pl.BlockSpec((1, tn), lambda j: (0, j))],
        out_specs=pl.BlockSpec((s, tn), lambda j: (0, j)),
        compiler_params=_cparams("arbitrary"),
        name="ada_mod",
    )(c, w, b.reshape(1, n))


def _inproj_kernel(x_ref, g_ref, sh_ref, sc_ref, w_ref, wf_ref, bf_ref,
                   q_ref, kf_ref, vf_ref, kb_ref, vb_ref, u_ref, lf_ref, h_scr, *, qscale):
    j = pl.program_id(1)

    @pl.when(j == 0)
    def _():
        h = _rms(x_ref[...], g_ref[...]) * (1.0 + sc_ref[...]) + sh_ref[...]
        hb = h.astype(BF16)
        h_scr[...] = hb
        fg = jnp.dot(hb, wf_ref[...], preferred_element_type=F32) + bf_ref[...]
        lf_ref[...] = jnp.minimum(fg, 0.0) - jnp.log1p(jnp.exp(-jnp.abs(fg)))

    p = jnp.dot(h_scr[...], w_ref[...], preferred_element_type=F32)

    @pl.when(j == 0)
    def _():
        q_ref[...] = (p * qscale).astype(BF16)

    @pl.when(j == 1)
    def _():
        kf_ref[...] = p
        kb_ref[...] = p.astype(BF16)

    @pl.when(j == 2)
    def _():
        vf_ref[...] = p
        vb_ref[...] = p.astype(BF16)

    @pl.when(j == 3)
    def _():
        for k in range(u_ref.shape[0]):
            u_ref[k] = p[:, k * LANES:(k + 1) * LANES]


def _inproj(x, g, sh, sc, w4, wf, bfp, seq_len, qscale):
    t, d = x.shape
    da = w4.shape[1] // 4
    tm = _tile(t, 512)
    sh_op, sh_spec = _mod_operand(sh, seq_len, tm)
    sc_op, sc_spec = _mod_operand(sc, seq_len, tm)
    row = lambda i, j: (i, 0)
    outs = pl.pallas_call(
        functools.partial(_inproj_kernel, qscale=qscale),
        out_shape=(jax.ShapeDtypeStruct((t, da), BF16),
                   jax.ShapeDtypeStruct((t, da), F32), jax.ShapeDtypeStruct((t, da), F32),
                   jax.ShapeDtypeStruct((t, da), BF16), jax.ShapeDtypeStruct((t, da), BF16),
                   jax.ShapeDtypeStruct((da // LANES, t, LANES), F32),
                   jax.ShapeDtypeStruct((t, LANES), F32)),
        grid=(t // tm, 4),
        in_specs=[pl.BlockSpec((tm, d), row),
                  pl.BlockSpec((1, d), lambda i, j: (0, 0)),
                  sh_spec, sc_spec,
                  pl.BlockSpec((d, da), lambda i, j: (0, j)),
                  pl.BlockSpec((d, LANES), lambda i, j: (0, 0)),
                  pl.BlockSpec((1, LANES), lambda i, j: (0, 0))],
        out_specs=(pl.BlockSpec((tm, da), row),) * 5
                  + (pl.BlockSpec((da // LANES, tm, LANES), lambda i, j: (0, i, 0)), pl.BlockSpec((tm, LANES), row)),
        scratch_shapes=[pltpu.VMEM((tm, d), BF16)],
        compiler_params=_cparams("arbitrary", "arbitrary"),
        name="in_proj",
    )(x, g.reshape(1, d), sh_op, sc_op, w4, wf, bfp)
    return outs


def _cumsum_kernel(x_ref, o_ref):
    sb, nb, _ = x_ref.shape
    li = lax.broadcasted_iota(jnp.int32, (LANES, LANES), 0)
    lj = lax.broadcasted_iota(jnp.int32, (LANES, LANES), 1)
    upper = (li <= lj).astype(F32)
    ri = lax.broadcasted_iota(jnp.int32, (nb, nb), 0)
    rj = lax.broadcasted_iota(jnp.int32, (nb, nb), 1)
    strict = (rj < ri).astype(F32)
    for s in range(sb):
        within = jnp.dot(x_ref[s], upper, precision=HIGHEST, preferred_element_type=F32)
        tot = jnp.broadcast_to(within[:, LANES - 1:LANES], (nb, LANES))
        off = jnp.dot(strict, tot, precision=HIGHEST, preferred_element_type=F32)
        o_ref[s] = within + off


def _cumsum_rows(x):
    n_rows, n = x.shape
    nb = -(-n // (LANES * SUBLANES)) * SUBLANES
    xp = jnp.pad(x, ((0, 0), (0, nb * LANES - n))).reshape(n_rows, nb, LANES)
    sb = _tile(n_rows, 16)
    out = pl.pallas_call(
        _cumsum_kernel,
        out_shape=jax.ShapeDtypeStruct((n_rows, nb, LANES), F32),
        grid=(n_rows // sb,),
        in_specs=[pl.BlockSpec((sb, nb, LANES), lambda i: (i, 0, 0))],
        out_specs=pl.BlockSpec((sb, nb, LANES), lambda i: (i, 0, 0)),
        compiler_params=_cparams("arbitrary"),
        name="logf_cumsum",
    )(xp)
    return out.reshape(n_rows, nb * LANES)


FOX_TQ = 2048
FOX_TK = 1024
FOX_SUB = 512


def _col_from_row(row):
    n = row.shape[1]
    eye = lax.broadcasted_iota(jnp.int32, (n, n), 0) == lax.broadcasted_iota(jnp.int32, (n, n), 1)
    return jnp.sum(jnp.where(eye, jnp.broadcast_to(row, (n, n)), 0.0), axis=1, keepdims=True)


def _lane_tiles(x):
    return [x[:, j * LANES:(j + 1) * LANES] for j in range(x.shape[1] // LANES)]


def _fox_kernel(q_ref, k_ref, v_ref, f_ref, o_ref, m_scr, l_scr, acc_scr, *, tq, tk, sub):
    qi = pl.program_id(2)
    nsub = tq // sub
    fqb = [jnp.broadcast_to(_col_from_row(f_ref[qi * nsub + a]), (sub, LANES)) for a in range(nsub)]
    m_scr[...] = jnp.full(m_scr.shape, NEG, F32)
    l_scr[...] = jnp.zeros(l_scr.shape, F32)
    acc_scr[...] = jnp.zeros(acc_scr.shape, F32)

    def chain(a, k, v, fk, diagonal):
        rows = pl.ds(a * sub, sub)
        t1 = lax.dot_general(q_ref[rows, :], k, NT_DIMS, preferred_element_type=F32) - fk
        if diagonal:
            row = lax.broadcasted_iota(jnp.int32, t1.shape, 0)
            col = lax.broadcasted_iota(jnp.int32, t1.shape, 1)
            t1 = jnp.where(col <= row, t1, NEG)
        tiles = _lane_tiles(t1)
        part = functools.reduce(jnp.maximum, tiles)
        m_prev = m_scr[rows, :]
        m_new = jnp.maximum(m_prev, jnp.max(part, axis=-1, keepdims=True) + fqb[a])
        c = m_new - fqb[a]
        p = [jnp.exp2(t - c) for t in tiles]
        alpha = jnp.exp2(m_prev - m_new)
        l_scr[rows, :] = alpha * l_scr[rows, :] + functools.reduce(jnp.add, p)
        pv = jnp.dot(jnp.concatenate(p, axis=1).astype(BF16), v, preferred_element_type=F32)
        acc_scr[rows, :] = alpha * acc_scr[rows, :] + pv
        m_scr[rows, :] = m_new

    def full_step(kt, carry):
        ks = pl.multiple_of(kt * tk, tk)
        k = k_ref[pl.ds(ks, tk), :]
        v = v_ref[pl.ds(ks, tk), :]
        fk = jnp.concatenate([f_ref[kt * (tk // sub) + j] for j in range(tk // sub)], axis=1)
        for a in range(nsub):
            chain(a, k, v, fk, False)
        return carry

    lax.fori_loop(0, qi * (tq // tk), full_step, 0)
    for a in range(nsub):
        for j in range(a + 1):
            ks = pl.multiple_of((qi * nsub + j) * sub, sub)
            chain(a, k_ref[pl.ds(ks, sub), :], v_ref[pl.ds(ks, sub), :], f_ref[qi * nsub + j], j == a)
    o_ref[...] = acc_scr[...] / jnp.sum(l_scr[...], axis=-1, keepdims=True)


def _fox_prompt(qb, kb, vb, fcum2, bsz, seq, n_heads, hd):
    assert hd == LANES
    tq = _tile(seq, FOX_TQ)
    tk = _tile(tq, FOX_TK)
    sub = _tile(tk, FOX_SUB)
    nq = seq // tq
    f = fcum2.reshape(bsz, n_heads, seq // sub, 1, sub)
    return pl.pallas_call(
        functools.partial(_fox_kernel, tq=tq, tk=tk, sub=sub),
        out_shape=jax.ShapeDtypeStruct((bsz * seq, n_heads * hd), F32),
        grid=(bsz, n_heads, nq),
        in_specs=[pl.BlockSpec((tq, hd), lambda b, h, i: (b * nq + i, h)),
                  pl.BlockSpec((seq, hd), lambda b, h, i: (b, h)),
                  pl.BlockSpec((seq, hd), lambda b, h, i: (b, h)),
                  pl.BlockSpec((None, None, seq // sub, 1, sub), lambda b, h, i: (b, h, 0, 0, 0))],
        out_specs=pl.BlockSpec((tq, hd), lambda b, h, i: (b * nq + i, h)),
        scratch_shapes=[pltpu.VMEM((tq, LANES), F32), pltpu.VMEM((tq, LANES), F32), pltpu.VMEM((tq, hd), F32)],
        compiler_params=_cparams("arbitrary", "arbitrary", "arbitrary"),
        name="fox_prompt",
    )(qb, kb, vb, f)


def _fox_sample_kernel(q_ref, ck_ref, cv_ref, kn_ref, vn_ref, fq_ref, fkc_ref, fkn_ref,
                       rh_ref, ri_ref, lh_ref, lhn_ref, kin_ref, o_ref, m_scr, l_scr, acc_scr):
    kt = pl.program_id(1)

    @pl.when(kt == 0)
    def _():
        m_scr[...] = jnp.full(m_scr.shape, NEG, F32)
        l_scr[...] = jnp.zeros(l_scr.shape, F32)
        acc_scr[...] = jnp.zeros(acc_scr.shape, F32)

    q = q_ref[...]
    fq = fq_ref[...]
    n_heads = ck_ref.shape[1]
    s_new = q.shape[0] // n_heads

    def update(k4, v4, t1_of):
        n, h, d = k4.shape
        k2 = k4.reshape(n * h, d).astype(BF16)
        v2 = v4.reshape(n * h, d).astype(BF16)
        t1 = t1_of(lax.dot_general(q, k2, NT_DIMS, preferred_element_type=F32))
        m_prev = m_scr[...]
        m_new = jnp.maximum(m_prev, jnp.max(t1, axis=-1, keepdims=True) + fq)
        alpha = jnp.exp2(m_prev - m_new)
        p = jnp.exp2(t1 - (m_new - fq))
        l_scr[...] = alpha * l_scr[...] + jnp.sum(p, axis=-1, keepdims=True)
        acc_scr[...] = alpha * acc_scr[...] + jnp.dot(p.astype(BF16), v2, preferred_element_type=F32)
        m_scr[...] = m_new

    def cache_t1(s):
        head = lax.broadcasted_iota(jnp.int32, (n_heads, 1), 0)
        fkm = jnp.where(lh_ref[...] == head, fkc_ref[...], -NEG)
        return jnp.concatenate([s[h * s_new:(h + 1) * s_new] - fkm[h:h + 1] for h in range(n_heads)], axis=0)

    update(ck_ref[...], cv_ref[...], cache_t1)

    @pl.when(kt == pl.num_programs(1) - 1)
    def _():
        causal_head = jnp.where(kin_ref[...] <= ri_ref[...], lhn_ref[...], -1)
        update(kn_ref[...], vn_ref[...],
               lambda s: jnp.where(rh_ref[...] == causal_head, s - fkn_ref[...], NEG))
        o_ref[...] = acc_scr[...] / l_scr[...]


def _fox_sample(qb, k_new, v_new, f_all2, cache_k, cache_v):
    bsz, s_new, n_heads, hd = qb.shape
    past = cache_k.shape[1]
    tk = _tile(past, 1024)
    n_rows = n_heads * s_new
    assert n_rows % 16 == 0 and s_new % SUBLANES == 0
    q2 = jnp.swapaxes(qb, 1, 2).reshape(bsz, n_rows, hd)
    fq = f_all2[:, :, past:].reshape(bsz, n_rows, 1)
    fkc = jnp.swapaxes(f_all2[:, :, :past], 1, 2).reshape(bsz, 1, past * n_heads)
    fkn = jnp.swapaxes(f_all2[:, :, past:], 1, 2).reshape(bsz, 1, s_new * n_heads)
    r = jnp.arange(n_rows, dtype=jnp.int32).reshape(n_rows, 1)
    lane = lambda n: jnp.arange(n * n_heads, dtype=jnp.int32).reshape(1, n * n_heads)
    const = lambda a: pl.BlockSpec(a.shape, lambda b, j: (0,) * a.ndim)
    consts = (r // s_new, r % s_new, lane(tk) % n_heads, lane(s_new) % n_heads, lane(s_new) // n_heads)
    out = pl.pallas_call(
        _fox_sample_kernel,
        out_shape=jax.ShapeDtypeStruct((bsz, n_rows, hd), F32),
        grid=(bsz, past // tk),
        in_specs=[pl.BlockSpec((None, n_rows, hd), lambda b, j: (b, 0, 0)),
                  pl.BlockSpec((None, tk, n_heads, hd), lambda b, j: (b, j, 0, 0)),
                  pl.BlockSpec((None, tk, n_heads, hd), lambda b, j: (b, j, 0, 0)),
                  pl.BlockSpec((None, s_new, n_heads, hd), lambda b, j: (b, 0, 0, 0)),
                  pl.BlockSpec((None, s_new, n_heads, hd), lambda b, j: (b, 0, 0, 0)),
                  pl.BlockSpec((None, n_rows, 1), lambda b, j: (b, 0, 0)),
                  pl.BlockSpec((None, 1, tk * n_heads), lambda b, j: (b, 0, j)),
                  pl.BlockSpec((None, 1, s_new * n_heads), lambda b, j: (b, 0, 0))]
                 + [const(a) for a in consts],
        out_specs=pl.BlockSpec((None, n_rows, hd), lambda b, j: (b, 0, 0)),
        scratch_shapes=[pltpu.VMEM((n_rows, 1), F32), pltpu.VMEM((n_rows, 1), F32), pltpu.VMEM((n_rows, hd), F32)],
        compiler_params=_cparams("arbitrary", "arbitrary"),
        name="fox_sample",
    )(q2, cache_k, cache_v, k_new, v_new, fq, fkc, fkn, *consts)
    return jnp.swapaxes(out.reshape(bsz, n_heads, s_new, hd), 1, 2)


S5_IN_GROUPS = 8
S5_OUT_GROUPS = 16


def _s5_operators(lam_re, lam_im, log_dt, b_re, b_im, c_re, c_im, lc, n_steps):
    g, p = lam_re.shape
    hc = b_re.shape[2]
    gi, go = S5_IN_GROUPS, S5_OUT_GROUPS
    dt = jnp.exp(log_dt.astype(F32))[:, None]

    def power(k):
        mag = jnp.exp(lam_re * dt * k)
        return mag * jnp.cos(lam_im * dt * k), mag * jnp.sin(lam_im * dt * k)

    lbr, lbi = power(1.0)
    den = lam_re * lam_re + lam_im * lam_im
    fr = ((lbr - 1.0) * lam_re + lbi * lam_im) / den
    fi = (lbi * lam_re - (lbr - 1.0) * lam_im) / den
    bbr = fr[:, :, None] * b_re - fi[:, :, None] * b_im
    bbi = fr[:, :, None] * b_im + fi[:, :, None] * b_re
    ks = jnp.arange(lc + 1, dtype=F32)[None, :, None]
    mag = jnp.exp(lam_re[:, None, :] * dt[:, None, :] * ks)
    ang = lam_im[:, None, :] * dt[:, None, :] * ks
    pwr, pwi = mag * jnp.cos(ang), mag * jnp.sin(ang)
    cr, ci = jnp.swapaxes(c_re, 1, 2), jnp.swapaxes(c_im, 1, 2)
    d_r = bbr[:, :, :, None] * cr[:, :, None, :] - bbi[:, :, :, None] * ci[:, :, None, :]
    d_i = bbr[:, :, :, None] * ci[:, :, None, :] + bbi[:, :, :, None] * cr[:, :, None, :]
    kern = (jnp.einsum("gtp,gpab->gtab", pwr[:, :lc], d_r, precision=HIGHEST)
            - jnp.einsum("gtp,gpab->gtab", pwi[:, :lc], d_i, precision=HIGHEST))
    bd = jnp.einsum("sgtab,gk->stgakb", kern.reshape(g // go, go, lc, hc, hc), jnp.eye(go, dtype=F32))
    bd = bd.reshape(g // go, lc, go * hc, go * hc).astype(BF16)
    rev_r, rev_i = pwr[:, lc - 1::-1][:, :lc], pwi[:, lc - 1::-1][:, :lc]
    bt_r, bt_i = jnp.swapaxes(bbr, 1, 2)[:, None], jnp.swapaxes(bbi, 1, 2)[:, None]
    w_r = rev_r[:, :, None, :] * bt_r - rev_i[:, :, None, :] * bt_i
    w_i = rev_r[:, :, None, :] * bt_i + rev_i[:, :, None, :] * bt_r
    eye_i = jnp.eye(gi, dtype=F32)
    place = lambda w: jnp.einsum("sgtap,gk->stgakp", w.reshape(g // gi, gi, lc, hc, p), eye_i).reshape(
        g // gi, lc, gi * hc, gi * p)
    bw = jnp.concatenate([place(w_r), place(w_i)], axis=-1).astype(BF16)
    nr, ni = jnp.swapaxes(pwr[:, 1:], 1, 2), jnp.swapaxes(pwi[:, 1:], 1, 2)
    v_r = cr[:, :, None, :] * nr[:, :, :, None] - ci[:, :, None, :] * ni[:, :, :, None]
    v_i = cr[:, :, None, :] * ni[:, :, :, None] + ci[:, :, None, :] * nr[:, :, :, None]
    halves = go // gi
    eye_h = jnp.eye(halves, dtype=F32)
    place_v = lambda v: jnp.einsum("shgptb,hk,gm->sthgpkmb", v.reshape(g // go, halves, gi, p, lc, hc),
                                   eye_h, eye_i)
    bv = jnp.stack([place_v(v_r), place_v(-v_i)], axis=3)
    bv = bv.reshape(g // go, lc, halves * 2 * gi * p, go * hc).astype(BF16)
    jr, ji = power(float(lc * n_steps))
    coef = jnp.stack([pwr[:, lc], pwi[:, lc], jr, ji], axis=1)
    coef = jnp.swapaxes(coef.reshape(g // gi, gi, 4, p), 1, 2).reshape(g // gi, 4, gi * p)
    coef = jnp.pad(coef, ((0, 0), (0, SUBLANES - 4), (0, 0)))
    return bd, bw, bv, coef


def _s5_state_kernel(u_ref, bw_ref, coef_ref, h0_ref, hin_ref, hl_ref, s_scr, *, lc, n_q, n_seg):
    i = pl.program_id(1)
    tr = u_ref.shape[0] // lc
    n_tiles = s_scr.shape[0]
    nh = n_tiles // 2
    half = nh * LANES
    n_steps = s_scr.shape[1] // n_q
    acc = None
    for s in range(lc):
        a = u_ref[pl.ds(s, tr, stride=lc), :].astype(BF16)
        d = jnp.dot(a, bw_ref[s], preferred_element_type=F32)
        acc = d if acc is None else acc + d
    for m in range(n_tiles):
        s_scr[m, pl.ds(pl.multiple_of(i * tr, tr), tr), :] = acc[:, m * LANES:(m + 1) * LANES]

    @pl.when(i == pl.num_programs(1) - 1)
    def _():
        bc = lambda r: jnp.broadcast_to(coef_ref[r:r + 1, :], (n_q, half))
        ar, ai = bc(0), bc(1)
        jr, ji = coef_ref[2:3, :], coef_ref[3:4, :]

        def load_rows(j):
            return jnp.concatenate([s_scr[m, pl.ds(j, n_q, stride=n_steps), :] for m in range(n_tiles)], axis=1)

        def store_rows(j, x):
            for m in range(n_tiles):
                s_scr[m, pl.ds(j, n_q, stride=n_steps), :] = x[:, m * LANES:(m + 1) * LANES]

        def scan_zero(j, carry):
            xr, xi = carry
            s = load_rows(j)
            store_rows(j, jnp.concatenate([xr, xi], axis=1))
            return ar * xr - ai * xi + s[:, :half], ar * xi + ai * xr + s[:, half:]

        zero = jnp.zeros((n_q, half), F32)
        xr_end, xi_end = lax.fori_loop(0, n_steps, scan_zero, (zero, zero))

        h0 = h0_ref[...]
        if n_seg == 1:
            er, ei = h0[:, :half], h0[:, half:]
            hl_ref[...] = jnp.concatenate([jr * er - ji * ei + xr_end, jr * ei + ji * er + xi_end], axis=1)
        else:
            er_rows, ei_rows = [], []
            for b in range(n_q // n_seg):
                r_, i_ = h0[b:b + 1, :half], h0[b:b + 1, half:]
                for sg in range(n_seg):
                    q = b * n_seg + sg
                    er_rows.append(r_)
                    ei_rows.append(i_)
                    r_, i_ = (jr * r_ - ji * i_ + xr_end[q:q + 1], jr * i_ + ji * r_ + xi_end[q:q + 1])
                hl_ref[b:b + 1, :] = jnp.concatenate([r_, i_], axis=1)
            er = jnp.concatenate(er_rows, axis=0)
            ei = jnp.concatenate(ei_rows, axis=0)

        def scan_fix(j, carry):
            fr, fi = carry
            store_rows(j, load_rows(j) + jnp.concatenate([fr, fi], axis=1))
            return ar * fr - ai * fi, ar * fi + ai * fr

        lax.fori_loop(0, n_steps, scan_fix, (er, ei))
        for m in range(n_tiles):
            hin_ref[:, m * LANES:(m + 1) * LANES] = s_scr[m].astype(BF16)


def _s5_out_kernel(u_ref, hin_ref, bd_ref, bv_ref, y_ref, a_scr, acc_scr, *, lc):
    tc = hin_ref.shape[0]
    n_blk = u_ref.shape[0]
    for s in range(lc):
        for h in range(n_blk):
            a_scr[s * tc:(s + 1) * tc, h * LANES:(h + 1) * LANES] = (
                u_ref[h, pl.ds(s, tc, stride=lc), :].astype(BF16))
    acc_scr[...] = jnp.dot(a_scr[...], bd_ref[0], preferred_element_type=F32)
    for tau in range(1, lc):
        n = (lc - tau) * tc
        acc_scr[tau * tc:, :] += jnp.dot(a_scr[:n, :], bd_ref[tau], preferred_element_type=F32)
    hin = hin_ref[...]
    for t in range(lc):
        y_t = acc_scr[t * tc:(t + 1) * tc, :] + jnp.dot(hin, bv_ref[t], preferred_element_type=F32)
        for h in range(n_blk):
            y_ref[h, pl.ds(t, tc, stride=lc), :] = y_t[:, h * LANES:(h + 1) * LANES]


def _s5(u3, h0_re, h0_im, ops, bsz, lc, n_seg):
    bd, bw, bv, coef = ops
    t = u3.shape[1]
    ds = u3.shape[0] * LANES
    n_groups, p = h0_re.shape[1], h0_re.shape[2]
    gi, go = S5_IN_GROUPS, S5_OUT_GROUPS
    wi, wo = ds // (n_groups // gi), ds // (n_groups // go)
    ws = gi * 2 * p
    nc = t // lc
    n_q = bsz * n_seg
    assert nc % n_q == 0 and n_q % SUBLANES == 0 and wi == LANES
    tr = _tile(nc, 256)
    pack = lambda h: h.reshape(bsz, n_groups // gi, gi * p)
    h0 = jnp.swapaxes(jnp.concatenate([pack(h0_re), pack(h0_im)], axis=-1), 0, 1)
    hin, hl = pl.pallas_call(
        functools.partial(_s5_state_kernel, lc=lc, n_q=n_q, n_seg=n_seg),
        out_shape=(jax.ShapeDtypeStruct((nc, (n_groups // gi) * ws), BF16),
                   jax.ShapeDtypeStruct((n_groups // gi, bsz, ws), F32)),
        grid=(n_groups // gi, nc // tr),
        in_specs=[pl.BlockSpec((None, tr * lc, wi), lambda k, i: (k, i, 0)),
                  pl.BlockSpec((None, lc, wi, ws), lambda k, i: (k, 0, 0, 0)),
                  pl.BlockSpec((None, SUBLANES, ws // 2), lambda k, i: (k, 0, 0)),
                  pl.BlockSpec((None, bsz, ws), lambda k, i: (k, 0, 0))],
        out_specs=(pl.BlockSpec((nc, ws), lambda k, i: (0, k)),
                   pl.BlockSpec((None, bsz, ws), lambda k, i: (k, 0, 0))),
        scratch_shapes=[pltpu.VMEM((ws // LANES, nc, LANES), F32)],
        compiler_params=_cparams("arbitrary", "arbitrary"),
        name="s5_state",
    )(u3, bw, coef, h0)
    tc = _tile(nc, 256)
    wso = (go // gi) * ws
    y = pl.pallas_call(
        functools.partial(_s5_out_kernel, lc=lc),
        out_shape=jax.ShapeDtypeStruct(u3.shape, F32),
        grid=(n_groups // go, nc // tc),
        in_specs=[pl.BlockSpec((wo // LANES, tc * lc, LANES), lambda s, i: (s, i, 0)),
                  pl.BlockSpec((tc, wso), lambda s, i: (i, s)),
                  pl.BlockSpec((None, lc, wo, wo), lambda s, i: (s, 0, 0, 0)),
                  pl.BlockSpec((None, lc, wso, wo), lambda s, i: (s, 0, 0, 0), pipeline_mode=pl.Buffered(1))],
        out_specs=pl.BlockSpec((wo // LANES, tc * lc, LANES), lambda s, i: (s, i, 0)),
        scratch_shapes=[pltpu.VMEM((lc * tc, wo), BF16), pltpu.VMEM((lc * tc, wo), F32)],
        compiler_params=_cparams("arbitrary", "arbitrary"),
        name="s5_out",
    )(u3, hin, bd, bv)
    hl = jnp.swapaxes(hl, 0, 1)
    unpack = lambda h: h.reshape(bsz, n_groups, p)
    return y, unpack(hl[:, :, :ws // 2]), unpack(hl[:, :, ws // 2:])


def _glu_kernel(y_ref, u_ref, d_ref, w_ref, b_ref, g_ref, o_ref):
    lanes = lambda r: jnp.concatenate([r[k] for k in range(r.shape[0])], axis=1)
    z = jax.nn.gelu(lanes(y_ref) + d_ref[...] * lanes(u_ref))
    gate = jax.nn.sigmoid(jnp.dot(z.astype(BF16), w_ref[...], preferred_element_type=F32) + b_ref[...])
    o_ref[...] = _rms(z * gate, g_ref[...]).astype(BF16)


def _glu(y, u, d_skip, w_glu, b_glu, g_ssm):
    nblk, t, _ = y.shape
    ds = nblk * LANES
    tm = _tile(t, 512)
    vec = pl.BlockSpec((1, ds), lambda i: (0, 0))
    blk = pl.BlockSpec((nblk, tm, LANES), lambda i: (0, i, 0))
    return pl.pallas_call(
        _glu_kernel,
        out_shape=jax.ShapeDtypeStruct((t, ds), BF16),
        grid=(t // tm,),
        in_specs=[blk, blk, vec, pl.BlockSpec((ds, ds), lambda i: (0, 0)), vec, vec],
        out_specs=pl.BlockSpec((tm, ds), lambda i: (i, 0)),
        compiler_params=_cparams("arbitrary"),
        name="s5_glu",
    )(y, u, d_skip.reshape(1, ds), w_glu, b_glu.reshape(1, ds), g_ssm.reshape(1, ds))


def _outproj_kernel(att_ref, ssm_ref, x_ref, wa_ref, ws_ref, ga_ref, gt_ref, gf_ref, sh_ref, sc_ref, wr_ref,
                    *rest):
    x1_ref, h2_ref, lg_ref = rest[-3:]
    a = _rms(att_ref[...], ga_ref[...]).astype(BF16)
    mixed = (jnp.dot(a, wa_ref[...], preferred_element_type=F32)
             + jnp.dot(ssm_ref[...], ws_ref[...], preferred_element_type=F32))
    x1 = x_ref[...] + gt_ref[...] * mixed
    x1_ref[...] = x1
    h2 = _rms(x1, gf_ref[...]) * (1.0 + sc_ref[...]) + sh_ref[...]
    hi = h2.astype(BF16)
    h2_ref[...] = hi
    lo = (h2 - hi.astype(F32)).astype(BF16)
    r = (jnp.dot(hi, wr_ref[...], preferred_element_type=F32)
         + jnp.dot(lo, wr_ref[...], preferred_element_type=F32))
    lg_ref[...] = r[:, :LANES] + r[:, LANES:]


OUTPROJ_TM = 256


def _outproj(att, ssm_n, x, wa, ws, g_att, gt1, g_ffn, sh2, sc2, wr, seq_len, t_all, row_off, shared):
    t, d = x.shape
    da, ds = att.shape[1], ssm_n.shape[1]
    tm = OUTPROJ_TM
    assert t % tm == 0 and row_off % tm == 0 and t_all % tm == 0
    off = row_off // tm
    gt_op, gt_spec = _mod_operand(gt1, seq_len, tm)
    sh_op, sh_spec = _mod_operand(sh2, seq_len, tm)
    sc_op, sc_spec = _mod_operand(sc2, seq_len, tm)
    row = lambda n: pl.BlockSpec((tm, n), lambda i: (i, 0))
    row_shared = lambda n: pl.BlockSpec((tm, n), lambda i: (i + off, 0))
    const = lambda a, b: pl.BlockSpec((a, b), lambda i: (0, 0))
    operands = [att, ssm_n, x, wa, ws, g_att.reshape(1, da), gt_op, g_ffn.reshape(1, d), sh_op, sc_op, wr]
    in_specs = [row(da), row(ds), row(d), const(da, d), const(ds, d), const(1, da),
                gt_spec, const(1, d), sh_spec, sc_spec, const(d, 2 * LANES)]
    aliases = {}
    if shared is not None:
        aliases = {len(operands): 1, len(operands) + 1: 2}
        operands += list(shared)
        in_specs += [pl.BlockSpec(memory_space=pl.ANY)] * 2
    x1, h2, lg = pl.pallas_call(
        _outproj_kernel,
        out_shape=(jax.ShapeDtypeStruct((t, d), F32), jax.ShapeDtypeStruct((t_all, d), BF16),
                   jax.ShapeDtypeStruct((t_all, LANES), F32)),
        grid=(t // tm,),
        in_specs=in_specs,
        out_specs=(row(d), row_shared(d), row_shared(LANES)),
        input_output_aliases=aliases,
        compiler_params=_cparams("arbitrary"),
        name="out_proj",
    )(*operands)
    return x1, (h2, lg)


def _expert_kernel(te_ref, nu_ref, xs_ref, rw_ref, w1_ref, w3_ref, w2_ref, o_ref, w1_scr, w3_scr, w2_scr):
    i = pl.program_id(0)
    live = i < nu_ref[0]

    @pl.when(live & ((i == 0) | (te_ref[i] != te_ref[jnp.maximum(i - 1, 0)])))
    def _():
        w1_scr[...] = w1_ref[...].astype(BF16)
        w3_scr[...] = w3_ref[...].astype(BF16)
        w2_scr[...] = w2_ref[...].astype(BF16)

    @pl.when(live)
    def _():
        x = xs_ref[...]
        h1 = jnp.dot(x, w1_scr[...], preferred_element_type=F32)
        h3 = jnp.dot(x, w3_scr[...], preferred_element_type=F32)
        hid = (h1 * jax.nn.sigmoid(h1) * h3).astype(BF16)
        o_ref[...] = jnp.dot(hid, w2_scr[...], preferred_element_type=F32) * rw_ref[...]

    @pl.when(i >= nu_ref[0])
    def _():
        o_ref[...] = jnp.zeros(o_ref.shape, F32)


def _route(logits, n_groups, n_experts, top_k, tm):
    t = logits.shape[0]
    epg = n_experts // n_groups
    tok = jnp.arange(t)
    g_logits = logits[:, :n_groups]
    p_group = jax.nn.softmax(g_logits, axis=-1)
    g_sel = jnp.argmax(g_logits, axis=-1)
    e_sel = logits[:, n_groups:n_groups + n_experts].reshape(t, n_groups, epg)[tok, g_sel]
    top_v, top_i = lax.top_k(e_sel, top_k)
    gate = p_group[tok, g_sel][:, None] * jax.nn.softmax(top_v, axis=-1)
    eid = (g_sel[:, None] * epg + top_i).reshape(-1).astype(jnp.int32)
    n_asg = t * top_k
    onehot = (eid[:, None] == jnp.arange(n_experts, dtype=jnp.int32)[None, :]).astype(jnp.int32)
    csum = jnp.cumsum(onehot, axis=0)
    rank = jnp.take_along_axis(csum, eid[:, None], axis=1)[:, 0] - 1
    counts = csum[-1]
    padded = (counts + tm - 1) // tm * tm
    pends = jnp.cumsum(padded)
    dest = (pends - padded)[eid] + rank
    n_pad = (n_asg + n_experts * (tm - 1) + tm - 1) // tm * tm
    row_asg = jnp.full((n_pad,), -1, jnp.int32).at[dest].set(jnp.arange(n_asg, dtype=jnp.int32))
    live = row_asg >= 0
    row_tok = jnp.where(live, row_asg // top_k, 0)
    row_w = jnp.where(live, gate.reshape(-1)[jnp.maximum(row_asg, 0)], 0.0)
    n_tiles = n_pad // tm
    tile_e = jnp.sum(pends[None, :] <= (jnp.arange(n_tiles, dtype=jnp.int32) * tm)[:, None], axis=1)
    tile_e = jnp.minimum(tile_e, n_experts - 1).astype(jnp.int32)
    n_used = (pends[-1] // tm).astype(jnp.int32).reshape(1)
    return row_tok, row_w, tile_e, n_used, dest.reshape(t, top_k)


def _experts(h2, logits, w1, w3, w2, n_groups, top_k):
    t, d = h2.shape
    n_experts, _, de = w1.shape
    tm = 256
    row_tok, row_w, tile_e, n_used, pos = _route(logits, n_groups, n_experts, top_k, tm)
    n_pad = row_tok.shape[0]
    xs = h2[row_tok]
    single = pl.Buffered(1)
    ys = pl.pallas_call(
        _expert_kernel,
        out_shape=jax.ShapeDtypeStruct((n_pad, d), F32),
        grid_spec=pltpu.PrefetchScalarGridSpec(
            num_scalar_prefetch=2,
            grid=(n_pad // tm,),
            in_specs=[pl.BlockSpec((tm, d), lambda i, te, nu: (i, 0)),
                      pl.BlockSpec((tm, 1), lambda i, te, nu: (i, 0)),
                      pl.BlockSpec((None, d, de), lambda i, te, nu: (te[i], 0, 0), pipeline_mode=single),
                      pl.BlockSpec((None, d, de), lambda i, te, nu: (te[i], 0, 0), pipeline_mode=single),
                      pl.BlockSpec((None, de, d), lambda i, te, nu: (te[i], 0, 0), pipeline_mode=single)],
            out_specs=pl.BlockSpec((tm, d), lambda i, te, nu: (i, 0)),
            scratch_shapes=[pltpu.VMEM((d, de), BF16), pltpu.VMEM((d, de), BF16), pltpu.VMEM((de, d), BF16)]),
        compiler_params=_cparams("arbitrary"),
        name="moe_experts",
    )(tile_e, n_used, xs, row_w.reshape(n_pad, 1), w1, w3, w2)
    return ys, pos


def _final_kernel(x_ref, gt_ref, g_ref, *rest):
    *y_refs, o_ref = rest
    moe = y_refs[0][...]
    for r in y_refs[1:]:
        moe = moe + r[...]
    o_ref[...] = _rms(x_ref[...] + gt_ref[...] * moe, g_ref[...])


def _final(x1, gt2, g_final, ys, seq_len):
    t, d = x1.shape
    tm = _tile(t, 256)
    gt_op, gt_spec = _mod_operand(gt2, seq_len, tm)
    row = pl.BlockSpec((tm, d), lambda i: (i, 0))
    return pl.pallas_call(
        _final_kernel,
        out_shape=jax.ShapeDtypeStruct((t, d), F32),
        grid=(t // tm,),
        in_specs=[row, gt_spec, pl.BlockSpec((1, d), lambda i: (0, 0))] + [row] * len(ys),
        out_specs=row,
        compiler_params=_cparams("arbitrary"),
        name="moe_combine_norm",
    )(x1, gt_op, g_final.reshape(1, d), *ys)


S5_CHUNK = 16
S5_SEGMENTS = 4
TOP_K = 2


def _mixers(x, mod, cache, wts, s5_ops, t_all, row_off, shared):
    (g_mix, w4, wf, bfp, d_skip, w_glu, b_glu, g_att, g_ssm, wa, ws, g_ffn, wr,
     n_heads, hd, n_groups_ssm) = wts
    bsz, seq, d = x.shape
    da = n_heads * hd
    t = bsz * seq
    xt = x.reshape(t, d)
    sh1, sc1, gt1, sh2, sc2, gt2 = jnp.split(mod, 6, axis=-1)
    qb, kf, vf, kb, vb, u, lfp = _inproj(xt, g_mix, sh1, sc1, w4, wf, bfp, seq, hd ** -0.5 * LOG2E)
    logf = lfp[:, :n_heads].reshape(bsz, seq, n_heads)
    if cache is None:
        fcum = _cumsum_rows(jnp.swapaxes(logf, 1, 2).reshape(bsz * n_heads, seq))[:, :seq]
        att = _fox_prompt(qb, kb, vb, (fcum * LOG2E).reshape(bsz, n_heads, seq), bsz, seq, n_heads, hd)
        h0 = jnp.zeros((bsz, n_groups_ssm, s5_ops[3].shape[2] // S5_IN_GROUPS), F32)
        ssm_y, h_re, h_im = _s5(u, h0, h0, s5_ops, bsz, S5_CHUNK, S5_SEGMENTS)
    else:
        cache_k, cache_v, cache_logf, st_re, st_im = cache
        past = cache_k.shape[1]
        lf_all = jnp.concatenate([cache_logf.astype(F32), logf], axis=1)
        f_all = _cumsum_rows(jnp.swapaxes(lf_all, 1, 2).reshape(bsz * n_heads, past + seq))
        f_all = (f_all[:, :past + seq] * LOG2E).reshape(bsz, n_heads, past + seq)
        att = _fox_sample(qb.reshape(bsz, seq, n_heads, hd), kf.reshape(bsz, seq, n_heads, hd),
                          vf.reshape(bsz, seq, n_heads, hd), f_all, cache_k, cache_v)
        att = att.reshape(t, da)
        ssm_y, h_re, h_im = _s5(u, st_re.astype(F32), st_im.astype(F32), s5_ops, bsz, seq, 1)
    ssm_n = _glu(ssm_y, u, d_skip, w_glu, b_glu, g_ssm)
    x1, shared = _outproj(att, ssm_n, xt, wa, ws, g_att, gt1, g_ffn, sh2, sc2, wr, seq, t_all, row_off, shared)
    new_cache = (kf.reshape(bsz, seq, n_heads, hd), vf.reshape(bsz, seq, n_heads, hd), logf, h_re, h_im)
    return x1, gt2, shared, new_cache


def kernel(x_prompt, x_sample, cache_k, cache_v, cache_logf, state_ssm_re, state_ssm_im, c_prompt, c_sample, w_ada, b_ada, g_mix, w_in, b_f, lam_re, lam_im, log_dt, b_re, b_im, c_re, c_im, d_skip, w_glu, b_glu, g_att, g_ssm, w_out, g_ffn, w_rg, w_re, w1, w3, w2, g_final):
    depth = w_ada.shape[0]
    assert depth == 1, "the residual stream of a deeper stack would have to be threaded through the layers"
    n_heads, hd = cache_k.shape[3], cache_k.shape[4]
    da = n_heads * hd
    d = x_prompt.shape[-1]
    ds = d - da
    assert da == ds
    n_groups_ssm = state_ssm_re.shape[2]
    n_expert_groups = w_rg.shape[-1]
    n_experts = w_re.shape[-1]
    assert n_expert_groups + n_experts <= LANES and n_heads <= LANES
    l = 0
    bp = x_prompt.shape[0]
    mod = _ada(jnp.concatenate([c_prompt, c_sample], axis=0).astype(F32), w_ada[l], b_ada[l])
    wi = w_in[l]
    w4 = jnp.concatenate([wi[:, :3 * da], wi[:, 3 * da + n_heads:]], axis=1).astype(BF16)
    wf = jnp.pad(wi[:, 3 * da:3 * da + n_heads], ((0, 0), (0, LANES - n_heads))).astype(BF16)
    bfp = jnp.pad(b_f[l], (0, LANES - n_heads)).reshape(1, LANES).astype(F32)
    wr = jnp.pad(jnp.concatenate([w_rg[l], w_re[l]], axis=1),
                 ((0, 0), (0, LANES - n_expert_groups - n_experts))).astype(F32)
    wr_hi = wr.astype(BF16)
    wr_lo = (wr - wr_hi.astype(F32)).astype(BF16)
    wr2 = jnp.concatenate([wr_hi, wr_lo], axis=1)
    wo = _to_bf16(w_out[l])
    wts = (g_mix[l], w4, wf, bfp, d_skip[l], _to_bf16(w_glu[l]), b_glu[l], g_att[l], g_ssm[l],
           wo[:da], wo[da:], g_ffn[l], wr2, n_heads, hd, n_groups_ssm)
    s5_args = (lam_re[l].astype(F32), lam_im[l].astype(F32), log_dt[l], b_re[l].astype(F32), b_im[l].astype(F32),
               c_re[l].astype(F32), c_im[l].astype(F32))
    seq_p = x_prompt.shape[1]
    ops_p = _s5_operators(*s5_args, S5_CHUNK, seq_p // (S5_SEGMENTS * S5_CHUNK))
    ops_s = _s5_operators(*s5_args, x_sample.shape[1], 1)
    bs, seq_s = x_sample.shape[:2]
    t_p, t_s = bp * seq_p, bs * seq_s
    x1p, gt2p, shared, (kp, vp, lfp, rep, imp) = _mixers(
        x_prompt.astype(F32), mod[:bp], None, wts, ops_p, t_p + t_s, 0, None)
    cache = (cache_k[l], cache_v[l], cache_logf[l], state_ssm_re[l], state_ssm_im[l])
    x1s, gt2s, (h2, logits), (ksm, vsm, lfs, res, ims) = _mixers(
        x_sample.astype(F32), mod[bp:], cache, wts, ops_s, t_p + t_s, t_p, shared)
    ys, pos = _experts(h2, logits, w1[l], w3[l], w2[l], n_expert_groups, TOP_K)
    yp = _final(x1p, gt2p, g_final, [ys[pos[:t_p, k]] for k in range(TOP_K)], seq_p).reshape(x_prompt.shape)
    ysm = _final(x1s, gt2s, g_final, [ys[pos[t_p:, k]] for k in range(TOP_K)], seq_s).reshape(x_sample.shape)
    return (yp, ysm, kp[None], vp[None], lfp[None], rep[None], imp[None],
            ksm[None], vsm[None], lfs[None], res[None], ims[None])
```

```python
import functools
import math

import jax
import jax.numpy as jnp
from jax import lax
from jax.experimental import pallas as pl
from jax.experimental.pallas import tpu as pltpu

F32 = jnp.float32
BF16 = jnp.bfloat16
EPS = 1e-6
NEG = -1e30
LOG2E = math.log2(math.e)
LANES = 128
SUBLANES = 8
VMEM_LIMIT = 56 * 1024 * 1024
HIGHEST = lax.Precision.HIGHEST
NT_DIMS = (((1,), (1,)), ((), ()))


def _cparams(*sem):
    return pltpu.CompilerParams(dimension_semantics=sem, vmem_limit_bytes=VMEM_LIMIT)


def _tile(n, pref):
    t = min(n, pref)
    assert n % t == 0, (n, pref)
    return t


def _rms(x, g):
    return x * lax.rsqrt(jnp.mean(x * x, axis=-1, keepdims=True) + EPS) * g


def _mod_operand(vec, seq_len, tm, off=0):
    n_seq, d = vec.shape
    if seq_len % tm == 0:
        per = seq_len // tm
        return vec[:, None, :], pl.BlockSpec((None, 1, d), lambda i, *_: ((i + off) // per, 0, 0))
    assert tm % seq_len == 0
    rows = jnp.repeat(vec, seq_len, axis=0).reshape(-1, tm, d)
    return rows, pl.BlockSpec((None, tm, d), lambda i, *_: (i + off, 0, 0))


def _cast_kernel(x_ref, o_ref):
    o_ref[...] = x_ref[...].astype(BF16)


def _to_bf16(w):
    n = w.shape[-1]
    w2 = w.reshape(-1, n)
    rows = w2.shape[0]
    tr = _tile(rows, max(SUBLANES, (1 << 20) // n))
    out = pl.pallas_call(
        _cast_kernel,
        out_shape=jax.ShapeDtypeStruct((rows, n), BF16),
        grid=(rows // tr,),
        in_specs=[pl.BlockSpec((tr, n), lambda i: (i, 0))],
        out_specs=pl.BlockSpec((tr, n), lambda i: (i, 0)),
        compiler_params=_cparams("arbitrary"),
        name="cast_bf16",
    )(w2)
    return out.reshape(w.shape)


def _ada_kernel(c_ref, w_ref, b_ref, o_ref):
    c = c_ref[...]
    a = (c * jax.nn.sigmoid(c)).astype(BF16)
    o_ref[...] = jnp.dot(a, w_ref[...].astype(BF16), preferred_element_type=F32) + b_ref[...]


def _ada(c, w, b):
    s, d = c.shape
    n = w.shape[1]
    tn = _tile(n, 1024)
    return pl.pallas_call(
        _ada_kernel,
        out_shape=jax.ShapeDtypeStruct((s, n), F32),
        grid=(n // tn,),
        in_specs=[pl.BlockSpec((s, d), lambda j: (0, 0)),
                  pl.BlockSpec((d, tn), lambda j: (0, j)),
                  pl.BlockSpec((1, tn), lambda j: (0, j))],
        out_specs=pl.BlockSpec((s, tn), lambda j: (0, j)),
        compiler_params=_cparams("arbitrary"),
        name="ada_mod",
    )(c, w, b.reshape(1, n))


def _inproj_kernel(x_ref, g_ref, sh_ref, sc_ref, w_ref, wf_ref, bf_ref,
                   q_ref, kf_ref, vf_ref, kb_ref, vb_ref, u_ref, lf_ref, h_scr, *, qscale):
    j = pl.program_id(1)

    @pl.when(j == 0)
    def _():
        h = _rms(x_ref[...], g_ref[...]) * (1.0 + sc_ref[...]) + sh_ref[...]
        hb = h.astype(BF16)
        h_scr[...] = hb
        fg = jnp.dot(hb, wf_ref[...], preferred_element_type=F32) + bf_ref[...]
        lf_ref[...] = jnp.minimum(fg, 0.0) - jnp.log1p(jnp.exp(-jnp.abs(fg)))

    p = jnp.dot(h_scr[...], w_ref[...], preferred_element_type=F32)

    @pl.when(j == 0)
    def _():
        q_ref[...] = (p * qscale).astype(BF16)

    @pl.when(j == 1)
    def _():
        kf_ref[...] = p
        kb_ref[...] = p.astype(BF16)

    @pl.when(j == 2)
    def _():
        vf_ref[...] = p
        vb_ref[...] = p.astype(BF16)

    @pl.when(j == 3)
    def _():
        for k in range(u_ref.shape[0]):
            u_ref[k] = p[:, k * LANES:(k + 1) * LANES]


def _inproj(x, g, sh, sc, w4, wf, bfp, seq_len, qscale):
    t, d = x.shape
    da = w4.shape[1] // 4
    tm = _tile(t, 512)
    sh_op, sh_spec = _mod_operand(sh, seq_len, tm)
    sc_op, sc_spec = _mod_operand(sc, seq_len, tm)
    row = lambda i, j: (i, 0)
    outs = pl.pallas_call(
        functools.partial(_inproj_kernel, qscale=qscale),
        out_shape=(jax.ShapeDtypeStruct((t, da), BF16),
                   jax.ShapeDtypeStruct((t, da), F32), jax.ShapeDtypeStruct((t, da), F32),
                   jax.ShapeDtypeStruct((t, da), BF16), jax.ShapeDtypeStruct((t, da), BF16),
                   jax.ShapeDtypeStruct((da // LANES, t, LANES), F32),
                   jax.ShapeDtypeStruct((t, LANES), F32)),
        grid=(t // tm, 4),
        in_specs=[pl.BlockSpec((tm, d), row),
                  pl.BlockSpec((1, d), lambda i, j: (0, 0)),
                  sh_spec, sc_spec,
                  pl.BlockSpec((d, da), lambda i, j: (0, j)),
                  pl.BlockSpec((d, LANES), lambda i, j: (0, 0)),
                  pl.BlockSpec((1, LANES), lambda i, j: (0, 0))],
        out_specs=(pl.BlockSpec((tm, da), row),) * 5
                  + (pl.BlockSpec((da // LANES, tm, LANES), lambda i, j: (0, i, 0)), pl.BlockSpec((tm, LANES), row)),
        scratch_shapes=[pltpu.VMEM((tm, d), BF16)],
        compiler_params=_cparams("arbitrary", "arbitrary"),
        name="in_proj",
    )(x, g.reshape(1, d), sh_op, sc_op, w4, wf, bfp)
    return outs


def _cumsum_kernel(x_ref, o_ref):
    sb, nb, _ = x_ref.shape
    li = lax.broadcasted_iota(jnp.int32, (LANES, LANES), 0)
    lj = lax.broadcasted_iota(jnp.int32, (LANES, LANES), 1)
    upper = (li <= lj).astype(F32)
    ri = lax.broadcasted_iota(jnp.int32, (nb, nb), 0)
    rj = lax.broadcasted_iota(jnp.int32, (nb, nb), 1)
    strict = (rj < ri).astype(F32)
    for s in range(sb):
        within = jnp.dot(x_ref[s], upper, precision=HIGHEST, preferred_element_type=F32)
        tot = jnp.broadcast_to(within[:, LANES - 1:LANES], (nb, LANES))
        off = jnp.dot(strict, tot, precision=HIGHEST, preferred_element_type=F32)
        o_ref[s] = within + off


def _cumsum_rows(x):
    n_rows, n = x.shape
    nb = -(-n // (LANES * SUBLANES)) * SUBLANES
    xp = jnp.pad(x, ((0, 0), (0, nb * LANES - n))).reshape(n_rows, nb, LANES)
    sb = _tile(n_rows, 16)
    out = pl.pallas_call(
        _cumsum_kernel,
        out_shape=jax.ShapeDtypeStruct((n_rows, nb, LANES), F32),
        grid=(n_rows // sb,),
        in_specs=[pl.BlockSpec((sb, nb, LANES), lambda i: (i, 0, 0))],
        out_specs=pl.BlockSpec((sb, nb, LANES), lambda i: (i, 0, 0)),
        compiler_params=_cparams("arbitrary"),
        name="logf_cumsum",
    )(xp)
    return out.reshape(n_rows, nb * LANES)


FOX_TQ = 2048
FOX_TK = 1024
FOX_SUB = 512


def _col_from_row(row):
    n = row.shape[1]
    eye = lax.broadcasted_iota(jnp.int32, (n, n), 0) == lax.broadcasted_iota(jnp.int32, (n, n), 1)
    return jnp.sum(jnp.where(eye, jnp.broadcast_to(row, (n, n)), 0.0), axis=1, keepdims=True)


def _lane_tiles(x):
    return [x[:, j * LANES:(j + 1) * LANES] for j in range(x.shape[1] // LANES)]


def _fox_kernel(q_ref, k_ref, v_ref, f_ref, o_ref, m_scr, l_scr, acc_scr, *, tq, tk, sub):
    qi = pl.program_id(2)
    nsub = tq // sub
    fqb = [jnp.broadcast_to(_col_from_row(f_ref[qi * nsub + a]), (sub, LANES)) for a in range(nsub)]
    m_scr[...] = jnp.full(m_scr.shape, NEG, F32)
    l_scr[...] = jnp.zeros(l_scr.shape, F32)
    acc_scr[...] = jnp.zeros(acc_scr.shape, F32)

    def chain(a, k, v, fk, diagonal):
        rows = pl.ds(a * sub, sub)
        t1 = lax.dot_general(q_ref[rows, :], k, NT_DIMS, preferred_element_type=F32) - fk
        if diagonal:
            row = lax.broadcasted_iota(jnp.int32, t1.shape, 0)
            col = lax.broadcasted_iota(jnp.int32, t1.shape, 1)
            t1 = jnp.where(col <= row, t1, NEG)
        tiles = _lane_tiles(t1)
        part = functools.reduce(jnp.maximum, tiles)
        m_prev = m_scr[rows, :]
        m_new = jnp.maximum(m_prev, jnp.max(part, axis=-1, keepdims=True) + fqb[a])
        c = m_new - fqb[a]
        p = [jnp.exp2(t - c) for t in tiles]
        alpha = jnp.exp2(m_prev - m_new)
        l_scr[rows, :] = alpha * l_scr[rows, :] + functools.reduce(jnp.add, p)
        pv = jnp.dot(jnp.concatenate(p, axis=1).astype(BF16), v, preferred_element_type=F32)
        acc_scr[rows, :] = alpha * acc_scr[rows, :] + pv
        m_scr[rows, :] = m_new

    def full_step(kt, carry):
        ks = pl.multiple_of(kt * tk, tk)
        k = k_ref[pl.ds(ks, tk), :]
        v = v_ref[pl.ds(ks, tk), :]
        fk = jnp.concatenate([f_ref[kt * (tk // sub) + j] for j in range(tk // sub)], axis=1)
        for a in range(nsub):
            chain(a, k, v, fk, False)
        return carry

    lax.fori_loop(0, qi * (tq // tk), full_step, 0)
    for a in range(nsub):
        for j in range(a + 1):
            ks = pl.multiple_of((qi * nsub + j) * sub, sub)
            chain(a, k_ref[pl.ds(ks, sub), :], v_ref[pl.ds(ks, sub), :], f_ref[qi * nsub + j], j == a)
    o_ref[...] = acc_scr[...] / jnp.sum(l_scr[...], axis=-1, keepdims=True)


def _fox_prompt(qb, kb, vb, fcum2, bsz, seq, n_heads, hd):
    assert hd == LANES
    tq = _tile(seq, FOX_TQ)
    tk = _tile(tq, FOX_TK)
    sub = _tile(tk, FOX_SUB)
    nq = seq // tq
    f = fcum2.reshape(bsz, n_heads, seq // sub, 1, sub)
    return pl.pallas_call(
        functools.partial(_fox_kernel, tq=tq, tk=tk, sub=sub),
        out_shape=jax.ShapeDtypeStruct((bsz * seq, n_heads * hd), F32),
        grid=(bsz, n_heads, nq),
        in_specs=[pl.BlockSpec((tq, hd), lambda b, h, i: (b * nq + i, h)),
                  pl.BlockSpec((seq, hd), lambda b, h, i: (b, h)),
                  pl.BlockSpec((seq, hd), lambda b, h, i: (b, h)),
                  pl.BlockSpec((None, None, seq // sub, 1, sub), lambda b, h, i: (b, h, 0, 0, 0))],
        out_specs=pl.BlockSpec((tq, hd), lambda b, h, i: (b * nq + i, h)),
        scratch_shapes=[pltpu.VMEM((tq, LANES), F32), pltpu.VMEM((tq, LANES), F32), pltpu.VMEM((tq, hd), F32)],
        compiler_params=_cparams("arbitrary", "arbitrary", "arbitrary"),
        name="fox_prompt",
    )(qb, kb, vb, f)


def _fox_sample_kernel(q_ref, ck_ref, cv_ref, kn_ref, vn_ref, fq_ref, fkc_ref, fkn_ref,
                       rh_ref, ri_ref, lh_ref, lhn_ref, kin_ref, o_ref, m_scr, l_scr, acc_scr):
    kt = pl.program_id(1)

    @pl.when(kt == 0)
    def _():
        m_scr[...] = jnp.full(m_scr.shape, NEG, F32)
        l_scr[...] = jnp.zeros(l_scr.shape, F32)
        acc_scr[...] = jnp.zeros(acc_scr.shape, F32)

    q = q_ref[...]
    fq = fq_ref[...]
    n_heads = ck_ref.shape[1]
    s_new = q.shape[0] // n_heads

    def update(k4, v4, t1_of):
        n, h, d = k4.shape
        k2 = k4.reshape(n * h, d).astype(BF16)
        v2 = v4.reshape(n * h, d).astype(BF16)
        t1 = t1_of(lax.dot_general(q, k2, NT_DIMS, preferred_element_type=F32))
        m_prev = m_scr[...]
        m_new = jnp.maximum(m_prev, jnp.max(t1, axis=-1, keepdims=True) + fq)
        alpha = jnp.exp2(m_prev - m_new)
        p = jnp.exp2(t1 - (m_new - fq))
        l_scr[...] = alpha * l_scr[...] + jnp.sum(p, axis=-1, keepdims=True)
        acc_scr[...] = alpha * acc_scr[...] + jnp.dot(p.astype(BF16), v2, preferred_element_type=F32)
        m_scr[...] = m_new

    def cache_t1(s):
        head = lax.broadcasted_iota(jnp.int32, (n_heads, 1), 0)
        fkm = jnp.where(lh_ref[...] == head, fkc_ref[...], -NEG)
        return jnp.concatenate([s[h * s_new:(h + 1) * s_new] - fkm[h:h + 1] for h in range(n_heads)], axis=0)

    update(ck_ref[...], cv_ref[...], cache_t1)

    @pl.when(kt == pl.num_programs(1) - 1)
    def _():
        causal_head = jnp.where(kin_ref[...] <= ri_ref[...], lhn_ref[...], -1)
        update(kn_ref[...], vn_ref[...],
               lambda s: jnp.where(rh_ref[...] == causal_head, s - fkn_ref[...], NEG))
        o_ref[...] = acc_scr[...] / l_scr[...]


def _fox_sample(qb, k_new, v_new, f_all2, cache_k, cache_v):
    bsz, s_new, n_heads, hd = qb.shape
    past = cache_k.shape[1]
    tk = _tile(past, 1024)
    n_rows = n_heads * s_new
    assert n_rows % 16 == 0 and s_new % SUBLANES == 0
    q2 = jnp.swapaxes(qb, 1, 2).reshape(bsz, n_rows, hd)
    fq = f_all2[:, :, past:].reshape(bsz, n_rows, 1)
    fkc = jnp.swapaxes(f_all2[:, :, :past], 1, 2).reshape(bsz, 1, past * n_heads)
    fkn = jnp.swapaxes(f_all2[:, :, past:], 1, 2).reshape(bsz, 1, s_new * n_heads)
    r = jnp.arange(n_rows, dtype=jnp.int32).reshape(n_rows, 1)
    lane = lambda n: jnp.arange(n * n_heads, dtype=jnp.int32).reshape(1, n * n_heads)
    const = lambda a: pl.BlockSpec(a.shape, lambda b, j: (0,) * a.ndim)
    consts = (r // s_new, r % s_new, lane(tk) % n_heads, lane(s_new) % n_heads, lane(s_new) // n_heads)
    out = pl.pallas_call(
        _fox_sample_kernel,
        out_shape=jax.ShapeDtypeStruct((bsz, n_rows, hd), F32),
        grid=(bsz, past // tk),
        in_specs=[pl.BlockSpec((None, n_rows, hd), lambda b, j: (b, 0, 0)),
                  pl.BlockSpec((None, tk, n_heads, hd), lambda b, j: (b, j, 0, 0)),
                  pl.BlockSpec((None, tk, n_heads, hd), lambda b, j: (b, j, 0, 0)),
                  pl.BlockSpec((None, s_new, n_heads, hd), lambda b, j: (b, 0, 0, 0)),
                  pl.BlockSpec((None, s_new, n_heads, hd), lambda b, j: (b, 0, 0, 0)),
                  pl.BlockSpec((None, n_rows, 1), lambda b, j: (b, 0, 0)),
                  pl.BlockSpec((None, 1, tk * n_heads), lambda b, j: (b, 0, j)),
                  pl.BlockSpec((None, 1, s_new * n_heads), lambda b, j: (b, 0, 0))]
                 + [const(a) for a in consts],
        out_specs=pl.BlockSpec((None, n_rows, hd), lambda b, j: (b, 0, 0)),
        scratch_shapes=[pltpu.VMEM((n_rows, 1), F32), pltpu.VMEM((n_rows, 1), F32), pltpu.VMEM((n_rows, hd), F32)],
        compiler_params=_cparams("arbitrary", "arbitrary"),
        name="fox_sample",
    )(q2, cache_k, cache_v, k_new, v_new, fq, fkc, fkn, *consts)
    return jnp.swapaxes(out.reshape(bsz, n_heads, s_new, hd), 1, 2)


S5_IN_GROUPS = 8
S5_OUT_GROUPS = 16


def _s5_operators(lam_re, lam_im, log_dt, b_re, b_im, c_re, c_im, lc, n_steps):
    g, p = lam_re.shape
    hc = b_re.shape[2]
    gi, go = S5_IN_GROUPS, S5_OUT_GROUPS
    dt = jnp.exp(log_dt.astype(F32))[:, None]

    def power(k):
        mag = jnp.exp(lam_re * dt * k)
        return mag * jnp.cos(lam_im * dt * k), mag * jnp.sin(lam_im * dt * k)

    lbr, lbi = power(1.0)
    den = lam_re * lam_re + lam_im * lam_im
    fr = ((lbr - 1.0) * lam_re + lbi * lam_im) / den
    fi = (lbi * lam_re - (lbr - 1.0) * lam_im) / den
    bbr = fr[:, :, None] * b_re - fi[:, :, None] * b_im
    bbi = fr[:, :, None] * b_im + fi[:, :, None] * b_re
    ks = jnp.arange(lc + 1, dtype=F32)[None, :, None]
    mag = jnp.exp(lam_re[:, None, :] * dt[:, None, :] * ks)
    ang = lam_im[:, None, :] * dt[:, None, :] * ks
    pwr, pwi = mag * jnp.cos(ang), mag * jnp.sin(ang)
    cr, ci = jnp.swapaxes(c_re, 1, 2), jnp.swapaxes(c_im, 1, 2)
    d_r = bbr[:, :, :, None] * cr[:, :, None, :] - bbi[:, :, :, None] * ci[:, :, None, :]
    d_i = bbr[:, :, :, None] * ci[:, :, None, :] + bbi[:, :, :, None] * cr[:, :, None, :]
    kern = (jnp.einsum("gtp,gpab->gtab", pwr[:, :lc], d_r, precision=HIGHEST)
            - jnp.einsum("gtp,gpab->gtab", pwi[:, :lc], d_i, precision=HIGHEST))
    same = lambda n: jnp.arange(n)[:, None, None, None] == jnp.arange(n)[None, None, :, None]
    bd = jnp.transpose(kern.reshape(g // go, go, lc, hc, hc), (0, 2, 1, 3, 4))[:, :, :, :, None, :]
    bd = jnp.where(same(go), bd, 0.0).astype(BF16).reshape(g // go, lc, go * hc, go * hc)
    rev_r, rev_i = pwr[:, lc - 1::-1][:, :lc], pwi[:, lc - 1::-1][:, :lc]
    bt_r, bt_i = jnp.swapaxes(bbr, 1, 2)[:, None], jnp.swapaxes(bbi, 1, 2)[:, None]
    w_r = rev_r[:, :, None, :] * bt_r - rev_i[:, :, None, :] * bt_i
    w_i = rev_r[:, :, None, :] * bt_i + rev_i[:, :, None, :] * bt_r
    def place(w):
        w = jnp.transpose(w.reshape(g // gi, gi, lc, hc, p), (0, 2, 1, 3, 4))[:, :, :, :, None, :]
        return jnp.where(same(gi), w, 0.0).astype(BF16).reshape(g // gi, lc, gi * hc, gi * p)

    bw = jnp.concatenate([place(w_r), place(w_i)], axis=-1)
    nr, ni = jnp.swapaxes(pwr[:, 1:], 1, 2), jnp.swapaxes(pwi[:, 1:], 1, 2)
    v_r = cr[:, :, None, :] * nr[:, :, :, None] - ci[:, :, None, :] * ni[:, :, :, None]
    v_i = cr[:, :, None, :] * ni[:, :, :, None] + ci[:, :, None, :] * nr[:, :, :, None]
    halves = go // gi
    split = lambda v: jnp.transpose(v.reshape(g // go, halves, gi, p, lc, hc), (0, 4, 1, 2, 3, 5))
    bv = jnp.stack([split(v_r), split(-v_i)], axis=3)
    idx = lambda n, axis: jnp.arange(n).reshape([n if a == axis else 1 for a in range(7)])
    keep = (idx(halves, 0) == idx(halves, 4)) & (idx(gi, 2) == idx(gi, 5))
    bv = jnp.where(keep, bv[:, :, :, :, :, :, None, None, :], 0.0).astype(BF16)
    bv = bv.reshape(g // go, lc, halves * 2 * gi * p, go * hc)

    def coef(n_steps):
        jr, ji = power(float(lc * n_steps))
        c = jnp.stack([pwr[:, lc], pwi[:, lc], jr, ji], axis=1)
        c = jnp.swapaxes(c.reshape(g // gi, gi, 4, p), 1, 2).reshape(g // gi, 4, gi * p)
        return jnp.pad(c, ((0, 0), (0, SUBLANES - 4), (0, 0)))

    return [(bd, bw, bv, coef(n)) for n in n_steps]


def _s5_state_kernel(u_ref, bw_ref, coef_ref, h0_ref, hin_ref, hl_ref, s_scr, *, lc, n_q, n_seg):
    i = pl.program_id(1)
    tr = u_ref.shape[0] // lc
    n_tiles = s_scr.shape[0]
    nh = n_tiles // 2
    half = nh * LANES
    n_steps = s_scr.shape[1] // n_q
    acc = None
    for s in range(lc):
        a = u_ref[pl.ds(s, tr, stride=lc), :].astype(BF16)
        d = jnp.dot(a, bw_ref[s], preferred_element_type=F32)
        acc = d if acc is None else acc + d
    if n_steps == 1:
        rows = pl.ds(pl.multiple_of(i * tr, tr), tr)
    else:
        assert n_steps % tr == 0
        per = n_steps // tr
        rows = pl.ds(lax.rem(i, per) * (tr * n_q) + lax.div(i, per), tr, stride=n_q)
    for m in range(n_tiles):
        s_scr[m, rows, :] = acc[:, m * LANES:(m + 1) * LANES]

    @pl.when(i == pl.num_programs(1) - 1)
    def _():
        bc = lambda r: jnp.broadcast_to(coef_ref[r:r + 1, :], (n_q, half))
        ar, ai = bc(0), bc(1)
        jr, ji = coef_ref[2:3, :], coef_ref[3:4, :]
        step_rows = lambda j: pl.ds(pl.multiple_of(j * n_q, n_q), n_q)

        def load_rows(j):
            return jnp.concatenate([s_scr[m, step_rows(j), :] for m in range(n_tiles)], axis=1)

        def store_rows(j, x):
            for m in range(n_tiles):
                s_scr[m, step_rows(j), :] = x[:, m * LANES:(m + 1) * LANES]

        def scan_zero(j, carry):
            xr, xi = carry
            s = load_rows(j)
            store_rows(j, jnp.concatenate([xr, xi], axis=1))
            return ar * xr - ai * xi + s[:, :half], ar * xi + ai * xr + s[:, half:]

        zero = jnp.zeros((n_q, half), F32)
        xr_end, xi_end = lax.fori_loop(0, n_steps, scan_zero, (zero, zero))

        h0 = h0_ref[...]
        if n_seg == 1:
            er, ei = h0[:, :half], h0[:, half:]
            hl_ref[...] = jnp.concatenate([jr * er - ji * ei + xr_end, jr * ei + ji * er + xi_end], axis=1)
        else:
            er_rows, ei_rows = [], []
            for b in range(n_q // n_seg):
                r_, i_ = h0[b:b + 1, :half], h0[b:b + 1, half:]
                for sg in range(n_seg):
                    q = b * n_seg + sg
                    er_rows.append(r_)
                    ei_rows.append(i_)
                    r_, i_ = (jr * r_ - ji * i_ + xr_end[q:q + 1], jr * i_ + ji * r_ + xi_end[q:q + 1])
                hl_ref[b:b + 1, :] = jnp.concatenate([r_, i_], axis=1)
            er = jnp.concatenate(er_rows, axis=0)
            ei = jnp.concatenate(ei_rows, axis=0)

        def scan_fix(j, carry):
            fr, fi = carry
            store_rows(j, load_rows(j) + jnp.concatenate([fr, fi], axis=1))
            return ar * fr - ai * fi, ar * fi + ai * fr

        lax.fori_loop(0, n_steps, scan_fix, (er, ei))
        for m in range(n_tiles):
            if n_steps == 1:
                hin_ref[:, m * LANES:(m + 1) * LANES] = s_scr[m].astype(BF16)
            else:
                for q in range(n_q):
                    hin_ref[q * n_steps:(q + 1) * n_steps, m * LANES:(m + 1) * LANES] = (
                        s_scr[m, pl.ds(q, n_steps, stride=n_q), :].astype(BF16))


def _s5_out_kernel(u_ref, hin_ref, bd_ref, bv_ref, y_ref, a_scr, acc_scr, *, lc):
    tc = hin_ref.shape[0]
    n_blk = u_ref.shape[0]
    for s in range(lc):
        for h in range(n_blk):
            a_scr[s * tc:(s + 1) * tc, h * LANES:(h + 1) * LANES] = (
                u_ref[h, pl.ds(s, tc, stride=lc), :].astype(BF16))
    acc_scr[...] = jnp.dot(a_scr[...], bd_ref[0], preferred_element_type=F32)
    for tau in range(1, lc):
        n = (lc - tau) * tc
        acc_scr[tau * tc:, :] += jnp.dot(a_scr[:n, :], bd_ref[tau], preferred_element_type=F32)
    hin = hin_ref[...]
    for t in range(lc):
        y_t = acc_scr[t * tc:(t + 1) * tc, :] + jnp.dot(hin, bv_ref[t], preferred_element_type=F32)
        for h in range(n_blk):
            y_ref[h, pl.ds(t, tc, stride=lc), :] = y_t[:, h * LANES:(h + 1) * LANES]


def _s5(u3, h0_re, h0_im, ops, bsz, lc, n_seg):
    bd, bw, bv, coef = ops
    t = u3.shape[1]
    ds = u3.shape[0] * LANES
    n_groups, p = h0_re.shape[1], h0_re.shape[2]
    gi, go = S5_IN_GROUPS, S5_OUT_GROUPS
    wi, wo = ds // (n_groups // gi), ds // (n_groups // go)
    ws = gi * 2 * p
    nc = t // lc
    n_q = bsz * n_seg
    assert nc % n_q == 0 and n_q % SUBLANES == 0 and wi == LANES
    tr = _tile(nc if nc == n_q else nc // n_q, 256)
    pack = lambda h: h.reshape(bsz, n_groups // gi, gi * p)
    h0 = jnp.swapaxes(jnp.concatenate([pack(h0_re), pack(h0_im)], axis=-1), 0, 1)
    hin, hl = pl.pallas_call(
        functools.partial(_s5_state_kernel, lc=lc, n_q=n_q, n_seg=n_seg),
        out_shape=(jax.ShapeDtypeStruct((nc, (n_groups // gi) * ws), BF16),
                   jax.ShapeDtypeStruct((n_groups // gi, bsz, ws), F32)),
        grid=(n_groups // gi, nc // tr),
        in_specs=[pl.BlockSpec((None, tr * lc, wi), lambda k, i: (k, i, 0)),
                  pl.BlockSpec((None, lc, wi, ws), lambda k, i: (k, 0, 0, 0)),
                  pl.BlockSpec((None, SUBLANES, ws // 2), lambda k, i: (k, 0, 0)),
                  pl.BlockSpec((None, bsz, ws), lambda k, i: (k, 0, 0))],
        out_specs=(pl.BlockSpec((nc, ws), lambda k, i: (0, k)),
                   pl.BlockSpec((None, bsz, ws), lambda k, i: (k, 0, 0))),
        scratch_shapes=[pltpu.VMEM((ws // LANES, nc, LANES), F32)],
        compiler_params=_cparams("arbitrary", "arbitrary"),
        name="s5_state",
    )(u3, bw, coef, h0)
    tc = _tile(nc, 256)
    wso = (go // gi) * ws
    y = pl.pallas_call(
        functools.partial(_s5_out_kernel, lc=lc),
        out_shape=jax.ShapeDtypeStruct(u3.shape, F32),
        grid=(n_groups // go, nc // tc),
        in_specs=[pl.BlockSpec((wo // LANES, tc * lc, LANES), lambda s, i: (s, i, 0)),
                  pl.BlockSpec((tc, wso), lambda s, i: (i, s)),
                  pl.BlockSpec((None, lc, wo, wo), lambda s, i: (s, 0, 0, 0)),
                  pl.BlockSpec((None, lc, wso, wo), lambda s, i: (s, 0, 0, 0), pipeline_mode=pl.Buffered(1))],
        out_specs=pl.BlockSpec((wo // LANES, tc * lc, LANES), lambda s, i: (s, i, 0)),
        scratch_shapes=[pltpu.VMEM((lc * tc, wo), BF16), pltpu.VMEM((lc * tc, wo), F32)],
        compiler_params=_cparams("arbitrary", "arbitrary"),
        name="s5_out",
    )(u3, hin, bd, bv)
    hl = jnp.swapaxes(hl, 0, 1)
    unpack = lambda h: h.reshape(bsz, n_groups, p)
    return y, unpack(hl[:, :, :ws // 2]), unpack(hl[:, :, ws // 2:])


def _glu_kernel(y_ref, u_ref, d_ref, w_ref, b_ref, g_ref, o_ref):
    lanes = lambda r: jnp.concatenate([r[k] for k in range(r.shape[0])], axis=1)
    z = jax.nn.gelu(lanes(y_ref) + d_ref[...] * lanes(u_ref))
    gate = jax.nn.sigmoid(jnp.dot(z.astype(BF16), w_ref[...], preferred_element_type=F32) + b_ref[...])
    o_ref[...] = _rms(z * gate, g_ref[...]).astype(BF16)


def _glu(y, u, d_skip, w_glu, b_glu, g_ssm):
    nblk, t, _ = y.shape
    ds = nblk * LANES
    tm = _tile(t, 512)
    vec = pl.BlockSpec((1, ds), lambda i: (0, 0))
    blk = pl.BlockSpec((nblk, tm, LANES), lambda i: (0, i, 0))
    return pl.pallas_call(
        _glu_kernel,
        out_shape=jax.ShapeDtypeStruct((t, ds), BF16),
        grid=(t // tm,),
        in_specs=[blk, blk, vec, pl.BlockSpec((ds, ds), lambda i: (0, 0)), vec, vec],
        out_specs=pl.BlockSpec((tm, ds), lambda i: (i, 0)),
        compiler_params=_cparams("arbitrary"),
        name="s5_glu",
    )(y, u, d_skip.reshape(1, ds), w_glu, b_glu.reshape(1, ds), g_ssm.reshape(1, ds))


def _outproj_kernel(att_ref, ssm_ref, x_ref, wa_ref, ws_ref, ga_ref, gt_ref, gf_ref, sh_ref, sc_ref, wr_ref,
                    *rest):
    x1_ref, h2_ref, lg_ref = rest[-3:]
    a = _rms(att_ref[...], ga_ref[...]).astype(BF16)
    mixed = (jnp.dot(a, wa_ref[...], preferred_element_type=F32)
             + jnp.dot(ssm_ref[...], ws_ref[...], preferred_element_type=F32))
    x1 = x_ref[...] + gt_ref[...] * mixed
    x1_ref[...] = x1
    h2 = _rms(x1, gf_ref[...]) * (1.0 + sc_ref[...]) + sh_ref[...]
    hi = h2.astype(BF16)
    h2_ref[...] = hi
    lo = (h2 - hi.astype(F32)).astype(BF16)
    r = (jnp.dot(hi, wr_ref[...], preferred_element_type=F32)
         + jnp.dot(lo, wr_ref[...], preferred_element_type=F32))
    lg_ref[...] = r[:, :LANES] + r[:, LANES:]


OUTPROJ_TM = 256


def _outproj(att, ssm_n, x, wa, ws, g_att, gt1, g_ffn, sh2, sc2, wr, seq_len, t_all, row_off, shared):
    t, d = x.shape
    da, ds = att.shape[1], ssm_n.shape[1]
    tm = OUTPROJ_TM
    assert t % tm == 0 and row_off % tm == 0 and t_all % tm == 0
    off = row_off // tm
    gt_op, gt_spec = _mod_operand(gt1, seq_len, tm)
    sh_op, sh_spec = _mod_operand(sh2, seq_len, tm)
    sc_op, sc_spec = _mod_operand(sc2, seq_len, tm)
    row = lambda n: pl.BlockSpec((tm, n), lambda i: (i, 0))
    row_shared = lambda n: pl.BlockSpec((tm, n), lambda i: (i + off, 0))
    const = lambda a, b: pl.BlockSpec((a, b), lambda i: (0, 0))
    operands = [att, ssm_n, x, wa, ws, g_att.reshape(1, da), gt_op, g_ffn.reshape(1, d), sh_op, sc_op, wr]
    in_specs = [row(da), row(ds), row(d), const(da, d), const(ds, d), const(1, da),
                gt_spec, const(1, d), sh_spec, sc_spec, const(d, 2 * LANES)]
    aliases = {}
    if shared is not None:
        aliases = {len(operands): 1, len(operands) + 1: 2}
        operands += list(shared)
        in_specs += [pl.BlockSpec(memory_space=pl.ANY)] * 2
    x1, h2, lg = pl.pallas_call(
        _outproj_kernel,
        out_shape=(jax.ShapeDtypeStruct((t, d), F32), jax.ShapeDtypeStruct((t_all, d), BF16),
                   jax.ShapeDtypeStruct((t_all, LANES), F32)),
        grid=(t // tm,),
        in_specs=in_specs,
        out_specs=(row(d), row_shared(d), row_shared(LANES)),
        input_output_aliases=aliases,
        compiler_params=_cparams("arbitrary"),
        name="out_proj",
    )(*operands)
    return x1, (h2, lg)


MOE_TM = 256
MOE_CHUNKS = 4


def _expert_kernel(te_ref, nu_ref, xs_ref, rw_ref, w1_ref, w3_ref, w2_ref, *rest):
    o_ref, w1_scr, w3_scr, w2_scr = rest[-4:]
    i = pl.program_id(0)
    live = i < nu_ref[0]

    @pl.when(live & ((i == 0) | (te_ref[i] != te_ref[jnp.maximum(i - 1, 0)])))
    def _():
        w1_scr[...] = w1_ref[...].astype(BF16)
        w3_scr[...] = w3_ref[...].astype(BF16)
        w2_scr[...] = w2_ref[...].astype(BF16)

    @pl.when(live)
    def _():
        x = xs_ref[...]
        h1 = jnp.dot(x, w1_scr[...], preferred_element_type=F32)
        h3 = jnp.dot(x, w3_scr[...], preferred_element_type=F32)
        hid = (h1 * jax.nn.sigmoid(h1) * h3).astype(BF16)
        o_ref[...] = jnp.dot(hid, w2_scr[...], preferred_element_type=F32) * rw_ref[...]

    @pl.when(i >= nu_ref[0])
    def _():
        o_ref[...] = jnp.zeros(o_ref.shape, F32)


def _route(logits, n_groups, n_experts, top_k, tm, n_chunks):
    t = logits.shape[0]
    epg = n_experts // n_groups
    tok = jnp.arange(t)
    g_logits = logits[:, :n_groups]
    p_group = jax.nn.softmax(g_logits, axis=-1)
    g_sel = jnp.argmax(g_logits, axis=-1)
    e_sel = logits[:, n_groups:n_groups + n_experts].reshape(t, n_groups, epg)[tok, g_sel]
    top_v, top_i = lax.top_k(e_sel, top_k)
    gate = p_group[tok, g_sel][:, None] * jax.nn.softmax(top_v, axis=-1)
    eid = (g_sel[:, None] * epg + top_i).reshape(-1).astype(jnp.int32)
    n_asg = t * top_k
    onehot = (eid[:, None] == jnp.arange(n_experts, dtype=jnp.int32)[None, :]).astype(jnp.int32)
    csum = jnp.cumsum(onehot, axis=0)
    rank = jnp.take_along_axis(csum, eid[:, None], axis=1)[:, 0] - 1
    counts = csum[-1]
    padded = (counts + tm - 1) // tm * tm
    pends = jnp.cumsum(padded)
    dest = (pends - padded)[eid] + rank
    n_pad = -(-(n_asg + n_experts * (tm - 1)) // (tm * n_chunks)) * (tm * n_chunks)
    row_asg = jnp.full((n_pad,), -1, jnp.int32).at[dest].set(jnp.arange(n_asg, dtype=jnp.int32))
    live = row_asg >= 0
    row_tok = jnp.where(live, row_asg // top_k, 0)
    row_w = jnp.where(live, gate.reshape(-1)[jnp.maximum(row_asg, 0)], 0.0)
    n_tiles = n_pad // tm
    tile_e = jnp.sum(pends[None, :] <= (jnp.arange(n_tiles, dtype=jnp.int32) * tm)[:, None], axis=1)
    tile_e = jnp.minimum(tile_e, n_experts - 1).astype(jnp.int32)
    n_used = (pends[-1] // tm).astype(jnp.int32).reshape(1)
    return row_tok, row_w, tile_e, n_used, dest.reshape(t, top_k)


def _experts(h2, logits, w1, w3, w2, n_groups, top_k):
    t, d = h2.shape
    n_experts, _, de = w1.shape
    tm = MOE_TM
    row_tok, row_w, tile_e, n_used, pos = _route(logits, n_groups, n_experts, top_k, tm, MOE_CHUNKS)
    n_pad = row_tok.shape[0]
    ct = n_pad // tm // MOE_CHUNKS
    rw = row_w.reshape(n_pad, 1)
    single = pl.Buffered(1)
    ys = None
    for c in range(MOE_CHUNKS):
        off = c * ct
        rows = slice(off * tm, (off + ct) * tm)
        operands = [tile_e[off:off + ct], jnp.clip(n_used - off, 0, ct), h2[row_tok[rows]], rw[rows], w1, w3, w2]
        in_specs = [pl.BlockSpec((tm, d), lambda i, te, nu: (i, 0)),
                    pl.BlockSpec((tm, 1), lambda i, te, nu: (i, 0)),
                    pl.BlockSpec((None, d, de), lambda i, te, nu: (te[i], 0, 0), pipeline_mode=single),
                    pl.BlockSpec((None, d, de), lambda i, te, nu: (te[i], 0, 0), pipeline_mode=single),
                    pl.BlockSpec((None, de, d), lambda i, te, nu: (te[i], 0, 0), pipeline_mode=single)]
        aliases = {}
        if ys is not None:
            aliases = {len(operands): 0}
            operands.append(ys)
            in_specs.append(pl.BlockSpec(memory_space=pl.ANY))
        ys = pl.pallas_call(
            _expert_kernel,
            out_shape=jax.ShapeDtypeStruct((n_pad, d), F32),
            grid_spec=pltpu.PrefetchScalarGridSpec(
                num_scalar_prefetch=2,
                grid=(ct,),
                in_specs=in_specs,
                out_specs=pl.BlockSpec((tm, d), lambda i, te, nu: (i + off, 0)),
                scratch_shapes=[pltpu.VMEM((d, de), BF16), pltpu.VMEM((d, de), BF16), pltpu.VMEM((de, d), BF16)]),
            input_output_aliases=aliases,
            compiler_params=_cparams("arbitrary"),
            name="moe_experts",
        )(*operands)
    return ys, pos


def _final_kernel(x_ref, gt_ref, g_ref, *rest, n_rows):
    y_refs, o_ref = rest[:n_rows], rest[-1]
    moe = y_refs[0][...]
    for r in y_refs[1:]:
        moe = moe + r[...]
    o_ref[...] = _rms(x_ref[...] + gt_ref[...] * moe, g_ref[...])


def _final(x1, gt2, g_final, ys, pos, seq_len, n_chunks):
    t, d = x1.shape
    tm = _tile(t, 256)
    ct = t // tm // n_chunks
    assert ct * n_chunks * tm == t
    y = None
    for c in range(n_chunks):
        off = c * ct
        gt_op, gt_spec = _mod_operand(gt2, seq_len, tm, off)
        rows = [ys[pos[off * tm:(off + ct) * tm, k]] for k in range(pos.shape[1])]
        local = pl.BlockSpec((tm, d), lambda i: (i, 0))
        shifted = pl.BlockSpec((tm, d), lambda i: (i + off, 0))
        operands = [x1, gt_op, g_final.reshape(1, d)] + rows
        in_specs = [shifted, gt_spec, pl.BlockSpec((1, d), lambda i: (0, 0))] + [local] * len(rows)
        aliases = {}
        if y is not None:
            aliases = {len(operands): 0}
            operands.append(y)
            in_specs.append(pl.BlockSpec(memory_space=pl.ANY))
        y = pl.pallas_call(
            functools.partial(_final_kernel, n_rows=len(rows)),
            out_shape=jax.ShapeDtypeStruct((t, d), F32),
            grid=(ct,),
            in_specs=in_specs,
            out_specs=shifted,
            input_output_aliases=aliases,
            compiler_params=_cparams("arbitrary"),
            name="moe_combine_norm",
        )(*operands)
    return y


S5_CHUNK = 16
S5_SEGMENTS = 4
TOP_K = 2


def _mixers(x, mod, cache, wts, s5_ops, t_all, row_off, shared):
    (g_mix, w4, wf, bfp, d_skip, w_glu, b_glu, g_att, g_ssm, wa, ws, g_ffn, wr,
     n_heads, hd, n_groups_ssm) = wts
    bsz, seq, d = x.shape
    da = n_heads * hd
    t = bsz * seq
    xt = x.reshape(t, d)
    sh1, sc1, gt1, sh2, sc2, gt2 = jnp.split(mod, 6, axis=-1)
    qb, kf, vf, kb, vb, u, lfp = _inproj(xt, g_mix, sh1, sc1, w4, wf, bfp, seq, hd ** -0.5 * LOG2E)
    logf = lfp[:, :n_heads].reshape(bsz, seq, n_heads)
    if cache is None:
        fcum = _cumsum_rows(jnp.swapaxes(logf, 1, 2).reshape(bsz * n_heads, seq))[:, :seq]
        att = _fox_prompt(qb, kb, vb, (fcum * LOG2E).reshape(bsz, n_heads, seq), bsz, seq, n_heads, hd)
        h0 = jnp.zeros((bsz, n_groups_ssm, s5_ops[3].shape[2] // S5_IN_GROUPS), F32)
        ssm_y, h_re, h_im = _s5(u, h0, h0, s5_ops, bsz, S5_CHUNK, S5_SEGMENTS)
    else:
        cache_k, cache_v, cache_logf, st_re, st_im = cache
        past = cache_k.shape[1]
        lf_all = jnp.concatenate([cache_logf.astype(F32), logf], axis=1)
        f_all = _cumsum_rows(jnp.swapaxes(lf_all, 1, 2).reshape(bsz * n_heads, past + seq))
        f_all = (f_all[:, :past + seq] * LOG2E).reshape(bsz, n_heads, past + seq)
        att = _fox_sample(qb.reshape(bsz, seq, n_heads, hd), kf.reshape(bsz, seq, n_heads, hd),
                          vf.reshape(bsz, seq, n_heads, hd), f_all, cache_k, cache_v)
        att = att.reshape(t, da)
        ssm_y, h_re, h_im = _s5(u, st_re.astype(F32), st_im.astype(F32), s5_ops, bsz, seq, 1)
    ssm_n = _glu(ssm_y, u, d_skip, w_glu, b_glu, g_ssm)
    x1, shared = _outproj(att, ssm_n, xt, wa, ws, g_att, gt1, g_ffn, sh2, sc2, wr, seq, t_all, row_off, shared)
    new_cache = (kf.reshape(bsz, seq, n_heads, hd), vf.reshape(bsz, seq, n_heads, hd), logf, h_re, h_im)
    return x1, gt2, shared, new_cache


def kernel(x_prompt, x_sample, cache_k, cache_v, cache_logf, state_ssm_re, state_ssm_im, c_prompt, c_sample, w_ada, b_ada, g_mix, w_in, b_f, lam_re, lam_im, log_dt, b_re, b_im, c_re, c_im, d_skip, w_glu, b_glu, g_att, g_ssm, w_out, g_ffn, w_rg, w_re, w1, w3, w2, g_final):
    depth = w_ada.shape[0]
    assert depth == 1, "the residual stream of a deeper stack would have to be threaded through the layers"
    n_heads, hd = cache_k.shape[3], cache_k.shape[4]
    da = n_heads * hd
    d = x_prompt.shape[-1]
    ds = d - da
    assert da == ds
    n_groups_ssm = state_ssm_re.shape[2]
    n_expert_groups = w_rg.shape[-1]
    n_experts = w_re.shape[-1]
    assert n_expert_groups + n_experts <= LANES and n_heads <= LANES
    l = 0
    bp = x_prompt.shape[0]
    mod = _ada(jnp.concatenate([c_prompt, c_sample], axis=0).astype(F32), w_ada[l], b_ada[l])
    wi = w_in[l]
    w4 = jnp.concatenate([wi[:, :3 * da], wi[:, 3 * da + n_heads:]], axis=1).astype(BF16)
    wf = jnp.pad(wi[:, 3 * da:3 * da + n_heads], ((0, 0), (0, LANES - n_heads))).astype(BF16)
    bfp = jnp.pad(b_f[l], (0, LANES - n_heads)).reshape(1, LANES).astype(F32)
    wr = jnp.pad(jnp.concatenate([w_rg[l], w_re[l]], axis=1),
                 ((0, 0), (0, LANES - n_expert_groups - n_experts))).astype(F32)
    wr_hi = wr.astype(BF16)
    wr_lo = (wr - wr_hi.astype(F32)).astype(BF16)
    wr2 = jnp.concatenate([wr_hi, wr_lo], axis=1)
    wo = _to_bf16(w_out[l])
    wts = (g_mix[l], w4, wf, bfp, d_skip[l], _to_bf16(w_glu[l]), b_glu[l], g_att[l], g_ssm[l],
           wo[:da], wo[da:], g_ffn[l], wr2, n_heads, hd, n_groups_ssm)
    s5_args = (lam_re[l].astype(F32), lam_im[l].astype(F32), log_dt[l], b_re[l].astype(F32), b_im[l].astype(F32),
               c_re[l].astype(F32), c_im[l].astype(F32))
    seq_p = x_prompt.shape[1]
    n_steps_p = seq_p // (S5_SEGMENTS * S5_CHUNK)
    if x_sample.shape[1] == S5_CHUNK:
        ops_p, ops_s = _s5_operators(*s5_args, S5_CHUNK, (n_steps_p, 1))
    else:
        (ops_p,) = _s5_operators(*s5_args, S5_CHUNK, (n_steps_p,))
        (ops_s,) = _s5_operators(*s5_args, x_sample.shape[1], (1,))
    bs, seq_s = x_sample.shape[:2]
    t_p, t_s = bp * seq_p, bs * seq_s
    x1p, gt2p, shared, (kp, vp, lfp, rep, imp) = _mixers(
        x_prompt.astype(F32), mod[:bp], None, wts, ops_p, t_p + t_s, 0, None)
    cache = (cache_k[l], cache_v[l], cache_logf[l], state_ssm_re[l], state_ssm_im[l])
    x1s, gt2s, (h2, logits), (ksm, vsm, lfs, res, ims) = _mixers(
        x_sample.astype(F32), mod[bp:], cache, wts, ops_s, t_p + t_s, t_p, shared)
    ys, pos = _experts(h2, logits, w1[l], w3[l], w2[l], n_expert_groups, TOP_K)
    yp = _final(x1p, gt2p, g_final, ys, pos[:t_p], seq_p, MOE_CHUNKS).reshape(x_prompt.shape)
    ysm = _final(x1s, gt2s, g_final, ys, pos[t_p:], seq_s, 1).reshape(x_sample.shape)
    return (yp, ysm, kp[None], vp[None], lfp[None], rep[None], imp[None],
            ksm[None], vsm[None], lfs[None], res[None], ims[None])
```

```python
import functools
import math

import jax
import jax.numpy as jnp
from jax import lax
from jax.experimental import pallas as pl
from jax.experimental.pallas import tpu as pltpu

F32 = jnp.float32
BF16 = jnp.bfloat16
EPS = 1e-6
NEG = -1e30
LOG2E = math.log2(math.e)
LANES = 128
SUBLANES = 8
VMEM_LIMIT = 56 * 1024 * 1024
HIGHEST = lax.Precision.HIGHEST
NT_DIMS = (((1,), (1,)), ((), ()))


def _cparams(*sem):
    return pltpu.CompilerParams(dimension_semantics=sem, vmem_limit_bytes=VMEM_LIMIT)


def _tile(n, pref):
    t = min(n, pref)
    assert n % t == 0, (n, pref)
    return t


def _rms(x, g):
    return x * lax.rsqrt(jnp.mean(x * x, axis=-1, keepdims=True) + EPS) * g


def _mod_operand(vec, seq_len, tm, off=0):
    n_seq, d = vec.shape
    if seq_len % tm == 0:
        per = seq_len // tm
        return vec[:, None, :], pl.BlockSpec((None, 1, d), lambda i, *_: ((i + off) // per, 0, 0))
    assert tm % seq_len == 0
    rows = jnp.repeat(vec, seq_len, axis=0).reshape(-1, tm, d)
    return rows, pl.BlockSpec((None, tm, d), lambda i, *_: (i + off, 0, 0))


def _cast_kernel(x_ref, o_ref):
    o_ref[...] = x_ref[...].astype(BF16)


def _to_bf16(w):
    n = w.shape[-1]
    w2 = w.reshape(-1, n)
    rows = w2.shape[0]
    tr = _tile(rows, max(SUBLANES, (1 << 20) // n))
    out = pl.pallas_call(
        _cast_kernel,
        out_shape=jax.ShapeDtypeStruct((rows, n), BF16),
        grid=(rows // tr,),
        in_specs=[pl.BlockSpec((tr, n), lambda i: (i, 0))],
        out_specs=pl.BlockSpec((tr, n), lambda i: (i, 0)),
        compiler_params=_cparams("arbitrary"),
        name="cast_bf16",
    )(w2)
    return out.reshape(w.shape)


def _ada_kernel(c_ref, w_ref, b_ref, o_ref):
    c = c_ref[...]
    a = (c * jax.nn.sigmoid(c)).astype(BF16)
    o_ref[...] = jnp.dot(a, w_ref[...].astype(BF16), preferred_element_type=F32) + b_ref[...]


def _ada(c, w, b):
    s, d = c.shape
    n = w.shape[1]
    tn = _tile(n, 1024)
    return pl.pallas_call(
        _ada_kernel,
        out_shape=jax.ShapeDtypeStruct((s, n), F32),
        grid=(n // tn,),
        in_specs=[pl.BlockSpec((s, d), lambda j: (0, 0)),
                  pl.BlockSpec((d, tn), lambda j: (0, j)),
                  pl.BlockSpec((1, tn), lambda j: (0, j))],
        out_specs=pl.BlockSpec((s, tn), lambda j: (0, j)),
        compiler_params=_cparams("arbitrary"),
        name="ada_mod",
    )(c, w, b.reshape(1, n))


def _inproj_kernel(x_ref, g_ref, sh_ref, sc_ref, w_ref, wf_ref, bf_ref,
                   q_ref, kf_ref, vf_ref, kb_ref, vb_ref, u_ref, lf_ref, h_scr, *, qscale):
    j = pl.program_id(1)

    @pl.when(j == 0)
    def _():
        h = _rms(x_ref[...], g_ref[...]) * (1.0 + sc_ref[...]) + sh_ref[...]
        hb = h.astype(BF16)
        h_scr[...] = hb
        fg = jnp.dot(hb, wf_ref[...], preferred_element_type=F32) + bf_ref[...]
        lf_ref[...] = jnp.minimum(fg, 0.0) - jnp.log1p(jnp.exp(-jnp.abs(fg)))

    p = jnp.dot(h_scr[...], w_ref[...], preferred_element_type=F32)

    @pl.when(j == 0)
    def _():
        q_ref[...] = (p * qscale).astype(BF16)

    @pl.when(j == 1)
    def _():
        kf_ref[...] = p
        kb_ref[...] = p.astype(BF16)

    @pl.when(j == 2)
    def _():
        vf_ref[...] = p
        vb_ref[...] = p.astype(BF16)

    @pl.when(j == 3)
    def _():
        for k in range(u_ref.shape[0]):
            u_ref[k] = p[:, k * LANES:(k + 1) * LANES]


def _inproj(x, g, sh, sc, w4, wf, bfp, seq_len, qscale):
    t, d = x.shape
    da = w4.shape[1] // 4
    tm = _tile(t, 512)
    sh_op, sh_spec = _mod_operand(sh, seq_len, tm)
    sc_op, sc_spec = _mod_operand(sc, seq_len, tm)
    row = lambda i, j: (i, 0)
    outs = pl.pallas_call(
        functools.partial(_inproj_kernel, qscale=qscale),
        out_shape=(jax.ShapeDtypeStruct((t, da), BF16),
                   jax.ShapeDtypeStruct((t, da), F32), jax.ShapeDtypeStruct((t, da), F32),
                   jax.ShapeDtypeStruct((t, da), BF16), jax.ShapeDtypeStruct((t, da), BF16),
                   jax.ShapeDtypeStruct((da // LANES, t, LANES), F32),
                   jax.ShapeDtypeStruct((t, LANES), F32)),
        grid=(t // tm, 4),
        in_specs=[pl.BlockSpec((tm, d), row),
                  pl.BlockSpec((1, d), lambda i, j: (0, 0)),
                  sh_spec, sc_spec,
                  pl.BlockSpec((d, da), lambda i, j: (0, j)),
                  pl.BlockSpec((d, LANES), lambda i, j: (0, 0)),
                  pl.BlockSpec((1, LANES), lambda i, j: (0, 0))],
        out_specs=(pl.BlockSpec((tm, da), row),) * 5
                  + (pl.BlockSpec((da // LANES, tm, LANES), lambda i, j: (0, i, 0)), pl.BlockSpec((tm, LANES), row)),
        scratch_shapes=[pltpu.VMEM((tm, d), BF16)],
        compiler_params=_cparams("arbitrary", "arbitrary"),
        name="in_proj",
    )(x, g.reshape(1, d), sh_op, sc_op, w4, wf, bfp)
    return outs


def _cumsum_kernel(x_ref, o_ref):
    sb, nb, _ = x_ref.shape
    li = lax.broadcasted_iota(jnp.int32, (LANES, LANES), 0)
    lj = lax.broadcasted_iota(jnp.int32, (LANES, LANES), 1)
    upper = (li <= lj).astype(F32)
    ri = lax.broadcasted_iota(jnp.int32, (nb, nb), 0)
    rj = lax.broadcasted_iota(jnp.int32, (nb, nb), 1)
    strict = (rj < ri).astype(F32)
    for s in range(sb):
        within = jnp.dot(x_ref[s], upper, precision=HIGHEST, preferred_element_type=F32)
        tot = jnp.broadcast_to(within[:, LANES - 1:LANES], (nb, LANES))
        off = jnp.dot(strict, tot, precision=HIGHEST, preferred_element_type=F32)
        o_ref[s] = within + off


def _cumsum_rows(x):
    n_rows, n = x.shape
    nb = -(-n // (LANES * SUBLANES)) * SUBLANES
    xp = jnp.pad(x, ((0, 0), (0, nb * LANES - n))).reshape(n_rows, nb, LANES)
    sb = _tile(n_rows, 16)
    out = pl.pallas_call(
        _cumsum_kernel,
        out_shape=jax.ShapeDtypeStruct((n_rows, nb, LANES), F32),
        grid=(n_rows // sb,),
        in_specs=[pl.BlockSpec((sb, nb, LANES), lambda i: (i, 0, 0))],
        out_specs=pl.BlockSpec((sb, nb, LANES), lambda i: (i, 0, 0)),
        compiler_params=_cparams("arbitrary"),
        name="logf_cumsum",
    )(xp)
    return out.reshape(n_rows, nb * LANES)


FOX_TQ = 2048
FOX_TK = 2048
FOX_SUB = 1024


def _col_from_row(row):
    n = row.shape[1]
    eye = lax.broadcasted_iota(jnp.int32, (n, n), 0) == lax.broadcasted_iota(jnp.int32, (n, n), 1)
    return jnp.sum(jnp.where(eye, jnp.broadcast_to(row, (n, n)), 0.0), axis=1, keepdims=True)


def _lane_tiles(x):
    return [x[:, j * LANES:(j + 1) * LANES] for j in range(x.shape[1] // LANES)]


def _fox_kernel(q_ref, k_ref, v_ref, f_ref, o_ref, m_scr, l_scr, acc_scr, *, tq, tk, sub):
    qi = pl.program_id(2)
    nsub = tq // sub
    fqb = [jnp.broadcast_to(_col_from_row(f_ref[qi * nsub + a]), (sub, LANES)) for a in range(nsub)]
    m_scr[...] = jnp.full(m_scr.shape, NEG, F32)
    l_scr[...] = jnp.zeros(l_scr.shape, F32)
    acc_scr[...] = jnp.zeros(acc_scr.shape, F32)

    def chain(a, k, v, fk, diagonal):
        rows = pl.ds(a * sub, sub)
        t1 = lax.dot_general(q_ref[rows, :], k, NT_DIMS, preferred_element_type=F32) - fk
        if diagonal:
            row = lax.broadcasted_iota(jnp.int32, t1.shape, 0)
            col = lax.broadcasted_iota(jnp.int32, t1.shape, 1)
            t1 = jnp.where(col <= row, t1, NEG)
        tiles = _lane_tiles(t1)
        part = functools.reduce(jnp.maximum, tiles)
        m_prev = m_scr[rows, :]
        m_new = jnp.maximum(m_prev, jnp.max(part, axis=-1, keepdims=True) + fqb[a])
        c = m_new - fqb[a]
        p = [jnp.exp2(t - c) for t in tiles]
        alpha = jnp.exp2(m_prev - m_new)
        l_scr[rows, :] = alpha * l_scr[rows, :] + functools.reduce(jnp.add, p)
        pv = jnp.dot(jnp.concatenate(p, axis=1).astype(BF16), v, preferred_element_type=F32)
        acc_scr[rows, :] = alpha * acc_scr[rows, :] + pv
        m_scr[rows, :] = m_new

    def full_step(kt, carry):
        ks = pl.multiple_of(kt * tk, tk)
        k = k_ref[pl.ds(ks, tk), :]
        v = v_ref[pl.ds(ks, tk), :]
        fk = jnp.concatenate([f_ref[kt * (tk // sub) + j] for j in range(tk // sub)], axis=1)
        for a in range(nsub):
            chain(a, k, v, fk, False)
        return carry

    lax.fori_loop(0, qi * (tq // tk), full_step, 0)
    for a in range(nsub):
        for j in range(a + 1):
            ks = pl.multiple_of((qi * nsub + j) * sub, sub)
            chain(a, k_ref[pl.ds(ks, sub), :], v_ref[pl.ds(ks, sub), :], f_ref[qi * nsub + j], j == a)
    o_ref[...] = acc_scr[...] / jnp.sum(l_scr[...], axis=-1, keepdims=True)


def _fox_prompt(qb, kb, vb, fcum2, bsz, seq, n_heads, hd):
    assert hd == LANES
    tq = _tile(seq, FOX_TQ)
    tk = _tile(tq, FOX_TK)
    sub = _tile(tk, FOX_SUB)
    nq = seq // tq
    f = fcum2.reshape(bsz, n_heads, seq // sub, 1, sub)
    return pl.pallas_call(
        functools.partial(_fox_kernel, tq=tq, tk=tk, sub=sub),
        out_shape=jax.ShapeDtypeStruct((bsz * seq, n_heads * hd), F32),
        grid=(bsz, n_heads, nq),
        in_specs=[pl.BlockSpec((tq, hd), lambda b, h, i: (b * nq + i, h)),
                  pl.BlockSpec((seq, hd), lambda b, h, i: (b, h)),
                  pl.BlockSpec((seq, hd), lambda b, h, i: (b, h)),
                  pl.BlockSpec((None, None, seq // sub, 1, sub), lambda b, h, i: (b, h, 0, 0, 0))],
        out_specs=pl.BlockSpec((tq, hd), lambda b, h, i: (b * nq + i, h)),
        scratch_shapes=[pltpu.VMEM((tq, LANES), F32), pltpu.VMEM((tq, LANES), F32), pltpu.VMEM((tq, hd), F32)],
        compiler_params=_cparams("arbitrary", "arbitrary", "arbitrary"),
        name="fox_prompt",
    )(qb, kb, vb, f)


def _fox_sample_kernel(q_ref, ck_ref, cv_ref, kn_ref, vn_ref, fq_ref, fkc_ref, fkn_ref,
                       rh_ref, ri_ref, lh_ref, lhn_ref, kin_ref, o_ref, m_scr, l_scr, acc_scr):
    kt = pl.program_id(1)

    @pl.when(kt == 0)
    def _():
        m_scr[...] = jnp.full(m_scr.shape, NEG, F32)
        l_scr[...] = jnp.zeros(l_scr.shape, F32)
        acc_scr[...] = jnp.zeros(acc_scr.shape, F32)

    q = q_ref[...]
    fq = fq_ref[...]
    n_heads = ck_ref.shape[1]
    s_new = q.shape[0] // n_heads

    def update(k4, v4, t1_of):
        n, h, d = k4.shape
        k2 = k4.reshape(n * h, d).astype(BF16)
        v2 = v4.reshape(n * h, d).astype(BF16)
        t1 = t1_of(lax.dot_general(q, k2, NT_DIMS, preferred_element_type=F32))
        m_prev = m_scr[...]
        m_new = jnp.maximum(m_prev, jnp.max(t1, axis=-1, keepdims=True) + fq)
        alpha = jnp.exp2(m_prev - m_new)
        p = jnp.exp2(t1 - (m_new - fq))
        l_scr[...] = alpha * l_scr[...] + jnp.sum(p, axis=-1, keepdims=True)
        acc_scr[...] = alpha * acc_scr[...] + jnp.dot(p.astype(BF16), v2, preferred_element_type=F32)
        m_scr[...] = m_new

    def cache_t1(s):
        head = lax.broadcasted_iota(jnp.int32, (n_heads, 1), 0)
        fkm = jnp.where(lh_ref[...] == head, fkc_ref[...], -NEG)
        return jnp.concatenate([s[h * s_new:(h + 1) * s_new] - fkm[h:h + 1] for h in range(n_heads)], axis=0)

    update(ck_ref[...], cv_ref[...], cache_t1)

    @pl.when(kt == pl.num_programs(1) - 1)
    def _():
        causal_head = jnp.where(kin_ref[...] <= ri_ref[...], lhn_ref[...], -1)
        update(kn_ref[...], vn_ref[...],
               lambda s: jnp.where(rh_ref[...] == causal_head, s - fkn_ref[...], NEG))
        o_ref[...] = acc_scr[...] / l_scr[...]


def _fox_sample(qb, k_new, v_new, f_all2, cache_k, cache_v):
    bsz, s_new, n_heads, hd = qb.shape
    past = cache_k.shape[1]
    tk = _tile(past, 1024)
    n_rows = n_heads * s_new
    assert n_rows % 16 == 0 and s_new % SUBLANES == 0
    q2 = jnp.swapaxes(qb, 1, 2).reshape(bsz, n_rows, hd)
    fq = f_all2[:, :, past:].reshape(bsz, n_rows, 1)
    fkc = jnp.swapaxes(f_all2[:, :, :past], 1, 2).reshape(bsz, 1, past * n_heads)
    fkn = jnp.swapaxes(f_all2[:, :, past:], 1, 2).reshape(bsz, 1, s_new * n_heads)
    r = jnp.arange(n_rows, dtype=jnp.int32).reshape(n_rows, 1)
    lane = lambda n: jnp.arange(n * n_heads, dtype=jnp.int32).reshape(1, n * n_heads)
    const = lambda a: pl.BlockSpec(a.shape, lambda b, j: (0,) * a.ndim)
    consts = (r // s_new, r % s_new, lane(tk) % n_heads, lane(s_new) % n_heads, lane(s_new) // n_heads)
    out = pl.pallas_call(
        _fox_sample_kernel,
        out_shape=jax.ShapeDtypeStruct((bsz, n_rows, hd), F32),
        grid=(bsz, past // tk),
        in_specs=[pl.BlockSpec((None, n_rows, hd), lambda b, j: (b, 0, 0)),
                  pl.BlockSpec((None, tk, n_heads, hd), lambda b, j: (b, j, 0, 0)),
                  pl.BlockSpec((None, tk, n_heads, hd), lambda b, j: (b, j, 0, 0)),
                  pl.BlockSpec((None, s_new, n_heads, hd), lambda b, j: (b, 0, 0, 0)),
                  pl.BlockSpec((None, s_new, n_heads, hd), lambda b, j: (b, 0, 0, 0)),
                  pl.BlockSpec((None, n_rows, 1), lambda b, j: (b, 0, 0)),
                  pl.BlockSpec((None, 1, tk * n_heads), lambda b, j: (b, 0, j)),
                  pl.BlockSpec((None, 1, s_new * n_heads), lambda b, j: (b, 0, 0))]
                 + [const(a) for a in consts],
        out_specs=pl.BlockSpec((None, n_rows, hd), lambda b, j: (b, 0, 0)),
        scratch_shapes=[pltpu.VMEM((n_rows, 1), F32), pltpu.VMEM((n_rows, 1), F32), pltpu.VMEM((n_rows, hd), F32)],
        compiler_params=_cparams("arbitrary", "arbitrary"),
        name="fox_sample",
    )(q2, cache_k, cache_v, k_new, v_new, fq, fkc, fkn, *consts)
    return jnp.swapaxes(out.reshape(bsz, n_heads, s_new, hd), 1, 2)


S5_IN_GROUPS = 8
S5_OUT_GROUPS = 16


def _s5_operators(lam_re, lam_im, log_dt, b_re, b_im, c_re, c_im, lc, n_steps):
    g, p = lam_re.shape
    hc = b_re.shape[2]
    gi, go = S5_IN_GROUPS, S5_OUT_GROUPS
    dt = jnp.exp(log_dt.astype(F32))[:, None]

    def power(k):
        mag = jnp.exp(lam_re * dt * k)
        return mag * jnp.cos(lam_im * dt * k), mag * jnp.sin(lam_im * dt * k)

    lbr, lbi = power(1.0)
    den = lam_re * lam_re + lam_im * lam_im
    fr = ((lbr - 1.0) * lam_re + lbi * lam_im) / den
    fi = (lbi * lam_re - (lbr - 1.0) * lam_im) / den
    bbr = fr[:, :, None] * b_re - fi[:, :, None] * b_im
    bbi = fr[:, :, None] * b_im + fi[:, :, None] * b_re
    ks = jnp.arange(lc + 1, dtype=F32)[None, :, None]
    mag = jnp.exp(lam_re[:, None, :] * dt[:, None, :] * ks)
    ang = lam_im[:, None, :] * dt[:, None, :] * ks
    pwr, pwi = mag * jnp.cos(ang), mag * jnp.sin(ang)
    cr, ci = jnp.swapaxes(c_re, 1, 2), jnp.swapaxes(c_im, 1, 2)
    d_r = bbr[:, :, :, None] * cr[:, :, None, :] - bbi[:, :, :, None] * ci[:, :, None, :]
    d_i = bbr[:, :, :, None] * ci[:, :, None, :] + bbi[:, :, :, None] * cr[:, :, None, :]
    kern = (jnp.einsum("gtp,gpab->gtab", pwr[:, :lc], d_r, precision=HIGHEST)
            - jnp.einsum("gtp,gpab->gtab", pwi[:, :lc], d_i, precision=HIGHEST))
    def block_diag(vals, row_block, n_blocks):
        w = vals.shape[-1]
        tiled = jnp.tile(vals, (1,) * (vals.ndim - 1) + (n_blocks,))
        col_block = jnp.arange(n_blocks * w) // w
        return jnp.where(row_block[:, None] == col_block[None, :], tiled, 0.0).astype(BF16)

    per_slab = lambda x, n: jnp.swapaxes(x.reshape((g // n, n) + x.shape[1:]), 1, 2)
    bd = block_diag(per_slab(kern, go).reshape(g // go, lc, go * hc, hc), jnp.arange(go * hc) // hc, go)
    rev_r, rev_i = pwr[:, lc - 1::-1][:, :lc], pwi[:, lc - 1::-1][:, :lc]
    bt_r, bt_i = jnp.swapaxes(bbr, 1, 2)[:, None], jnp.swapaxes(bbi, 1, 2)[:, None]
    w_r = rev_r[:, :, None, :] * bt_r - rev_i[:, :, None, :] * bt_i
    w_i = rev_r[:, :, None, :] * bt_i + rev_i[:, :, None, :] * bt_r
    place = lambda w: block_diag(per_slab(w, gi).reshape(g // gi, lc, gi * hc, p), jnp.arange(gi * hc) // hc, gi)
    bw = jnp.concatenate([place(w_r), place(w_i)], axis=-1)
    nr, ni = jnp.swapaxes(pwr[:, 1:], 1, 2), jnp.swapaxes(pwi[:, 1:], 1, 2)
    v_r = cr[:, :, None, :] * nr[:, :, :, None] - ci[:, :, None, :] * ni[:, :, :, None]
    v_i = cr[:, :, None, :] * ni[:, :, :, None] + ci[:, :, None, :] * nr[:, :, :, None]
    halves = go // gi
    split = lambda v: jnp.transpose(v.reshape(g // go, halves, gi, p, lc, hc), (0, 4, 1, 2, 3, 5))
    bv = jnp.stack([split(v_r), split(-v_i)], axis=3)
    r = jnp.arange(halves * 2 * gi * p)
    row_group = (r // (2 * gi * p)) * gi + (r // p) % gi
    bv = block_diag(bv.reshape(g // go, lc, halves * 2 * gi * p, hc), row_group, go)

    def coef(n_steps):
        jr, ji = power(float(lc * n_steps))
        c = jnp.stack([pwr[:, lc], pwi[:, lc], jr, ji], axis=1)
        c = jnp.swapaxes(c.reshape(g // gi, gi, 4, p), 1, 2).reshape(g // gi, 4, gi * p)
        return jnp.pad(c, ((0, 0), (0, SUBLANES - 4), (0, 0)))

    return [(bd, bw, bv, coef(n)) for n in n_steps]


def _s5_state_kernel(u_ref, bw_ref, coef_ref, h0_ref, hin_ref, hl_ref, s_scr, *, lc, n_q, n_seg):
    i = pl.program_id(1)
    tr = u_ref.shape[0] // lc
    n_tiles = s_scr.shape[0]
    nh = n_tiles // 2
    half = nh * LANES
    n_steps = s_scr.shape[1] // n_q
    acc = None
    for s in range(lc):
        a = u_ref[pl.ds(s, tr, stride=lc), :].astype(BF16)
        d = jnp.dot(a, bw_ref[s], preferred_element_type=F32)
        acc = d if acc is None else acc + d
    if n_steps == 1:
        rows = pl.ds(pl.multiple_of(i * tr, tr), tr)
    else:
        assert n_steps % tr == 0
        per = n_steps // tr
        rows = pl.ds(lax.rem(i, per) * (tr * n_q) + lax.div(i, per), tr, stride=n_q)
    for m in range(n_tiles):
        s_scr[m, rows, :] = acc[:, m * LANES:(m + 1) * LANES]

    @pl.when(i == pl.num_programs(1) - 1)
    def _():
        bc = lambda r: jnp.broadcast_to(coef_ref[r:r + 1, :], (n_q, half))
        ar, ai = bc(0), bc(1)
        jr, ji = coef_ref[2:3, :], coef_ref[3:4, :]
        step_rows = lambda j: pl.ds(pl.multiple_of(j * n_q, n_q), n_q)

        def load_rows(j):
            return jnp.concatenate([s_scr[m, step_rows(j), :] for m in range(n_tiles)], axis=1)

        def store_rows(j, x):
            for m in range(n_tiles):
                s_scr[m, step_rows(j), :] = x[:, m * LANES:(m + 1) * LANES]

        def scan_zero(j, carry):
            xr, xi = carry
            s = load_rows(j)
            store_rows(j, jnp.concatenate([xr, xi], axis=1))
            return ar * xr - ai * xi + s[:, :half], ar * xi + ai * xr + s[:, half:]

        zero = jnp.zeros((n_q, half), F32)
        xr_end, xi_end = lax.fori_loop(0, n_steps, scan_zero, (zero, zero))

        h0 = h0_ref[...]
        if n_seg == 1:
            er, ei = h0[:, :half], h0[:, half:]
            hl_ref[...] = jnp.concatenate([jr * er - ji * ei + xr_end, jr * ei + ji * er + xi_end], axis=1)
        else:
            er_rows, ei_rows = [], []
            for b in range(n_q // n_seg):
                r_, i_ = h0[b:b + 1, :half], h0[b:b + 1, half:]
                for sg in range(n_seg):
                    q = b * n_seg + sg
                    er_rows.append(r_)
                    ei_rows.append(i_)
                    r_, i_ = (jr * r_ - ji * i_ + xr_end[q:q + 1], jr * i_ + ji * r_ + xi_end[q:q + 1])
                hl_ref[b:b + 1, :] = jnp.concatenate([r_, i_], axis=1)
            er = jnp.concatenate(er_rows, axis=0)
            ei = jnp.concatenate(ei_rows, axis=0)

        def scan_fix(j, carry):
            fr, fi = carry
            store_rows(j, load_rows(j) + jnp.concatenate([fr, fi], axis=1))
            return ar * fr - ai * fi, ar * fi + ai * fr

        lax.fori_loop(0, n_steps, scan_fix, (er, ei))
        for m in range(n_tiles):
            if n_steps == 1:
                hin_ref[:, m * LANES:(m + 1) * LANES] = s_scr[m].astype(BF16)
            else:
                for q in range(n_q):
                    hin_ref[q * n_steps:(q + 1) * n_steps, m * LANES:(m + 1) * LANES] = (
                        s_scr[m, pl.ds(q, n_steps, stride=n_q), :].astype(BF16))


def _s5_out_kernel(u_ref, hin_ref, bd_ref, bv_ref, y_ref, a_scr, acc_scr, *, lc):
    tc = hin_ref.shape[0]
    n_blk = u_ref.shape[0]
    for s in range(lc):
        for h in range(n_blk):
            a_scr[s * tc:(s + 1) * tc, h * LANES:(h + 1) * LANES] = (
                u_ref[h, pl.ds(s, tc, stride=lc), :].astype(BF16))
    acc_scr[...] = jnp.dot(a_scr[...], bd_ref[0], preferred_element_type=F32)
    for tau in range(1, lc):
        n = (lc - tau) * tc
        acc_scr[tau * tc:, :] += jnp.dot(a_scr[:n, :], bd_ref[tau], preferred_element_type=F32)
    hin = hin_ref[...]
    for t in range(lc):
        y_t = acc_scr[t * tc:(t + 1) * tc, :] + jnp.dot(hin, bv_ref[t], preferred_element_type=F32)
        for h in range(n_blk):
            y_ref[h, pl.ds(t, tc, stride=lc), :] = y_t[:, h * LANES:(h + 1) * LANES]


def _s5(u3, h0_re, h0_im, ops, bsz, lc, n_seg):
    bd, bw, bv, coef = ops
    t = u3.shape[1]
    ds = u3.shape[0] * LANES
    n_groups, p = h0_re.shape[1], h0_re.shape[2]
    gi, go = S5_IN_GROUPS, S5_OUT_GROUPS
    wi, wo = ds // (n_groups // gi), ds // (n_groups // go)
    ws = gi * 2 * p
    nc = t // lc
    n_q = bsz * n_seg
    assert nc % n_q == 0 and n_q % SUBLANES == 0 and wi == LANES
    tr = _tile(nc if nc == n_q else nc // n_q, 256)
    pack = lambda h: h.reshape(bsz, n_groups // gi, gi * p)
    h0 = jnp.swapaxes(jnp.concatenate([pack(h0_re), pack(h0_im)], axis=-1), 0, 1)
    hin, hl = pl.pallas_call(
        functools.partial(_s5_state_kernel, lc=lc, n_q=n_q, n_seg=n_seg),
        out_shape=(jax.ShapeDtypeStruct((nc, (n_groups // gi) * ws), BF16),
                   jax.ShapeDtypeStruct((n_groups // gi, bsz, ws), F32)),
        grid=(n_groups // gi, nc // tr),
        in_specs=[pl.BlockSpec((None, tr * lc, wi), lambda k, i: (k, i, 0)),
                  pl.BlockSpec((None, lc, wi, ws), lambda k, i: (k, 0, 0, 0)),
                  pl.BlockSpec((None, SUBLANES, ws // 2), lambda k, i: (k, 0, 0)),
                  pl.BlockSpec((None, bsz, ws), lambda k, i: (k, 0, 0))],
        out_specs=(pl.BlockSpec((nc, ws), lambda k, i: (0, k)),
                   pl.BlockSpec((None, bsz, ws), lambda k, i: (k, 0, 0))),
        scratch_shapes=[pltpu.VMEM((ws // LANES, nc, LANES), F32)],
        compiler_params=_cparams("arbitrary", "arbitrary"),
        name="s5_state",
    )(u3, bw, coef, h0)
    tc = _tile(nc, 256)
    wso = (go // gi) * ws
    y = pl.pallas_call(
        functools.partial(_s5_out_kernel, lc=lc),
        out_shape=jax.ShapeDtypeStruct(u3.shape, F32),
        grid=(n_groups // go, nc // tc),
        in_specs=[pl.BlockSpec((wo // LANES, tc * lc, LANES), lambda s, i: (s, i, 0)),
                  pl.BlockSpec((tc, wso), lambda s, i: (i, s)),
                  pl.BlockSpec((None, lc, wo, wo), lambda s, i: (s, 0, 0, 0)),
                  pl.BlockSpec((None, lc, wso, wo), lambda s, i: (s, 0, 0, 0), pipeline_mode=pl.Buffered(1))],
        out_specs=pl.BlockSpec((wo // LANES, tc * lc, LANES), lambda s, i: (s, i, 0)),
        scratch_shapes=[pltpu.VMEM((lc * tc, wo), BF16), pltpu.VMEM((lc * tc, wo), F32)],
        compiler_params=_cparams("arbitrary", "arbitrary"),
        name="s5_out",
    )(u3, hin, bd, bv)
    hl = jnp.swapaxes(hl, 0, 1)
    unpack = lambda h: h.reshape(bsz, n_groups, p)
    return y, unpack(hl[:, :, :ws // 2]), unpack(hl[:, :, ws // 2:])


def _glu_kernel(y_ref, u_ref, d_ref, w_ref, b_ref, g_ref, o_ref):
    lanes = lambda r: jnp.concatenate([r[k] for k in range(r.shape[0])], axis=1)
    z = jax.nn.gelu(lanes(y_ref) + d_ref[...] * lanes(u_ref))
    gate = jax.nn.sigmoid(jnp.dot(z.astype(BF16), w_ref[...], preferred_element_type=F32) + b_ref[...])
    o_ref[...] = _rms(z * gate, g_ref[...]).astype(BF16)


def _glu(y, u, d_skip, w_glu, b_glu, g_ssm):
    nblk, t, _ = y.shape
    ds = nblk * LANES
    tm = _tile(t, 512)
    vec = pl.BlockSpec((1, ds), lambda i: (0, 0))
    blk = pl.BlockSpec((nblk, tm, LANES), lambda i: (0, i, 0))
    return pl.pallas_call(
        _glu_kernel,
        out_shape=jax.ShapeDtypeStruct((t, ds), BF16),
        grid=(t // tm,),
        in_specs=[blk, blk, vec, pl.BlockSpec((ds, ds), lambda i: (0, 0)), vec, vec],
        out_specs=pl.BlockSpec((tm, ds), lambda i: (i, 0)),
        compiler_params=_cparams("arbitrary"),
        name="s5_glu",
    )(y, u, d_skip.reshape(1, ds), w_glu, b_glu.reshape(1, ds), g_ssm.reshape(1, ds))


def _outproj_kernel(att_ref, ssm_ref, x_ref, wa_ref, ws_ref, ga_ref, gt_ref, gf_ref, sh_ref, sc_ref, wr_ref,
                    *rest):
    x1_ref, h2_ref, lg_ref = rest[-3:]
    a = _rms(att_ref[...], ga_ref[...]).astype(BF16)
    mixed = (jnp.dot(a, wa_ref[...], preferred_element_type=F32)
             + jnp.dot(ssm_ref[...], ws_ref[...], preferred_element_type=F32))
    x1 = x_ref[...] + gt_ref[...] * mixed
    x1_ref[...] = x1
    h2 = _rms(x1, gf_ref[...]) * (1.0 + sc_ref[...]) + sh_ref[...]
    hi = h2.astype(BF16)
    h2_ref[...] = hi
    lo = (h2 - hi.astype(F32)).astype(BF16)
    r = (jnp.dot(hi, wr_ref[...], preferred_element_type=F32)
         + jnp.dot(lo, wr_ref[...], preferred_element_type=F32))
    lg_ref[...] = r[:, :LANES] + r[:, LANES:]


OUTPROJ_TM = 256


def _outproj(att, ssm_n, x, wa, ws, g_att, gt1, g_ffn, sh2, sc2, wr, seq_len, t_all, row_off, shared):
    t, d = x.shape
    da, ds = att.shape[1], ssm_n.shape[1]
    tm = OUTPROJ_TM
    assert t % tm == 0 and row_off % tm == 0 and t_all % tm == 0
    off = row_off // tm
    gt_op, gt_spec = _mod_operand(gt1, seq_len, tm)
    sh_op, sh_spec = _mod_operand(sh2, seq_len, tm)
    sc_op, sc_spec = _mod_operand(sc2, seq_len, tm)
    row = lambda n: pl.BlockSpec((tm, n), lambda i: (i, 0))
    row_shared = lambda n: pl.BlockSpec((tm, n), lambda i: (i + off, 0))
    const = lambda a, b: pl.BlockSpec((a, b), lambda i: (0, 0))
    operands = [att, ssm_n, x, wa, ws, g_att.reshape(1, da), gt_op, g_ffn.reshape(1, d), sh_op, sc_op, wr]
    in_specs = [row(da), row(ds), row(d), const(da, d), const(ds, d), const(1, da),
                gt_spec, const(1, d), sh_spec, sc_spec, const(d, 2 * LANES)]
    aliases = {}
    if shared is not None:
        aliases = {len(operands): 1, len(operands) + 1: 2}
        operands += list(shared)
        in_specs += [pl.BlockSpec(memory_space=pl.ANY)] * 2
    x1, h2, lg = pl.pallas_call(
        _outproj_kernel,
        out_shape=(jax.ShapeDtypeStruct((t, d), F32), jax.ShapeDtypeStruct((t_all, d), BF16),
                   jax.ShapeDtypeStruct((t_all, LANES), F32)),
        grid=(t // tm,),
        in_specs=in_specs,
        out_specs=(row(d), row_shared(d), row_shared(LANES)),
        input_output_aliases=aliases,
        compiler_params=_cparams("arbitrary"),
        name="out_proj",
    )(*operands)
    return x1, (h2, lg)


MOE_TM = 256
MOE_CHUNKS = 4


def _expert_kernel(te_ref, nu_ref, xs_ref, rw_ref, w1_ref, w3_ref, w2_ref, *rest):
    o_ref, w1_scr, w3_scr, w2_scr = rest[-4:]
    i = pl.program_id(0)
    live = i < nu_ref[0]

    @pl.when(live & ((i == 0) | (te_ref[i] != te_ref[jnp.maximum(i - 1, 0)])))
    def _():
        w1_scr[...] = w1_ref[...].astype(BF16)
        w3_scr[...] = w3_ref[...].astype(BF16)
        w2_scr[...] = w2_ref[...].astype(BF16)

    @pl.when(live)
    def _():
        x = xs_ref[...]
        h1 = jnp.dot(x, w1_scr[...], preferred_element_type=F32)
        h3 = jnp.dot(x, w3_scr[...], preferred_element_type=F32)
        hid = (h1 * jax.nn.sigmoid(h1) * h3).astype(BF16)
        o_ref[...] = jnp.dot(hid, w2_scr[...], preferred_element_type=F32) * rw_ref[...]

    @pl.when(i >= nu_ref[0])
    def _():
        o_ref[...] = jnp.zeros(o_ref.shape, F32)


def _route(logits, n_groups, n_experts, top_k, tm, n_chunks):
    t = logits.shape[0]
    epg = n_experts // n_groups
    tok = jnp.arange(t)
    g_logits = logits[:, :n_groups]
    p_group = jax.nn.softmax(g_logits, axis=-1)
    g_sel = jnp.argmax(g_logits, axis=-1)
    e_sel = logits[:, n_groups:n_groups + n_experts].reshape(t, n_groups, epg)[tok, g_sel]
    top_v, top_i = lax.top_k(e_sel, top_k)
    gate = p_group[tok, g_sel][:, None] * jax.nn.softmax(top_v, axis=-1)
    eid = (g_sel[:, None] * epg + top_i).reshape(-1).astype(jnp.int32)
    n_asg = t * top_k
    onehot = (eid[:, None] == jnp.arange(n_experts, dtype=jnp.int32)[None, :]).astype(jnp.int32)
    csum = jnp.cumsum(onehot, axis=0)
    rank = jnp.take_along_axis(csum, eid[:, None], axis=1)[:, 0] - 1
    counts = csum[-1]
    padded = (counts + tm - 1) // tm * tm
    pends = jnp.cumsum(padded)
    dest = (pends - padded)[eid] + rank
    n_pad = -(-(n_asg + n_experts * (tm - 1)) // (tm * n_chunks)) * (tm * n_chunks)
    row_asg = jnp.full((n_pad,), -1, jnp.int32).at[dest].set(jnp.arange(n_asg, dtype=jnp.int32))
    live = row_asg >= 0
    row_tok = jnp.where(live, row_asg // top_k, 0)
    row_w = jnp.where(live, gate.reshape(-1)[jnp.maximum(row_asg, 0)], 0.0)
    n_tiles = n_pad // tm
    tile_e = jnp.sum(pends[None, :] <= (jnp.arange(n_tiles, dtype=jnp.int32) * tm)[:, None], axis=1)
    tile_e = jnp.minimum(tile_e, n_experts - 1).astype(jnp.int32)
    n_used = (pends[-1] // tm).astype(jnp.int32).reshape(1)
    return row_tok, row_w, tile_e, n_used, dest.reshape(t, top_k)


def _experts(h2, logits, w1, w3, w2, n_groups, top_k):
    t, d = h2.shape
    n_experts, _, de = w1.shape
    tm = MOE_TM
    row_tok, row_w, tile_e, n_used, pos = _route(logits, n_groups, n_experts, top_k, tm, MOE_CHUNKS)
    n_pad = row_tok.shape[0]
    ct = n_pad // tm // MOE_CHUNKS
    rw = row_w.reshape(n_pad, 1)
    single = pl.Buffered(1)
    ys = None
    for c in range(MOE_CHUNKS):
        off = c * ct
        rows = slice(off * tm, (off + ct) * tm)
        operands = [tile_e[off:off + ct], jnp.clip(n_used - off, 0, ct), h2[row_tok[rows]], rw[rows], w1, w3, w2]
        in_specs = [pl.BlockSpec((tm, d), lambda i, te, nu: (i, 0)),
                    pl.BlockSpec((tm, 1), lambda i, te, nu: (i, 0)),
                    pl.BlockSpec((None, d, de), lambda i, te, nu: (te[i], 0, 0), pipeline_mode=single),
                    pl.BlockSpec((None, d, de), lambda i, te, nu: (te[i], 0, 0), pipeline_mode=single),
                    pl.BlockSpec((None, de, d), lambda i, te, nu: (te[i], 0, 0), pipeline_mode=single)]
        aliases = {}
        if ys is not None:
            aliases = {len(operands): 0}
            operands.append(ys)
            in_specs.append(pl.BlockSpec(memory_space=pl.ANY))
        ys = pl.pallas_call(
            _expert_kernel,
            out_shape=jax.ShapeDtypeStruct((n_pad, d), F32),
            grid_spec=pltpu.PrefetchScalarGridSpec(
                num_scalar_prefetch=2,
                grid=(ct,),
                in_specs=in_specs,
                out_specs=pl.BlockSpec((tm, d), lambda i, te, nu: (i + off, 0)),
                scratch_shapes=[pltpu.VMEM((d, de), BF16), pltpu.VMEM((d, de), BF16), pltpu.VMEM((de, d), BF16)]),
            input_output_aliases=aliases,
            compiler_params=_cparams("arbitrary"),
            name="moe_experts",
        )(*operands)
    return ys, pos


def _final_kernel(x_ref, gt_ref, g_ref, *rest, n_rows):
    y_refs, o_ref = rest[:n_rows], rest[-1]
    moe = y_refs[0][...]
    for r in y_refs[1:]:
        moe = moe + r[...]
    o_ref[...] = _rms(x_ref[...] + gt_ref[...] * moe, g_ref[...])


def _final(x1, gt2, g_final, ys, pos, seq_len, n_chunks):
    t, d = x1.shape
    tm = _tile(t, 256)
    ct = t // tm // n_chunks
    assert ct * n_chunks * tm == t
    y = None
    for c in range(n_chunks):
        off = c * ct
        gt_op, gt_spec = _mod_operand(gt2, seq_len, tm, off)
        rows = [ys[pos[off * tm:(off + ct) * tm, k]] for k in range(pos.shape[1])]
        local = pl.BlockSpec((tm, d), lambda i: (i, 0))
        shifted = pl.BlockSpec((tm, d), lambda i: (i + off, 0))
        operands = [x1, gt_op, g_final.reshape(1, d)] + rows
        in_specs = [shifted, gt_spec, pl.BlockSpec((1, d), lambda i: (0, 0))] + [local] * len(rows)
        aliases = {}
        if y is not None:
            aliases = {len(operands): 0}
            operands.append(y)
            in_specs.append(pl.BlockSpec(memory_space=pl.ANY))
        y = pl.pallas_call(
            functools.partial(_final_kernel, n_rows=len(rows)),
            out_shape=jax.ShapeDtypeStruct((t, d), F32),
            grid=(ct,),
            in_specs=in_specs,
            out_specs=shifted,
            input_output_aliases=aliases,
            compiler_params=_cparams("arbitrary"),
            name="moe_combine_norm",
        )(*operands)
    return y


S5_CHUNK = 16
S5_SEGMENTS = 4
TOP_K = 2


def _mixers(x, mod, cache, wts, s5_ops, t_all, row_off, shared):
    (g_mix, w4, wf, bfp, d_skip, w_glu, b_glu, g_att, g_ssm, wa, ws, g_ffn, wr,
     n_heads, hd, n_groups_ssm) = wts
    bsz, seq, d = x.shape
    da = n_heads * hd
    t = bsz * seq
    xt = x.reshape(t, d)
    sh1, sc1, gt1, sh2, sc2, gt2 = jnp.split(mod, 6, axis=-1)
    qb, kf, vf, kb, vb, u, lfp = _inproj(xt, g_mix, sh1, sc1, w4, wf, bfp, seq, hd ** -0.5 * LOG2E)
    logf = lfp[:, :n_heads].reshape(bsz, seq, n_heads)
    if cache is None:
        fcum = _cumsum_rows(jnp.swapaxes(logf, 1, 2).reshape(bsz * n_heads, seq))[:, :seq]
        att = _fox_prompt(qb, kb, vb, (fcum * LOG2E).reshape(bsz, n_heads, seq), bsz, seq, n_heads, hd)
        h0 = jnp.zeros((bsz, n_groups_ssm, s5_ops[3].shape[2] // S5_IN_GROUPS), F32)
        ssm_y, h_re, h_im = _s5(u, h0, h0, s5_ops, bsz, S5_CHUNK, S5_SEGMENTS)
    else:
        cache_k, cache_v, cache_logf, st_re, st_im = cache
        past = cache_k.shape[1]
        lf_all = jnp.concatenate([cache_logf.astype(F32), logf], axis=1)
        f_all = _cumsum_rows(jnp.swapaxes(lf_all, 1, 2).reshape(bsz * n_heads, past + seq))
        f_all = (f_all[:, :past + seq] * LOG2E).reshape(bsz, n_heads, past + seq)
        att = _fox_sample(qb.reshape(bsz, seq, n_heads, hd), kf.reshape(bsz, seq, n_heads, hd),
                          vf.reshape(bsz, seq, n_heads, hd), f_all, cache_k, cache_v)
        att = att.reshape(t, da)
        ssm_y, h_re, h_im = _s5(u, st_re.astype(F32), st_im.astype(F32), s5_ops, bsz, seq, 1)
    ssm_n = _glu(ssm_y, u, d_skip, w_glu, b_glu, g_ssm)
    x1, shared = _outproj(att, ssm_n, xt, wa, ws, g_att, gt1, g_ffn, sh2, sc2, wr, seq, t_all, row_off, shared)
    new_cache = (kf.reshape(bsz, seq, n_heads, hd), vf.reshape(bsz, seq, n_heads, hd), logf, h_re, h_im)
    return x1, gt2, shared, new_cache


def kernel(x_prompt, x_sample, cache_k, cache_v, cache_logf, state_ssm_re, state_ssm_im, c_prompt, c_sample, w_ada, b_ada, g_mix, w_in, b_f, lam_re, lam_im, log_dt, b_re, b_im, c_re, c_im, d_skip, w_glu, b_glu, g_att, g_ssm, w_out, g_ffn, w_rg, w_re, w1, w3, w2, g_final):
    depth = w_ada.shape[0]
    assert depth == 1, "the residual stream of a deeper stack would have to be threaded through the layers"
    n_heads, hd = cache_k.shape[3], cache_k.shape[4]
    da = n_heads * hd
    d = x_prompt.shape[-1]
    ds = d - da
    assert da == ds
    n_groups_ssm = state_ssm_re.shape[2]
    n_expert_groups = w_rg.shape[-1]
    n_experts = w_re.shape[-1]
    assert n_expert_groups + n_experts <= LANES and n_heads <= LANES
    l = 0
    bp = x_prompt.shape[0]
    mod = _ada(jnp.concatenate([c_prompt, c_sample], axis=0).astype(F32), w_ada[l], b_ada[l])
    wi = w_in[l]
    w4 = jnp.concatenate([wi[:, :3 * da], wi[:, 3 * da + n_heads:]], axis=1).astype(BF16)
    wf = jnp.pad(wi[:, 3 * da:3 * da + n_heads], ((0, 0), (0, LANES - n_heads))).astype(BF16)
    bfp = jnp.pad(b_f[l], (0, LANES - n_heads)).reshape(1, LANES).astype(F32)
    wr = jnp.pad(jnp.concatenate([w_rg[l], w_re[l]], axis=1),
                 ((0, 0), (0, LANES - n_expert_groups - n_experts))).astype(F32)
    wr_hi = wr.astype(BF16)
    wr_lo = (wr - wr_hi.astype(F32)).astype(BF16)
    wr2 = jnp.concatenate([wr_hi, wr_lo], axis=1)
    wo = _to_bf16(w_out[l])
    wts = (g_mix[l], w4, wf, bfp, d_skip[l], _to_bf16(w_glu[l]), b_glu[l], g_att[l], g_ssm[l],
           wo[:da], wo[da:], g_ffn[l], wr2, n_heads, hd, n_groups_ssm)
    s5_args = (lam_re[l].astype(F32), lam_im[l].astype(F32), log_dt[l], b_re[l].astype(F32), b_im[l].astype(F32),
               c_re[l].astype(F32), c_im[l].astype(F32))
    seq_p = x_prompt.shape[1]
    n_steps_p = seq_p // (S5_SEGMENTS * S5_CHUNK)
    if x_sample.shape[1] == S5_CHUNK:
        ops_p, ops_s = _s5_operators(*s5_args, S5_CHUNK, (n_steps_p, 1))
    else:
        (ops_p,) = _s5_operators(*s5_args, S5_CHUNK, (n_steps_p,))
        (ops_s,) = _s5_operators(*s5_args, x_sample.shape[1], (1,))
    bs, seq_s = x_sample.shape[:2]
    t_p, t_s = bp * seq_p, bs * seq_s
    x1p, gt2p, shared, (kp, vp, lfp, rep, imp) = _mixers(
        x_prompt.astype(F32), mod[:bp], None, wts, ops_p, t_p + t_s, 0, None)
    cache = (cache_k[l], cache_v[l], cache_logf[l], state_ssm_re[l], state_ssm_im[l])
    x1s, gt2s, (h2, logits), (ksm, vsm, lfs, res, ims) = _mixers(
        x_sample.astype(F32), mod[bp:], cache, wts, ops_s, t_p + t_s, t_p, shared)
    ys, pos = _experts(h2, logits, w1[l], w3[l], w2[l], n_expert_groups, TOP_K)
    yp = _final(x1p, gt2p, g_final, ys, pos[:t_p], seq_p, MOE_CHUNKS).reshape(x_prompt.shape)
    ysm = _final(x1s, gt2s, g_final, ys, pos[t_p:], seq_s, 1).reshape(x_sample.shape)
    return (yp, ysm, kp[None], vp[None], lfp[None], rep[None], imp[None],
            ksm[None], vsm[None], lfs[None], res[None], ims[None])
```

```python
import functools
import math

import jax
import jax.numpy as jnp
from jax import lax
from jax.experimental import pallas as pl
from jax.experimental.pallas import tpu as pltpu

F32 = jnp.float32
BF16 = jnp.bfloat16
EPS = 1e-6
NEG = -1e30
LOG2E = math.log2(math.e)
LANES = 128
SUBLANES = 8
VMEM_LIMIT = 56 * 1024 * 1024
HIGHEST = lax.Precision.HIGHEST
NT_DIMS = (((1,), (1,)), ((), ()))


def _cparams(*sem):
    return pltpu.CompilerParams(dimension_semantics=sem, vmem_limit_bytes=VMEM_LIMIT)


def _tile(n, pref):
    t = min(n, pref)
    assert n % t == 0, (n, pref)
    return t


def _rms(x, g):
    return x * lax.rsqrt(jnp.mean(x * x, axis=-1, keepdims=True) + EPS) * g


def _mod_operand(vec, seq_len, tm, off=0):
    n_seq, d = vec.shape
    if seq_len % tm == 0:
        per = seq_len // tm
        return vec[:, None, :], pl.BlockSpec((None, 1, d), lambda i, *_: ((i + off) // per, 0, 0))
    assert tm % seq_len == 0
    rows = jnp.repeat(vec, seq_len, axis=0).reshape(-1, tm, d)
    return rows, pl.BlockSpec((None, tm, d), lambda i, *_: (i + off, 0, 0))


def _cast_kernel(x_ref, o_ref):
    o_ref[...] = x_ref[...].astype(BF16)


def _to_bf16(w):
    n = w.shape[-1]
    w2 = w.reshape(-1, n)
    rows = w2.shape[0]
    tr = _tile(rows, max(SUBLANES, (1 << 20) // n))
    out = pl.pallas_call(
        _cast_kernel,
        out_shape=jax.ShapeDtypeStruct((rows, n), BF16),
        grid=(rows // tr,),
        in_specs=[pl.BlockSpec((tr, n), lambda i: (i, 0))],
        out_specs=pl.BlockSpec((tr, n), lambda i: (i, 0)),
        compiler_params=_cparams("arbitrary"),
        name="cast_bf16",
    )(w2)
    return out.reshape(w.shape)


def _ada_kernel(c_ref, w_ref, b_ref, o_ref):
    c = c_ref[...]
    a = (c * jax.nn.sigmoid(c)).astype(BF16)
    o_ref[...] = jnp.dot(a, w_ref[...].astype(BF16), preferred_element_type=F32) + b_ref[...]


def _ada(c, w, b):
    s, d = c.shape
    n = w.shape[1]
    tn = _tile(n, 1024)
    return pl.pallas_call(
        _ada_kernel,
        out_shape=jax.ShapeDtypeStruct((s, n), F32),
        grid=(n // tn,),
        in_specs=[pl.BlockSpec((s, d), lambda j: (0, 0)),
                  pl.BlockSpec((d, tn), lambda j: (0, j)),
                  pl.BlockSpec((1, tn), lambda j: (0, j))],
        out_specs=pl.BlockSpec((s, tn), lambda j: (0, j)),
        compiler_params=_cparams("arbitrary"),
        name="ada_mod",
    )(c, w, b.reshape(1, n))


def _inproj_kernel(x_ref, g_ref, sh_ref, sc_ref, w_ref, wf_ref, bf_ref,
                   q_ref, kf_ref, vf_ref, kb_ref, vb_ref, u_ref, lf_ref, h_scr, *, qscale):
    j = pl.program_id(1)

    @pl.when(j == 0)
    def _():
        h = _rms(x_ref[...], g_ref[...]) * (1.0 + sc_ref[...]) + sh_ref[...]
        hb = h.astype(BF16)
        h_scr[...] = hb
        fg = jnp.dot(hb, wf_ref[...], preferred_element_type=F32) + bf_ref[...]
        lf_ref[...] = jnp.minimum(fg, 0.0) - jnp.log1p(jnp.exp(-jnp.abs(fg)))

    p = jnp.dot(h_scr[...], w_ref[...], preferred_element_type=F32)

    @pl.when(j == 0)
    def _():
        q_ref[...] = (p * qscale).astype(BF16)

    @pl.when(j == 1)
    def _():
        kf_ref[...] = p
        kb_ref[...] = p.astype(BF16)

    @pl.when(j == 2)
    def _():
        vf_ref[...] = p
        vb_ref[...] = p.astype(BF16)

    @pl.when(j == 3)
    def _():
        for k in range(u_ref.shape[0]):
            u_ref[k] = p[:, k * LANES:(k + 1) * LANES]


def _inproj(x, g, sh, sc, w4, wf, bfp, seq_len, qscale):
    t, d = x.shape
    da = w4.shape[1] // 4
    tm = _tile(t, 512)
    sh_op, sh_spec = _mod_operand(sh, seq_len, tm)
    sc_op, sc_spec = _mod_operand(sc, seq_len, tm)
    row = lambda i, j: (i, 0)
    outs = pl.pallas_call(
        functools.partial(_inproj_kernel, qscale=qscale),
        out_shape=(jax.ShapeDtypeStruct((t, da), BF16),
                   jax.ShapeDtypeStruct((t, da), F32), jax.ShapeDtypeStruct((t, da), F32),
                   jax.ShapeDtypeStruct((t, da), BF16), jax.ShapeDtypeStruct((t, da), BF16),
                   jax.ShapeDtypeStruct((da // LANES, t, LANES), F32),
                   jax.ShapeDtypeStruct((t, LANES), F32)),
        grid=(t // tm, 4),
        in_specs=[pl.BlockSpec((tm, d), row),
                  pl.BlockSpec((1, d), lambda i, j: (0, 0)),
                  sh_spec, sc_spec,
                  pl.BlockSpec((d, da), lambda i, j: (0, j)),
                  pl.BlockSpec((d, LANES), lambda i, j: (0, 0)),
                  pl.BlockSpec((1, LANES), lambda i, j: (0, 0))],
        out_specs=(pl.BlockSpec((tm, da), row),) * 5
                  + (pl.BlockSpec((da // LANES, tm, LANES), lambda i, j: (0, i, 0)), pl.BlockSpec((tm, LANES), row)),
        scratch_shapes=[pltpu.VMEM((tm, d), BF16)],
        compiler_params=_cparams("arbitrary", "arbitrary"),
        name="in_proj",
    )(x, g.reshape(1, d), sh_op, sc_op, w4, wf, bfp)
    return outs


def _cumsum_kernel(x_ref, o_ref):
    sb, nb, _ = x_ref.shape
    li = lax.broadcasted_iota(jnp.int32, (LANES, LANES), 0)
    lj = lax.broadcasted_iota(jnp.int32, (LANES, LANES), 1)
    upper = (li <= lj).astype(F32)
    ri = lax.broadcasted_iota(jnp.int32, (nb, nb), 0)
    rj = lax.broadcasted_iota(jnp.int32, (nb, nb), 1)
    strict = (rj < ri).astype(F32)
    for s in range(sb):
        within = jnp.dot(x_ref[s], upper, precision=HIGHEST, preferred_element_type=F32)
        tot = jnp.broadcast_to(within[:, LANES - 1:LANES], (nb, LANES))
        off = jnp.dot(strict, tot, precision=HIGHEST, preferred_element_type=F32)
        o_ref[s] = within + off


def _cumsum_rows(x):
    n_rows, n = x.shape
    nb = -(-n // (LANES * SUBLANES)) * SUBLANES
    xp = jnp.pad(x, ((0, 0), (0, nb * LANES - n))).reshape(n_rows, nb, LANES)
    sb = _tile(n_rows, 16)
    out = pl.pallas_call(
        _cumsum_kernel,
        out_shape=jax.ShapeDtypeStruct((n_rows, nb, LANES), F32),
        grid=(n_rows // sb,),
        in_specs=[pl.BlockSpec((sb, nb, LANES), lambda i: (i, 0, 0))],
        out_specs=pl.BlockSpec((sb, nb, LANES), lambda i: (i, 0, 0)),
        compiler_params=_cparams("arbitrary"),
        name="logf_cumsum",
    )(xp)
    return out.reshape(n_rows, nb * LANES)


FOX_TQ = 2048
FOX_TK = 2048
FOX_SUB = 1024


def _col_from_row(row):
    n = row.shape[1]
    eye = lax.broadcasted_iota(jnp.int32, (n, n), 0) == lax.broadcasted_iota(jnp.int32, (n, n), 1)
    return jnp.sum(jnp.where(eye, jnp.broadcast_to(row, (n, n)), 0.0), axis=1, keepdims=True)


def _lane_tiles(x):
    return [x[:, j * LANES:(j + 1) * LANES] for j in range(x.shape[1] // LANES)]


def _fox_kernel(q_ref, k_ref, v_ref, f_ref, o_ref, m_scr, l_scr, acc_scr, *, tq, tk, sub):
    qi = pl.program_id(2)
    nsub = tq // sub
    fqb = [jnp.broadcast_to(_col_from_row(f_ref[qi * nsub + a]), (sub, LANES)) for a in range(nsub)]
    m_scr[...] = jnp.full(m_scr.shape, NEG, F32)
    l_scr[...] = jnp.zeros(l_scr.shape, F32)
    acc_scr[...] = jnp.zeros(acc_scr.shape, F32)

    def chain(a, k, v, fk, diagonal):
        rows = pl.ds(a * sub, sub)
        t1 = lax.dot_general(q_ref[rows, :], k, NT_DIMS, preferred_element_type=F32) - fk
        if diagonal:
            row = lax.broadcasted_iota(jnp.int32, t1.shape, 0)
            col = lax.broadcasted_iota(jnp.int32, t1.shape, 1)
            t1 = jnp.where(col <= row, t1, NEG)
        tiles = _lane_tiles(t1)
        part = functools.reduce(jnp.maximum, tiles)
        m_prev = m_scr[rows, :]
        m_new = jnp.maximum(m_prev, jnp.max(part, axis=-1, keepdims=True) + fqb[a])
        c = m_new - fqb[a]
        p = [jnp.exp2(t - c) for t in tiles]
        alpha = jnp.exp2(m_prev - m_new)
        l_scr[rows, :] = alpha * l_scr[rows, :] + functools.reduce(jnp.add, p)
        pv = jnp.dot(jnp.concatenate(p, axis=1).astype(BF16), v, preferred_element_type=F32)
        acc_scr[rows, :] = alpha * acc_scr[rows, :] + pv
        m_scr[rows, :] = m_new

    def full_step(kt, carry):
        ks = pl.multiple_of(kt * tk, tk)
        k = k_ref[pl.ds(ks, tk), :]
        v = v_ref[pl.ds(ks, tk), :]
        fk = jnp.concatenate([f_ref[kt * (tk // sub) + j] for j in range(tk // sub)], axis=1)
        for a in range(nsub):
            chain(a, k, v, fk, False)
        return carry

    lax.fori_loop(0, qi * (tq // tk), full_step, 0)
    for a in range(nsub):
        for j in range(a + 1):
            ks = pl.multiple_of((qi * nsub + j) * sub, sub)
            chain(a, k_ref[pl.ds(ks, sub), :], v_ref[pl.ds(ks, sub), :], f_ref[qi * nsub + j], j == a)
    o_ref[...] = acc_scr[...] / jnp.sum(l_scr[...], axis=-1, keepdims=True)


def _fox_prompt(qb, kb, vb, fcum2, bsz, seq, n_heads, hd):
    assert hd == LANES
    tq = _tile(seq, FOX_TQ)
    tk = _tile(tq, FOX_TK)
    sub = _tile(tk, FOX_SUB)
    nq = seq // tq
    f = fcum2.reshape(bsz, n_heads, seq // sub, 1, sub)
    return pl.pallas_call(
        functools.partial(_fox_kernel, tq=tq, tk=tk, sub=sub),
        out_shape=jax.ShapeDtypeStruct((bsz * seq, n_heads * hd), F32),
        grid=(bsz, n_heads, nq),
        in_specs=[pl.BlockSpec((tq, hd), lambda b, h, i: (b * nq + i, h)),
                  pl.BlockSpec((seq, hd), lambda b, h, i: (b, h)),
                  pl.BlockSpec((seq, hd), lambda b, h, i: (b, h)),
                  pl.BlockSpec((None, None, seq // sub, 1, sub), lambda b, h, i: (b, h, 0, 0, 0))],
        out_specs=pl.BlockSpec((tq, hd), lambda b, h, i: (b * nq + i, h)),
        scratch_shapes=[pltpu.VMEM((tq, LANES), F32), pltpu.VMEM((tq, LANES), F32), pltpu.VMEM((tq, hd), F32)],
        compiler_params=_cparams("arbitrary", "arbitrary", "arbitrary"),
        name="fox_prompt",
    )(qb, kb, vb, f)


def _fox_sample_kernel(q_ref, ck_ref, cv_ref, kn_ref, vn_ref, fq_ref, fkc_ref, fkn_ref,
                       rh_ref, ri_ref, lh_ref, lhn_ref, kin_ref, o_ref, m_scr, l_scr, acc_scr):
    kt = pl.program_id(1)

    @pl.when(kt == 0)
    def _():
        m_scr[...] = jnp.full(m_scr.shape, NEG, F32)
        l_scr[...] = jnp.zeros(l_scr.shape, F32)
        acc_scr[...] = jnp.zeros(acc_scr.shape, F32)

    q = q_ref[...]
    fq = fq_ref[...]
    n_heads = ck_ref.shape[1]
    s_new = q.shape[0] // n_heads

    def update(k4, v4, t1_of):
        n, h, d = k4.shape
        k2 = k4.reshape(n * h, d).astype(BF16)
        v2 = v4.reshape(n * h, d).astype(BF16)
        t1 = t1_of(lax.dot_general(q, k2, NT_DIMS, preferred_element_type=F32))
        m_prev = m_scr[...]
        m_new = jnp.maximum(m_prev, jnp.max(t1, axis=-1, keepdims=True) + fq)
        alpha = jnp.exp2(m_prev - m_new)
        p = jnp.exp2(t1 - (m_new - fq))
        l_scr[...] = alpha * l_scr[...] + jnp.sum(p, axis=-1, keepdims=True)
        acc_scr[...] = alpha * acc_scr[...] + jnp.dot(p.astype(BF16), v2, preferred_element_type=F32)
        m_scr[...] = m_new

    def cache_t1(s):
        head = lax.broadcasted_iota(jnp.int32, (n_heads, 1), 0)
        fkm = jnp.where(lh_ref[...] == head, fkc_ref[...], -NEG)
        return jnp.concatenate([s[h * s_new:(h + 1) * s_new] - fkm[h:h + 1] for h in range(n_heads)], axis=0)

    update(ck_ref[...], cv_ref[...], cache_t1)

    @pl.when(kt == pl.num_programs(1) - 1)
    def _():
        causal_head = jnp.where(kin_ref[...] <= ri_ref[...], lhn_ref[...], -1)
        update(kn_ref[...], vn_ref[...],
               lambda s: jnp.where(rh_ref[...] == causal_head, s - fkn_ref[...], NEG))
        o_ref[...] = acc_scr[...] / l_scr[...]


def _fox_sample(qb, k_new, v_new, f_all2, cache_k, cache_v):
    bsz, s_new, n_heads, hd = qb.shape
    past = cache_k.shape[1]
    tk = _tile(past, 1024)
    n_rows = n_heads * s_new
    assert n_rows % 16 == 0 and s_new % SUBLANES == 0
    q2 = jnp.swapaxes(qb, 1, 2).reshape(bsz, n_rows, hd)
    fq = f_all2[:, :, past:].reshape(bsz, n_rows, 1)
    fkc = jnp.swapaxes(f_all2[:, :, :past], 1, 2).reshape(bsz, 1, past * n_heads)
    fkn = jnp.swapaxes(f_all2[:, :, past:], 1, 2).reshape(bsz, 1, s_new * n_heads)
    r = jnp.arange(n_rows, dtype=jnp.int32).reshape(n_rows, 1)
    lane = lambda n: jnp.arange(n * n_heads, dtype=jnp.int32).reshape(1, n * n_heads)
    const = lambda a: pl.BlockSpec(a.shape, lambda b, j: (0,) * a.ndim)
    consts = (r // s_new, r % s_new, lane(tk) % n_heads, lane(s_new) % n_heads, lane(s_new) // n_heads)
    out = pl.pallas_call(
        _fox_sample_kernel,
        out_shape=jax.ShapeDtypeStruct((bsz, n_rows, hd), F32),
        grid=(bsz, past // tk),
        in_specs=[pl.BlockSpec((None, n_rows, hd), lambda b, j: (b, 0, 0)),
                  pl.BlockSpec((None, tk, n_heads, hd), lambda b, j: (b, j, 0, 0)),
                  pl.BlockSpec((None, tk, n_heads, hd), lambda b, j: (b, j, 0, 0)),
                  pl.BlockSpec((None, s_new, n_heads, hd), lambda b, j: (b, 0, 0, 0)),
                  pl.BlockSpec((None, s_new, n_heads, hd), lambda b, j: (b, 0, 0, 0)),
                  pl.BlockSpec((None, n_rows, 1), lambda b, j: (b, 0, 0)),
                  pl.BlockSpec((None, 1, tk * n_heads), lambda b, j: (b, 0, j)),
                  pl.BlockSpec((None, 1, s_new * n_heads), lambda b, j: (b, 0, 0))]
                 + [const(a) for a in consts],
        out_specs=pl.BlockSpec((None, n_rows, hd), lambda b, j: (b, 0, 0)),
        scratch_shapes=[pltpu.VMEM((n_rows, 1), F32), pltpu.VMEM((n_rows, 1), F32), pltpu.VMEM((n_rows, hd), F32)],
        compiler_params=_cparams("arbitrary", "arbitrary"),
        name="fox_sample",
    )(q2, cache_k, cache_v, k_new, v_new, fq, fkc, fkn, *consts)
    return jnp.swapaxes(out.reshape(bsz, n_heads, s_new, hd), 1, 2)


S5_IN_GROUPS = 8
S5_OUT_GROUPS = 16


def _s5_operators(lam_re, lam_im, log_dt, b_re, b_im, c_re, c_im, lc, n_steps):
    g, p = lam_re.shape
    hc = b_re.shape[2]
    gi, go = S5_IN_GROUPS, S5_OUT_GROUPS
    dt = jnp.exp(log_dt.astype(F32))[:, None]

    def power(k):
        mag = jnp.exp(lam_re * dt * k)
        return mag * jnp.cos(lam_im * dt * k), mag * jnp.sin(lam_im * dt * k)

    lbr, lbi = power(1.0)
    den = lam_re * lam_re + lam_im * lam_im
    fr = ((lbr - 1.0) * lam_re + lbi * lam_im) / den
    fi = (lbi * lam_re - (lbr - 1.0) * lam_im) / den
    bbr = fr[:, :, None] * b_re - fi[:, :, None] * b_im
    bbi = fr[:, :, None] * b_im + fi[:, :, None] * b_re
    ks = jnp.arange(lc + 1, dtype=F32)[None, :, None]
    mag = jnp.exp(lam_re[:, None, :] * dt[:, None, :] * ks)
    ang = lam_im[:, None, :] * dt[:, None, :] * ks
    pwr, pwi = mag * jnp.cos(ang), mag * jnp.sin(ang)
    cr, ci = jnp.swapaxes(c_re, 1, 2), jnp.swapaxes(c_im, 1, 2)
    d_r = bbr[:, :, :, None] * cr[:, :, None, :] - bbi[:, :, :, None] * ci[:, :, None, :]
    d_i = bbr[:, :, :, None] * ci[:, :, None, :] + bbi[:, :, :, None] * cr[:, :, None, :]
    kern = (jnp.einsum("gtp,gpab->gtab", pwr[:, :lc], d_r, precision=HIGHEST)
            - jnp.einsum("gtp,gpab->gtab", pwi[:, :lc], d_i, precision=HIGHEST))
    def block_diag(vals, row_block, n_blocks):
        w = vals.shape[-1]
        tiled = jnp.tile(vals, (1,) * (vals.ndim - 1) + (n_blocks,))
        col_block = jnp.arange(n_blocks * w) // w
        return jnp.where(row_block[:, None] == col_block[None, :], tiled, 0.0).astype(BF16)

    per_slab = lambda x, n: jnp.swapaxes(x.reshape((g // n, n) + x.shape[1:]), 1, 2)
    bd = block_diag(per_slab(kern, go).reshape(g // go, lc, go * hc, hc), jnp.arange(go * hc) // hc, go)
    rev_r, rev_i = pwr[:, lc - 1::-1][:, :lc], pwi[:, lc - 1::-1][:, :lc]
    bt_r, bt_i = jnp.swapaxes(bbr, 1, 2)[:, None], jnp.swapaxes(bbi, 1, 2)[:, None]
    w_r = rev_r[:, :, None, :] * bt_r - rev_i[:, :, None, :] * bt_i
    w_i = rev_r[:, :, None, :] * bt_i + rev_i[:, :, None, :] * bt_r
    place = lambda w: block_diag(per_slab(w, gi).reshape(g // gi, lc, gi * hc, p), jnp.arange(gi * hc) // hc, gi)
    bw = jnp.concatenate([place(w_r), place(w_i)], axis=-1)
    nr, ni = jnp.swapaxes(pwr[:, 1:], 1, 2), jnp.swapaxes(pwi[:, 1:], 1, 2)
    v_r = cr[:, :, None, :] * nr[:, :, :, None] - ci[:, :, None, :] * ni[:, :, :, None]
    v_i = cr[:, :, None, :] * ni[:, :, :, None] + ci[:, :, None, :] * nr[:, :, :, None]
    halves = go // gi
    split = lambda v: jnp.transpose(v.reshape(g // go, halves, gi, p, lc, hc), (0, 4, 1, 2, 3, 5))
    bv = jnp.stack([split(v_r), split(-v_i)], axis=3)
    r = jnp.arange(halves * 2 * gi * p)
    row_group = (r // (2 * gi * p)) * gi + (r // p) % gi
    bv = block_diag(bv.reshape(g // go, lc, halves * 2 * gi * p, hc), row_group, go)

    def coef(n_steps):
        jr, ji = power(float(lc * n_steps))
        c = jnp.stack([pwr[:, lc], pwi[:, lc], jr, ji], axis=1)
        c = jnp.swapaxes(c.reshape(g // gi, gi, 4, p), 1, 2).reshape(g // gi, 4, gi * p)
        return jnp.pad(c, ((0, 0), (0, SUBLANES - 4), (0, 0)))

    return [(bd, bw, bv, coef(n)) for n in n_steps]


def _s5_state_kernel(u_ref, bw_ref, coef_ref, h0_ref, hin_ref, hl_ref, s_scr, *, lc, n_q, n_seg):
    i = pl.program_id(1)
    tr = u_ref.shape[0] // lc
    n_tiles = s_scr.shape[0]
    nh = n_tiles // 2
    half = nh * LANES
    n_steps = s_scr.shape[1] // n_q
    acc = None
    for s in range(lc):
        a = u_ref[pl.ds(s, tr, stride=lc), :].astype(BF16)
        d = jnp.dot(a, bw_ref[s], preferred_element_type=F32)
        acc = d if acc is None else acc + d
    if n_steps == 1:
        rows = pl.ds(pl.multiple_of(i * tr, tr), tr)
    else:
        assert n_steps % tr == 0
        per = n_steps // tr
        rows = pl.ds(lax.rem(i, per) * (tr * n_q) + lax.div(i, per), tr, stride=n_q)
    for m in range(n_tiles):
        s_scr[m, rows, :] = acc[:, m * LANES:(m + 1) * LANES]

    @pl.when(i == pl.num_programs(1) - 1)
    def _():
        bc = lambda r: jnp.broadcast_to(coef_ref[r:r + 1, :], (n_q, half))
        ar, ai = bc(0), bc(1)
        jr, ji = coef_ref[2:3, :], coef_ref[3:4, :]
        step_rows = lambda j: pl.ds(pl.multiple_of(j * n_q, n_q), n_q)

        def load_rows(j):
            return jnp.concatenate([s_scr[m, step_rows(j), :] for m in range(n_tiles)], axis=1)

        def store_rows(j, x):
            for m in range(n_tiles):
                s_scr[m, step_rows(j), :] = x[:, m * LANES:(m + 1) * LANES]

        def scan_zero(j, carry):
            xr, xi = carry
            s = load_rows(j)
            store_rows(j, jnp.concatenate([xr, xi], axis=1))
            return ar * xr - ai * xi + s[:, :half], ar * xi + ai * xr + s[:, half:]

        zero = jnp.zeros((n_q, half), F32)
        xr_end, xi_end = lax.fori_loop(0, n_steps, scan_zero, (zero, zero))

        h0 = h0_ref[...]
        if n_seg == 1:
            er, ei = h0[:, :half], h0[:, half:]
            hl_ref[...] = jnp.concatenate([jr * er - ji * ei + xr_end, jr * ei + ji * er + xi_end], axis=1)
        else:
            er_rows, ei_rows = [], []
            for b in range(n_q // n_seg):
                r_, i_ = h0[b:b + 1, :half], h0[b:b + 1, half:]
                for sg in range(n_seg):
                    q = b * n_seg + sg
                    er_rows.append(r_)
                    ei_rows.append(i_)
                    r_, i_ = (jr * r_ - ji * i_ + xr_end[q:q + 1], jr * i_ + ji * r_ + xi_end[q:q + 1])
                hl_ref[b:b + 1, :] = jnp.concatenate([r_, i_], axis=1)
            er = jnp.concatenate(er_rows, axis=0)
            ei = jnp.concatenate(ei_rows, axis=0)

        def scan_fix(j, carry):
            fr, fi = carry
            store_rows(j, load_rows(j) + jnp.concatenate([fr, fi], axis=1))
            return ar * fr - ai * fi, ar * fi + ai * fr

        lax.fori_loop(0, n_steps, scan_fix, (er, ei))
        for m in range(n_tiles):
            if n_steps == 1:
                hin_ref[:, m * LANES:(m + 1) * LANES] = s_scr[m].astype(BF16)
            else:
                for q in range(n_q):
                    hin_ref[q * n_steps:(q + 1) * n_steps, m * LANES:(m + 1) * LANES] = (
                        s_scr[m, pl.ds(q, n_steps, stride=n_q), :].astype(BF16))


def _s5_out_kernel(u_ref, hin_ref, bd_ref, bv_ref, y_ref, a_scr, acc_scr, *, lc):
    tc = hin_ref.shape[0]
    n_blk = u_ref.shape[0]
    for s in range(lc):
        for h in range(n_blk):
            a_scr[s * tc:(s + 1) * tc, h * LANES:(h + 1) * LANES] = (
                u_ref[h, pl.ds(s, tc, stride=lc), :].astype(BF16))
    acc_scr[...] = jnp.dot(a_scr[...], bd_ref[0], preferred_element_type=F32)
    for tau in range(1, lc):
        n = (lc - tau) * tc
        acc_scr[tau * tc:, :] += jnp.dot(a_scr[:n, :], bd_ref[tau], preferred_element_type=F32)
    hin = hin_ref[...]
    for t in range(lc):
        y_t = acc_scr[t * tc:(t + 1) * tc, :] + jnp.dot(hin, bv_ref[t], preferred_element_type=F32)
        for h in range(n_blk):
            y_ref[h, pl.ds(t, tc, stride=lc), :] = y_t[:, h * LANES:(h + 1) * LANES]


def _s5(u3, h0_re, h0_im, ops, bsz, lc, n_seg):
    bd, bw, bv, coef = ops
    t = u3.shape[1]
    ds = u3.shape[0] * LANES
    n_groups, p = h0_re.shape[1], h0_re.shape[2]
    gi, go = S5_IN_GROUPS, S5_OUT_GROUPS
    wi, wo = ds // (n_groups // gi), ds // (n_groups // go)
    ws = gi * 2 * p
    nc = t // lc
    n_q = bsz * n_seg
    assert nc % n_q == 0 and n_q % SUBLANES == 0 and wi == LANES
    tr = _tile(nc if nc == n_q else nc // n_q, 256)
    pack = lambda h: h.reshape(bsz, n_groups // gi, gi * p)
    h0 = jnp.swapaxes(jnp.concatenate([pack(h0_re), pack(h0_im)], axis=-1), 0, 1)
    hin, hl = pl.pallas_call(
        functools.partial(_s5_state_kernel, lc=lc, n_q=n_q, n_seg=n_seg),
        out_shape=(jax.ShapeDtypeStruct((nc, (n_groups // gi) * ws), BF16),
                   jax.ShapeDtypeStruct((n_groups // gi, bsz, ws), F32)),
        grid=(n_groups // gi, nc // tr),
        in_specs=[pl.BlockSpec((None, tr * lc, wi), lambda k, i: (k, i, 0)),
                  pl.BlockSpec((None, lc, wi, ws), lambda k, i: (k, 0, 0, 0)),
                  pl.BlockSpec((None, SUBLANES, ws // 2), lambda k, i: (k, 0, 0)),
                  pl.BlockSpec((None, bsz, ws), lambda k, i: (k, 0, 0))],
        out_specs=(pl.BlockSpec((nc, ws), lambda k, i: (0, k)),
                   pl.BlockSpec((None, bsz, ws), lambda k, i: (k, 0, 0))),
        scratch_shapes=[pltpu.VMEM((ws // LANES, nc, LANES), F32)],
        compiler_params=_cparams("arbitrary", "arbitrary"),
        name="s5_state",
    )(u3, bw, coef, h0)
    tc = _tile(nc, 256)
    wso = (go // gi) * ws
    y = pl.pallas_call(
        functools.partial(_s5_out_kernel, lc=lc),
        out_shape=jax.ShapeDtypeStruct(u3.shape, F32),
        grid=(n_groups // go, nc // tc),
        in_specs=[pl.BlockSpec((wo // LANES, tc * lc, LANES), lambda s, i: (s, i, 0)),
                  pl.BlockSpec((tc, wso), lambda s, i: (i, s)),
                  pl.BlockSpec((None, lc, wo, wo), lambda s, i: (s, 0, 0, 0)),
                  pl.BlockSpec((None, lc, wso, wo), lambda s, i: (s, 0, 0, 0), pipeline_mode=pl.Buffered(1))],
        out_specs=pl.BlockSpec((wo // LANES, tc * lc, LANES), lambda s, i: (s, i, 0)),
        scratch_shapes=[pltpu.VMEM((lc * tc, wo), BF16), pltpu.VMEM((lc * tc, wo), F32)],
        compiler_params=_cparams("arbitrary", "arbitrary"),
        name="s5_out",
    )(u3, hin, bd, bv)
    hl = jnp.swapaxes(hl, 0, 1)
    unpack = lambda h: h.reshape(bsz, n_groups, p)
    return y, unpack(hl[:, :, :ws // 2]), unpack(hl[:, :, ws // 2:])


def _glu_kernel(y_ref, u_ref, d_ref, w_ref, b_ref, g_ref, o_ref):
    lanes = lambda r: jnp.concatenate([r[k] for k in range(r.shape[0])], axis=1)
    z = jax.nn.gelu(lanes(y_ref) + d_ref[...] * lanes(u_ref))
    gate = jax.nn.sigmoid(jnp.dot(z.astype(BF16), w_ref[...], preferred_element_type=F32) + b_ref[...])
    o_ref[...] = _rms(z * gate, g_ref[...]).astype(BF16)


def _glu(y, u, d_skip, w_glu, b_glu, g_ssm):
    nblk, t, _ = y.shape
    ds = nblk * LANES
    tm = _tile(t, 512)
    vec = pl.BlockSpec((1, ds), lambda i: (0, 0))
    blk = pl.BlockSpec((nblk, tm, LANES), lambda i: (0, i, 0))
    return pl.pallas_call(
        _glu_kernel,
        out_shape=jax.ShapeDtypeStruct((t, ds), BF16),
        grid=(t // tm,),
        in_specs=[blk, blk, vec, pl.BlockSpec((ds, ds), lambda i: (0, 0)), vec, vec],
        out_specs=pl.BlockSpec((tm, ds), lambda i: (i, 0)),
        compiler_params=_cparams("arbitrary"),
        name="s5_glu",
    )(y, u, d_skip.reshape(1, ds), w_glu, b_glu.reshape(1, ds), g_ssm.reshape(1, ds))


def _outproj_kernel(att_ref, ssm_ref, x_ref, wa_ref, ws_ref, ga_ref, gt_ref, gf_ref, sh_ref, sc_ref, wr_ref,
                    *rest):
    x1_ref, h2_ref, lg_ref = rest[-3:]
    a = _rms(att_ref[...], ga_ref[...]).astype(BF16)
    mixed = (jnp.dot(a, wa_ref[...], preferred_element_type=F32)
             + jnp.dot(ssm_ref[...], ws_ref[...], preferred_element_type=F32))
    x1 = x_ref[...] + gt_ref[...] * mixed
    x1_ref[...] = x1
    h2 = _rms(x1, gf_ref[...]) * (1.0 + sc_ref[...]) + sh_ref[...]
    hi = h2.astype(BF16)
    h2_ref[...] = hi
    lo = (h2 - hi.astype(F32)).astype(BF16)
    r = (jnp.dot(hi, wr_ref[...], preferred_element_type=F32)
         + jnp.dot(lo, wr_ref[...], preferred_element_type=F32))
    lg_ref[...] = r[:, :LANES] + r[:, LANES:]


OUTPROJ_TM = 256


def _outproj(att, ssm_n, x, wa, ws, g_att, gt1, g_ffn, sh2, sc2, wr, seq_len, t_all, row_off, shared):
    t, d = x.shape
    da, ds = att.shape[1], ssm_n.shape[1]
    tm = OUTPROJ_TM
    assert t % tm == 0 and row_off % tm == 0 and t_all % tm == 0
    off = row_off // tm
    gt_op, gt_spec = _mod_operand(gt1, seq_len, tm)
    sh_op, sh_spec = _mod_operand(sh2, seq_len, tm)
    sc_op, sc_spec = _mod_operand(sc2, seq_len, tm)
    row = lambda n: pl.BlockSpec((tm, n), lambda i: (i, 0))
    row_shared = lambda n: pl.BlockSpec((tm, n), lambda i: (i + off, 0))
    const = lambda a, b: pl.BlockSpec((a, b), lambda i: (0, 0))
    operands = [att, ssm_n, x, wa, ws, g_att.reshape(1, da), gt_op, g_ffn.reshape(1, d), sh_op, sc_op, wr]
    in_specs = [row(da), row(ds), row(d), const(da, d), const(ds, d), const(1, da),
                gt_spec, const(1, d), sh_spec, sc_spec, const(d, 2 * LANES)]
    aliases = {}
    if shared is not None:
        aliases = {len(operands): 1, len(operands) + 1: 2}
        operands += list(shared)
        in_specs += [pl.BlockSpec(memory_space=pl.ANY)] * 2
    x1, h2, lg = pl.pallas_call(
        _outproj_kernel,
        out_shape=(jax.ShapeDtypeStruct((t, d), F32), jax.ShapeDtypeStruct((t_all, d), BF16),
                   jax.ShapeDtypeStruct((t_all, LANES), F32)),
        grid=(t // tm,),
        in_specs=in_specs,
        out_specs=(row(d), row_shared(d), row_shared(LANES)),
        input_output_aliases=aliases,
        compiler_params=_cparams("arbitrary"),
        name="out_proj",
    )(*operands)
    return x1, (h2, lg)


MOE_TM = 256
MOE_CHUNKS = 6
COMBINE_CHUNKS = 4


def _expert_kernel(te_ref, nu_ref, xs_ref, rw_ref, w1_ref, w3_ref, w2_ref, *rest):
    o_ref, w1_scr, w3_scr, w2_scr = rest[-4:]
    i = pl.program_id(0)
    live = i < nu_ref[0]

    @pl.when(live & ((i == 0) | (te_ref[i] != te_ref[jnp.maximum(i - 1, 0)])))
    def _():
        w1_scr[...] = w1_ref[...].astype(BF16)
        w3_scr[...] = w3_ref[...].astype(BF16)
        w2_scr[...] = w2_ref[...].astype(BF16)

    @pl.when(live)
    def _():
        x = xs_ref[...]
        h1 = jnp.dot(x, w1_scr[...], preferred_element_type=F32)
        h3 = jnp.dot(x, w3_scr[...], preferred_element_type=F32)
        hid = (h1 * jax.nn.sigmoid(h1) * h3).astype(BF16)
        o_ref[...] = jnp.dot(hid, w2_scr[...], preferred_element_type=F32) * rw_ref[...]

    @pl.when(i >= nu_ref[0])
    def _():
        o_ref[...] = jnp.zeros(o_ref.shape, F32)


def _route(logits, n_groups, n_experts, top_k, tm, n_chunks):
    t = logits.shape[0]
    epg = n_experts // n_groups
    tok = jnp.arange(t)
    g_logits = logits[:, :n_groups]
    p_group = jax.nn.softmax(g_logits, axis=-1)
    g_sel = jnp.argmax(g_logits, axis=-1)
    e_sel = logits[:, n_groups:n_groups + n_experts].reshape(t, n_groups, epg)[tok, g_sel]
    top_v, top_i = lax.top_k(e_sel, top_k)
    gate = p_group[tok, g_sel][:, None] * jax.nn.softmax(top_v, axis=-1)
    eid = (g_sel[:, None] * epg + top_i).reshape(-1).astype(jnp.int32)
    n_asg = t * top_k
    onehot = (eid[:, None] == jnp.arange(n_experts, dtype=jnp.int32)[None, :]).astype(jnp.int32)
    csum = jnp.cumsum(onehot, axis=0)
    rank = jnp.take_along_axis(csum, eid[:, None], axis=1)[:, 0] - 1
    counts = csum[-1]
    padded = (counts + tm - 1) // tm * tm
    pends = jnp.cumsum(padded)
    dest = (pends - padded)[eid] + rank
    n_pad = -(-(n_asg + n_experts * (tm - 1)) // (tm * n_chunks)) * (tm * n_chunks)
    row_asg = jnp.full((n_pad,), -1, jnp.int32).at[dest].set(
        jnp.arange(n_asg, dtype=jnp.int32), unique_indices=True, mode="promise_in_bounds")
    live = row_asg >= 0
    row_tok = jnp.where(live, row_asg // top_k, 0)
    row_w = jnp.where(live, gate.reshape(-1)[jnp.maximum(row_asg, 0)], 0.0)
    n_tiles = n_pad // tm
    tile_e = jnp.sum(pends[None, :] <= (jnp.arange(n_tiles, dtype=jnp.int32) * tm)[:, None], axis=1)
    tile_e = jnp.minimum(tile_e, n_experts - 1).astype(jnp.int32)
    n_used = (pends[-1] // tm).astype(jnp.int32).reshape(1)
    return row_tok, row_w, tile_e, n_used, dest.reshape(t, top_k)


def _experts(h2, logits, w1, w3, w2, n_groups, top_k):
    t, d = h2.shape
    n_experts, _, de = w1.shape
    tm = MOE_TM
    row_tok, row_w, tile_e, n_used, pos = _route(logits, n_groups, n_experts, top_k, tm, MOE_CHUNKS)
    n_pad = row_tok.shape[0]
    ct = n_pad // tm // MOE_CHUNKS
    rw = row_w.reshape(n_pad, 1)
    single = pl.Buffered(1)
    ys = None
    for c in range(MOE_CHUNKS):
        off = c * ct
        rows = slice(off * tm, (off + ct) * tm)
        xs = h2.at[row_tok[rows]].get(mode="promise_in_bounds")
        operands = [tile_e[off:off + ct], jnp.clip(n_used - off, 0, ct), xs, rw[rows], w1, w3, w2]
        in_specs = [pl.BlockSpec((tm, d), lambda i, te, nu: (i, 0)),
                    pl.BlockSpec((tm, 1), lambda i, te, nu: (i, 0)),
                    pl.BlockSpec((None, d, de), lambda i, te, nu: (te[i], 0, 0), pipeline_mode=single),
                    pl.BlockSpec((None, d, de), lambda i, te, nu: (te[i], 0, 0), pipeline_mode=single),
                    pl.BlockSpec((None, de, d), lambda i, te, nu: (te[i], 0, 0), pipeline_mode=single)]
        aliases = {}
        if ys is not None:
            aliases = {len(operands): 0}
            operands.append(ys)
            in_specs.append(pl.BlockSpec(memory_space=pl.ANY))
        ys = pl.pallas_call(
            _expert_kernel,
            out_shape=jax.ShapeDtypeStruct((n_pad, d), F32),
            grid_spec=pltpu.PrefetchScalarGridSpec(
                num_scalar_prefetch=2,
                grid=(ct,),
                in_specs=in_specs,
                out_specs=pl.BlockSpec((tm, d), lambda i, te, nu: (i + off, 0)),
                scratch_shapes=[pltpu.VMEM((d, de), BF16), pltpu.VMEM((d, de), BF16), pltpu.VMEM((de, d), BF16)]),
            input_output_aliases=aliases,
            compiler_params=_cparams("arbitrary"),
            name="moe_experts",
        )(*operands)
    return ys, pos


def _final_kernel(x_ref, gt_ref, g_ref, *rest, n_rows):
    y_refs, o_ref = rest[:n_rows], rest[-1]
    moe = y_refs[0][...]
    for r in y_refs[1:]:
        moe = moe + r[...]
    o_ref[...] = _rms(x_ref[...] + gt_ref[...] * moe, g_ref[...])


def _final(x1, gt2, g_final, ys, pos, seq_len, n_chunks):
    t, d = x1.shape
    tm = _tile(t, 256)
    ct = t // tm // n_chunks
    assert ct * n_chunks * tm == t
    y = None
    for c in range(n_chunks):
        off = c * ct
        gt_op, gt_spec = _mod_operand(gt2, seq_len, tm, off)
        rows = [ys.at[pos[off * tm:(off + ct) * tm, k]].get(mode="promise_in_bounds") for k in range(pos.shape[1])]
        local = pl.BlockSpec((tm, d), lambda i: (i, 0))
        shifted = pl.BlockSpec((tm, d), lambda i: (i + off, 0))
        operands = [x1, gt_op, g_final.reshape(1, d)] + rows
        in_specs = [shifted, gt_spec, pl.BlockSpec((1, d), lambda i: (0, 0))] + [local] * len(rows)
        aliases = {}
        if y is not None:
            aliases = {len(operands): 0}
            operands.append(y)
            in_specs.append(pl.BlockSpec(memory_space=pl.ANY))
        y = pl.pallas_call(
            functools.partial(_final_kernel, n_rows=len(rows)),
            out_shape=jax.ShapeDtypeStruct((t, d), F32),
            grid=(ct,),
            in_specs=in_specs,
            out_specs=shifted,
            input_output_aliases=aliases,
            compiler_params=_cparams("arbitrary"),
            name="moe_combine_norm",
        )(*operands)
    return y


S5_CHUNK = 16
S5_SEGMENTS = 4
TOP_K = 2


def _mixers(x, mod, cache, wts, s5_ops, t_all, row_off, shared):
    (g_mix, w4, wf, bfp, d_skip, w_glu, b_glu, g_att, g_ssm, wa, ws, g_ffn, wr,
     n_heads, hd, n_groups_ssm) = wts
    bsz, seq, d = x.shape
    da = n_heads * hd
    t = bsz * seq
    xt = x.reshape(t, d)
    sh1, sc1, gt1, sh2, sc2, gt2 = jnp.split(mod, 6, axis=-1)
    qb, kf, vf, kb, vb, u, lfp = _inproj(xt, g_mix, sh1, sc1, w4, wf, bfp, seq, hd ** -0.5 * LOG2E)
    logf = lfp[:, :n_heads].reshape(bsz, seq, n_heads)
    if cache is None:
        fcum = _cumsum_rows(jnp.swapaxes(logf, 1, 2).reshape(bsz * n_heads, seq))[:, :seq]
        att = _fox_prompt(qb, kb, vb, (fcum * LOG2E).reshape(bsz, n_heads, seq), bsz, seq, n_heads, hd)
        h0 = jnp.zeros((bsz, n_groups_ssm, s5_ops[3].shape[2] // S5_IN_GROUPS), F32)
        ssm_y, h_re, h_im = _s5(u, h0, h0, s5_ops, bsz, S5_CHUNK, S5_SEGMENTS)
    else:
        cache_k, cache_v, cache_logf, st_re, st_im = cache
        past = cache_k.shape[1]
        lf_all = jnp.concatenate([cache_logf.astype(F32), logf], axis=1)
        f_all = _cumsum_rows(jnp.swapaxes(lf_all, 1, 2).reshape(bsz * n_heads, past + seq))
        f_all = (f_all[:, :past + seq] * LOG2E).reshape(bsz, n_heads, past + seq)
        att = _fox_sample(qb.reshape(bsz, seq, n_heads, hd), kf.reshape(bsz, seq, n_heads, hd),
                          vf.reshape(bsz, seq, n_heads, hd), f_all, cache_k, cache_v)
        att = att.reshape(t, da)
        ssm_y, h_re, h_im = _s5(u, st_re.astype(F32), st_im.astype(F32), s5_ops, bsz, seq, 1)
    ssm_n = _glu(ssm_y, u, d_skip, w_glu, b_glu, g_ssm)
    x1, shared = _outproj(att, ssm_n, xt, wa, ws, g_att, gt1, g_ffn, sh2, sc2, wr, seq, t_all, row_off, shared)
    new_cache = (kf.reshape(bsz, seq, n_heads, hd), vf.reshape(bsz, seq, n_heads, hd), logf, h_re, h_im)
    return x1, gt2, shared, new_cache


def kernel(x_prompt, x_sample, cache_k, cache_v, cache_logf, state_ssm_re, state_ssm_im, c_prompt, c_sample, w_ada, b_ada, g_mix, w_in, b_f, lam_re, lam_im, log_dt, b_re, b_im, c_re, c_im, d_skip, w_glu, b_glu, g_att, g_ssm, w_out, g_ffn, w_rg, w_re, w1, w3, w2, g_final):
    depth = w_ada.shape[0]
    assert depth == 1, "the residual stream of a deeper stack would have to be threaded through the layers"
    n_heads, hd = cache_k.shape[3], cache_k.shape[4]
    da = n_heads * hd
    d = x_prompt.shape[-1]
    ds = d - da
    assert da == ds
    n_groups_ssm = state_ssm_re.shape[2]
    n_expert_groups = w_rg.shape[-1]
    n_experts = w_re.shape[-1]
    assert n_expert_groups + n_experts <= LANES and n_heads <= LANES
    l = 0
    bp = x_prompt.shape[0]
    mod = _ada(jnp.concatenate([c_prompt, c_sample], axis=0).astype(F32), w_ada[l], b_ada[l])
    wi = w_in[l]
    w4 = jnp.concatenate([wi[:, :3 * da], wi[:, 3 * da + n_heads:]], axis=1).astype(BF16)
    wf = jnp.pad(wi[:, 3 * da:3 * da + n_heads], ((0, 0), (0, LANES - n_heads))).astype(BF16)
    bfp = jnp.pad(b_f[l], (0, LANES - n_heads)).reshape(1, LANES).astype(F32)
    wr = jnp.pad(jnp.concatenate([w_rg[l], w_re[l]], axis=1),
                 ((0, 0), (0, LANES - n_expert_groups - n_experts))).astype(F32)
    wr_hi = wr.astype(BF16)
    wr_lo = (wr - wr_hi.astype(F32)).astype(BF16)
    wr2 = jnp.concatenate([wr_hi, wr_lo], axis=1)
    wo = _to_bf16(w_out[l])
    wts = (g_mix[l], w4, wf, bfp, d_skip[l], _to_bf16(w_glu[l]), b_glu[l], g_att[l], g_ssm[l],
           wo[:da], wo[da:], g_ffn[l], wr2, n_heads, hd, n_groups_ssm)
    s5_args = (lam_re[l].astype(F32), lam_im[l].astype(F32), log_dt[l], b_re[l].astype(F32), b_im[l].astype(F32),
               c_re[l].astype(F32), c_im[l].astype(F32))
    seq_p = x_prompt.shape[1]
    n_steps_p = seq_p // (S5_SEGMENTS * S5_CHUNK)
    if x_sample.shape[1] == S5_CHUNK:
        ops_p, ops_s = _s5_operators(*s5_args, S5_CHUNK, (n_steps_p, 1))
    else:
        (ops_p,) = _s5_operators(*s5_args, S5_CHUNK, (n_steps_p,))
        (ops_s,) = _s5_operators(*s5_args, x_sample.shape[1], (1,))
    bs, seq_s = x_sample.shape[:2]
    t_p, t_s = bp * seq_p, bs * seq_s
    x1p, gt2p, shared, (kp, vp, lfp, rep, imp) = _mixers(
        x_prompt.astype(F32), mod[:bp], None, wts, ops_p, t_p + t_s, 0, None)
    cache = (cache_k[l], cache_v[l], cache_logf[l], state_ssm_re[l], state_ssm_im[l])
    x1s, gt2s, (h2, logits), (ksm, vsm, lfs, res, ims) = _mixers(
        x_sample.astype(F32), mod[bp:], cache, wts, ops_s, t_p + t_s, t_p, shared)
    ys, pos = _experts(h2, logits, w1[l], w3[l], w2[l], n_expert_groups, TOP_K)
    yp = _final(x1p, gt2p, g_final, ys, pos[:t_p], seq_p, COMBINE_CHUNKS).reshape(x_prompt.shape)
    ysm = _final(x1s, gt2s, g_final, ys, pos[t_p:], seq_s, 1).reshape(x_sample.shape)
    return (yp, ysm, kp[None], vp[None], lfp[None], rep[None], imp[None],
            ksm[None], vsm[None], lfs[None], res[None], ims[None])
```

```python
import functools
import math

import jax
import jax.numpy as jnp
from jax import lax
from jax.experimental import pallas as pl
from jax.experimental.pallas import tpu as pltpu

F32 = jnp.float32
BF16 = jnp.bfloat16
EPS = 1e-6
NEG = -1e30
LOG2E = math.log2(math.e)
LANES = 128
SUBLANES = 8
VMEM_LIMIT = 56 * 1024 * 1024
HIGHEST = lax.Precision.HIGHEST
NT_DIMS = (((1,), (1,)), ((), ()))


def _cparams(*sem):
    return pltpu.CompilerParams(dimension_semantics=sem, vmem_limit_bytes=VMEM_LIMIT)


def _tile(n, pref):
    t = min(n, pref)
    assert n % t == 0, (n, pref)
    return t


def _rms(x, g):
    return x * lax.rsqrt(jnp.mean(x * x, axis=-1, keepdims=True) + EPS) * g


def _mod_operand(vec, seq_len, tm, off=0):
    n_seq, d = vec.shape
    if seq_len % tm == 0:
        per = seq_len // tm
        return vec[:, None, :], pl.BlockSpec((None, 1, d), lambda i, *_: ((i + off) // per, 0, 0))
    assert tm % seq_len == 0
    rows = jnp.repeat(vec, seq_len, axis=0).reshape(-1, tm, d)
    return rows, pl.BlockSpec((None, tm, d), lambda i, *_: (i + off, 0, 0))


def _cast_kernel(x_ref, o_ref):
    o_ref[...] = x_ref[...].astype(BF16)


def _to_bf16(w):
    n = w.shape[-1]
    w2 = w.reshape(-1, n)
    rows = w2.shape[0]
    tr = _tile(rows, max(SUBLANES, (1 << 20) // n))
    out = pl.pallas_call(
        _cast_kernel,
        out_shape=jax.ShapeDtypeStruct((rows, n), BF16),
        grid=(rows // tr,),
        in_specs=[pl.BlockSpec((tr, n), lambda i: (i, 0))],
        out_specs=pl.BlockSpec((tr, n), lambda i: (i, 0)),
        compiler_params=_cparams("arbitrary"),
        name="cast_bf16",
    )(w2)
    return out.reshape(w.shape)


def _ada_kernel(c_ref, w_ref, b_ref, o_ref):
    c = c_ref[...]
    a = (c * jax.nn.sigmoid(c)).astype(BF16)
    o_ref[...] = jnp.dot(a, w_ref[...].astype(BF16), preferred_element_type=F32) + b_ref[...]


def _ada(c, w, b):
    s, d = c.shape
    n = w.shape[1]
    tn = _tile(n, 1024)
    return pl.pallas_call(
        _ada_kernel,
        out_shape=jax.ShapeDtypeStruct((s, n), F32),
        grid=(n // tn,),
        in_specs=[pl.BlockSpec((s, d), lambda j: (0, 0)),
                  pl.BlockSpec((d, tn), lambda j: (0, j)),
                  pl.BlockSpec((1, tn), lambda j: (0, j))],
        out_specs=pl.BlockSpec((s, tn), lambda j: (0, j)),
        compiler_params=_cparams("arbitrary"),
        name="ada_mod",
    )(c, w, b.reshape(1, n))


def _inproj_kernel(x_ref, g_ref, sh_ref, sc_ref, w_ref, wf_ref, bf_ref,
                   q_ref, kf_ref, vf_ref, kb_ref, vb_ref, u_ref, lf_ref, *, qscale):
    da = q_ref.shape[1]
    h = _rms(x_ref[...], g_ref[...]) * (1.0 + sc_ref[...]) + sh_ref[...]
    hb = h.astype(BF16)
    fg = jnp.dot(hb, wf_ref[...], preferred_element_type=F32) + bf_ref[...]
    lf_ref[...] = jnp.minimum(fg, 0.0) - jnp.log1p(jnp.exp(-jnp.abs(fg)))
    proj = lambda j: jnp.dot(hb, w_ref[:, j * da:(j + 1) * da], preferred_element_type=F32)
    q_ref[...] = (proj(0) * qscale).astype(BF16)
    k = proj(1)
    kf_ref[...] = k
    kb_ref[...] = k.astype(BF16)
    v = proj(2)
    vf_ref[...] = v
    vb_ref[...] = v.astype(BF16)
    u = proj(3)
    for blk in range(u_ref.shape[0]):
        u_ref[blk] = u[:, blk * LANES:(blk + 1) * LANES]


def _inproj(x, g, sh, sc, w4, wf, bfp, seq_len, qscale):
    t, d = x.shape
    da = w4.shape[1] // 4
    tm = _tile(t, 512)
    sh_op, sh_spec = _mod_operand(sh, seq_len, tm)
    sc_op, sc_spec = _mod_operand(sc, seq_len, tm)
    row = lambda i: (i, 0)
    const = lambda i: (0, 0)
    outs = pl.pallas_call(
        functools.partial(_inproj_kernel, qscale=qscale),
        out_shape=(jax.ShapeDtypeStruct((t, da), BF16),
                   jax.ShapeDtypeStruct((t, da), F32), jax.ShapeDtypeStruct((t, da), F32),
                   jax.ShapeDtypeStruct((t, da), BF16), jax.ShapeDtypeStruct((t, da), BF16),
                   jax.ShapeDtypeStruct((da // LANES, t, LANES), F32),
                   jax.ShapeDtypeStruct((t, LANES), F32)),
        grid=(t // tm,),
        in_specs=[pl.BlockSpec((tm, d), row),
                  pl.BlockSpec((1, d), const),
                  sh_spec, sc_spec,
                  pl.BlockSpec((d, 4 * da), const, pipeline_mode=pl.Buffered(1)),
                  pl.BlockSpec((d, LANES), const),
                  pl.BlockSpec((1, LANES), const)],
        out_specs=(pl.BlockSpec((tm, da), row),) * 5
                  + (pl.BlockSpec((da // LANES, tm, LANES), lambda i: (0, i, 0)), pl.BlockSpec((tm, LANES), row)),
        compiler_params=_cparams("arbitrary"),
        name="in_proj",
    )(x, g.reshape(1, d), sh_op, sc_op, w4, wf, bfp)
    return outs


def _cumsum_kernel(x_ref, o_ref):
    sb, nb, _ = x_ref.shape
    li = lax.broadcasted_iota(jnp.int32, (LANES, LANES), 0)
    lj = lax.broadcasted_iota(jnp.int32, (LANES, LANES), 1)
    upper = (li <= lj).astype(F32)
    ri = lax.broadcasted_iota(jnp.int32, (nb, nb), 0)
    rj = lax.broadcasted_iota(jnp.int32, (nb, nb), 1)
    strict = (rj < ri).astype(F32)
    for s in range(sb):
        within = jnp.dot(x_ref[s], upper, precision=HIGHEST, preferred_element_type=F32)
        tot = jnp.broadcast_to(within[:, LANES - 1:LANES], (nb, LANES))
        off = jnp.dot(strict, tot, precision=HIGHEST, preferred_element_type=F32)
        o_ref[s] = within + off


def _cumsum_rows(x):
    n_rows, n = x.shape
    nb = -(-n // (LANES * SUBLANES)) * SUBLANES
    xp = jnp.pad(x, ((0, 0), (0, nb * LANES - n))).reshape(n_rows, nb, LANES)
    sb = _tile(n_rows, 16)
    out = pl.pallas_call(
        _cumsum_kernel,
        out_shape=jax.ShapeDtypeStruct((n_rows, nb, LANES), F32),
        grid=(n_rows // sb,),
        in_specs=[pl.BlockSpec((sb, nb, LANES), lambda i: (i, 0, 0))],
        out_specs=pl.BlockSpec((sb, nb, LANES), lambda i: (i, 0, 0)),
        compiler_params=_cparams("arbitrary"),
        name="logf_cumsum",
    )(xp)
    return out.reshape(n_rows, nb * LANES)


FOX_TQ = 2048
FOX_TK = 2048
FOX_SUB = 1024


def _col_from_row(row):
    n = row.shape[1]
    eye = lax.broadcasted_iota(jnp.int32, (n, n), 0) == lax.broadcasted_iota(jnp.int32, (n, n), 1)
    return jnp.sum(jnp.where(eye, jnp.broadcast_to(row, (n, n)), 0.0), axis=1, keepdims=True)


def _lane_tiles(x):
    return [x[:, j * LANES:(j + 1) * LANES] for j in range(x.shape[1] // LANES)]


def _fox_kernel(q_ref, k_ref, v_ref, f_ref, o_ref, m_scr, l_scr, acc_scr, *, tq, tk, sub):
    qi = pl.program_id(2)
    nsub = tq // sub
    fqb = [jnp.broadcast_to(_col_from_row(f_ref[qi * nsub + a]), (sub, LANES)) for a in range(nsub)]
    m_scr[...] = jnp.full(m_scr.shape, NEG, F32)
    l_scr[...] = jnp.zeros(l_scr.shape, F32)
    acc_scr[...] = jnp.zeros(acc_scr.shape, F32)

    def chain(a, k, v, fk, diagonal):
        rows = pl.ds(a * sub, sub)
        t1 = lax.dot_general(q_ref[rows, :], k, NT_DIMS, preferred_element_type=F32) - fk
        if diagonal:
            row = lax.broadcasted_iota(jnp.int32, t1.shape, 0)
            col = lax.broadcasted_iota(jnp.int32, t1.shape, 1)
            t1 = jnp.where(col <= row, t1, NEG)
        tiles = _lane_tiles(t1)
        part = functools.reduce(jnp.maximum, tiles)
        m_prev = m_scr[rows, :]
        m_new = jnp.maximum(m_prev, jnp.max(part, axis=-1, keepdims=True) + fqb[a])
        c = m_new - fqb[a]
        p = [jnp.exp2(t - c) for t in tiles]
        alpha = jnp.exp2(m_prev - m_new)
        l_scr[rows, :] = alpha * l_scr[rows, :] + functools.reduce(jnp.add, p)
        pv = jnp.dot(jnp.concatenate(p, axis=1).astype(BF16), v, preferred_element_type=F32)
        acc_scr[rows, :] = alpha * acc_scr[rows, :] + pv
        m_scr[rows, :] = m_new

    def full_step(kt, carry):
        ks = pl.multiple_of(kt * tk, tk)
        k = k_ref[pl.ds(ks, tk), :]
        v = v_ref[pl.ds(ks, tk), :]
        fk = jnp.concatenate([f_ref[kt * (tk // sub) + j] for j in range(tk // sub)], axis=1)
        for a in range(nsub):
            chain(a, k, v, fk, False)
        return carry

    lax.fori_loop(0, qi * (tq // tk), full_step, 0)
    for a in range(nsub):
        for j in range(a + 1):
            ks = pl.multiple_of((qi * nsub + j) * sub, sub)
            chain(a, k_ref[pl.ds(ks, sub), :], v_ref[pl.ds(ks, sub), :], f_ref[qi * nsub + j], j == a)
    o_ref[...] = acc_scr[...] / jnp.sum(l_scr[...], axis=-1, keepdims=True)


def _fox_prompt(qb, kb, vb, fcum2, bsz, seq, n_heads, hd):
    assert hd == LANES
    tq = _tile(seq, FOX_TQ)
    tk = _tile(tq, FOX_TK)
    sub = _tile(tk, FOX_SUB)
    nq = seq // tq
    f = fcum2.reshape(bsz, n_heads, seq // sub, 1, sub)
    return pl.pallas_call(
        functools.partial(_fox_kernel, tq=tq, tk=tk, sub=sub),
        out_shape=jax.ShapeDtypeStruct((bsz * seq, n_heads * hd), F32),
        grid=(bsz, n_heads, nq),
        in_specs=[pl.BlockSpec((tq, hd), lambda b, h, i: (b * nq + i, h)),
                  pl.BlockSpec((seq, hd), lambda b, h, i: (b, h)),
                  pl.BlockSpec((seq, hd), lambda b, h, i: (b, h)),
                  pl.BlockSpec((None, None, seq // sub, 1, sub), lambda b, h, i: (b, h, 0, 0, 0))],
        out_specs=pl.BlockSpec((tq, hd), lambda b, h, i: (b * nq + i, h)),
        scratch_shapes=[pltpu.VMEM((tq, LANES), F32), pltpu.VMEM((tq, LANES), F32), pltpu.VMEM((tq, hd), F32)],
        compiler_params=_cparams("arbitrary", "arbitrary", "arbitrary"),
        name="fox_prompt",
    )(qb, kb, vb, f)


def _fox_sample_kernel(q_ref, ck_ref, cv_ref, kn_ref, vn_ref, fq_ref, fkc_ref, fkn_ref,
                       rh_ref, ri_ref, lh_ref, lhn_ref, kin_ref, o_ref, m_scr, l_scr, acc_scr):
    kt = pl.program_id(1)

    @pl.when(kt == 0)
    def _():
        m_scr[...] = jnp.full(m_scr.shape, NEG, F32)
        l_scr[...] = jnp.zeros(l_scr.shape, F32)
        acc_scr[...] = jnp.zeros(acc_scr.shape, F32)

    q = q_ref[...]
    fq = fq_ref[...]
    n_heads = ck_ref.shape[1]
    s_new = q.shape[0] // n_heads

    def update(k4, v4, t1_of):
        n, h, d = k4.shape
        k2 = k4.reshape(n * h, d).astype(BF16)
        v2 = v4.reshape(n * h, d).astype(BF16)
        t1 = t1_of(lax.dot_general(q, k2, NT_DIMS, preferred_element_type=F32))
        m_prev = m_scr[...]
        m_new = jnp.maximum(m_prev, jnp.max(t1, axis=-1, keepdims=True) + fq)
        alpha = jnp.exp2(m_prev - m_new)
        p = jnp.exp2(t1 - (m_new - fq))
        l_scr[...] = alpha * l_scr[...] + jnp.sum(p, axis=-1, keepdims=True)
        acc_scr[...] = alpha * acc_scr[...] + jnp.dot(p.astype(BF16), v2, preferred_element_type=F32)
        m_scr[...] = m_new

    def cache_t1(s):
        head = lax.broadcasted_iota(jnp.int32, (n_heads, 1), 0)
        fkm = jnp.where(lh_ref[...] == head, fkc_ref[...], -NEG)
        return jnp.concatenate([s[h * s_new:(h + 1) * s_new] - fkm[h:h + 1] for h in range(n_heads)], axis=0)

    update(ck_ref[...], cv_ref[...], cache_t1)

    @pl.when(kt == pl.num_programs(1) - 1)
    def _():
        causal_head = jnp.where(kin_ref[...] <= ri_ref[...], lhn_ref[...], -1)
        update(kn_ref[...], vn_ref[...],
               lambda s: jnp.where(rh_ref[...] == causal_head, s - fkn_ref[...], NEG))
        o_ref[...] = acc_scr[...] / l_scr[...]


def _fox_sample(qb, k_new, v_new, f_all2, cache_k, cache_v):
    bsz, s_new, n_heads, hd = qb.shape
    past = cache_k.shape[1]
    tk = _tile(past, 1024)
    n_rows = n_heads * s_new
    assert n_rows % 16 == 0 and s_new % SUBLANES == 0
    q2 = jnp.swapaxes(qb, 1, 2).reshape(bsz, n_rows, hd)
    fq = f_all2[:, :, past:].reshape(bsz, n_rows, 1)
    fkc = jnp.swapaxes(f_all2[:, :, :past], 1, 2).reshape(bsz, 1, past * n_heads)
    fkn = jnp.swapaxes(f_all2[:, :, past:], 1, 2).reshape(bsz, 1, s_new * n_heads)
    r = jnp.arange(n_rows, dtype=jnp.int32).reshape(n_rows, 1)
    lane = lambda n: jnp.arange(n * n_heads, dtype=jnp.int32).reshape(1, n * n_heads)
    const = lambda a: pl.BlockSpec(a.shape, lambda b, j: (0,) * a.ndim)
    consts = (r // s_new, r % s_new, lane(tk) % n_heads, lane(s_new) % n_heads, lane(s_new) // n_heads)
    out = pl.pallas_call(
        _fox_sample_kernel,
        out_shape=jax.ShapeDtypeStruct((bsz, n_rows, hd), F32),
        grid=(bsz, past // tk),
        in_specs=[pl.BlockSpec((None, n_rows, hd), lambda b, j: (b, 0, 0)),
                  pl.BlockSpec((None, tk, n_heads, hd), lambda b, j: (b, j, 0, 0)),
                  pl.BlockSpec((None, tk, n_heads, hd), lambda b, j: (b, j, 0, 0)),
                  pl.BlockSpec((None, s_new, n_heads, hd), lambda b, j: (b, 0, 0, 0)),
                  pl.BlockSpec((None, s_new, n_heads, hd), lambda b, j: (b, 0, 0, 0)),
                  pl.BlockSpec((None, n_rows, 1), lambda b, j: (b, 0, 0)),
                  pl.BlockSpec((None, 1, tk * n_heads), lambda b, j: (b, 0, j)),
                  pl.BlockSpec((None, 1, s_new * n_heads), lambda b, j: (b, 0, 0))]
                 + [const(a) for a in consts],
        out_specs=pl.BlockSpec((None, n_rows, hd), lambda b, j: (b, 0, 0)),
        scratch_shapes=[pltpu.VMEM((n_rows, 1), F32), pltpu.VMEM((n_rows, 1), F32), pltpu.VMEM((n_rows, hd), F32)],
        compiler_params=_cparams("arbitrary", "arbitrary"),
        name="fox_sample",
    )(q2, cache_k, cache_v, k_new, v_new, fq, fkc, fkn, *consts)
    return jnp.swapaxes(out.reshape(bsz, n_heads, s_new, hd), 1, 2)


S5_IN_GROUPS = 8
S5_OUT_GROUPS = 16


def _s5_operators(lam_re, lam_im, log_dt, b_re, b_im, c_re, c_im, lc, n_steps):
    g, p = lam_re.shape
    hc = b_re.shape[2]
    gi, go = S5_IN_GROUPS, S5_OUT_GROUPS
    dt = jnp.exp(log_dt.astype(F32))[:, None]

    def power(k):
        mag = jnp.exp(lam_re * dt * k)
        return mag * jnp.cos(lam_im * dt * k), mag * jnp.sin(lam_im * dt * k)

    lbr, lbi = power(1.0)
    den = lam_re * lam_re + lam_im * lam_im
    fr = ((lbr - 1.0) * lam_re + lbi * lam_im) / den
    fi = (lbi * lam_re - (lbr - 1.0) * lam_im) / den
    bbr = fr[:, :, None] * b_re - fi[:, :, None] * b_im
    bbi = fr[:, :, None] * b_im + fi[:, :, None] * b_re
    ks = jnp.arange(lc + 1, dtype=F32)[None, :, None]
    mag = jnp.exp(lam_re[:, None, :] * dt[:, None, :] * ks)
    ang = lam_im[:, None, :] * dt[:, None, :] * ks
    pwr, pwi = mag * jnp.cos(ang), mag * jnp.sin(ang)
    cr, ci = jnp.swapaxes(c_re, 1, 2), jnp.swapaxes(c_im, 1, 2)
    d_r = bbr[:, :, :, None] * cr[:, :, None, :] - bbi[:, :, :, None] * ci[:, :, None, :]
    d_i = bbr[:, :, :, None] * ci[:, :, None, :] + bbi[:, :, :, None] * cr[:, :, None, :]
    kern = (jnp.einsum("gtp,gpab->gtab", pwr[:, :lc], d_r, precision=HIGHEST)
            - jnp.einsum("gtp,gpab->gtab", pwi[:, :lc], d_i, precision=HIGHEST))
    def block_diag(vals, row_block, n_blocks):
        w = vals.shape[-1]
        tiled = jnp.tile(vals, (1,) * (vals.ndim - 1) + (n_blocks,))
        col_block = jnp.arange(n_blocks * w) // w
        return jnp.where(row_block[:, None] == col_block[None, :], tiled, 0.0).astype(BF16)

    per_slab = lambda x, n: jnp.swapaxes(x.reshape((g // n, n) + x.shape[1:]), 1, 2)
    bd = block_diag(per_slab(kern, go).reshape(g // go, lc, go * hc, hc), jnp.arange(go * hc) // hc, go)
    rev_r, rev_i = pwr[:, lc - 1::-1][:, :lc], pwi[:, lc - 1::-1][:, :lc]
    bt_r, bt_i = jnp.swapaxes(bbr, 1, 2)[:, None], jnp.swapaxes(bbi, 1, 2)[:, None]
    w_r = rev_r[:, :, None, :] * bt_r - rev_i[:, :, None, :] * bt_i
    w_i = rev_r[:, :, None, :] * bt_i + rev_i[:, :, None, :] * bt_r
    place = lambda w: block_diag(per_slab(w, gi).reshape(g // gi, lc, gi * hc, p), jnp.arange(gi * hc) // hc, gi)
    bw = jnp.concatenate([place(w_r), place(w_i)], axis=-1)
    nr, ni = jnp.swapaxes(pwr[:, 1:], 1, 2), jnp.swapaxes(pwi[:, 1:], 1, 2)
    v_r = cr[:, :, None, :] * nr[:, :, :, None] - ci[:, :, None, :] * ni[:, :, :, None]
    v_i = cr[:, :, None, :] * ni[:, :, :, None] + ci[:, :, None, :] * nr[:, :, :, None]
    halves = go // gi
    split = lambda v: jnp.transpose(v.reshape(g // go, halves, gi, p, lc, hc), (0, 4, 1, 2, 3, 5))
    bv = jnp.stack([split(v_r), split(-v_i)], axis=3)
    r = jnp.arange(halves * 2 * gi * p)
    row_group = (r // (2 * gi * p)) * gi + (r // p) % gi
    bv = block_diag(bv.reshape(g // go, lc, halves * 2 * gi * p, hc), row_group, go)

    def coef(n_steps):
        jr, ji = power(float(lc * n_steps))
        c = jnp.stack([pwr[:, lc], pwi[:, lc], jr, ji], axis=1)
        c = jnp.swapaxes(c.reshape(g // gi, gi, 4, p), 1, 2).reshape(g // gi, 4, gi * p)
        return jnp.pad(c, ((0, 0), (0, SUBLANES - 4), (0, 0)))

    return [(bd, bw, bv, coef(n)) for n in n_steps]


def _s5_state_kernel(u_ref, bw_ref, coef_ref, h0_ref, hin_ref, hl_ref, s_scr, *, lc, n_q, n_seg):
    i = pl.program_id(1)
    tr = u_ref.shape[0] // lc
    n_tiles = s_scr.shape[0]
    nh = n_tiles // 2
    half = nh * LANES
    n_steps = s_scr.shape[1] // n_q
    acc = None
    for s in range(lc):
        a = u_ref[pl.ds(s, tr, stride=lc), :].astype(BF16)
        d = jnp.dot(a, bw_ref[s], preferred_element_type=F32)
        acc = d if acc is None else acc + d
    if n_steps == 1:
        rows = pl.ds(pl.multiple_of(i * tr, tr), tr)
    else:
        assert n_steps % tr == 0
        per = n_steps // tr
        rows = pl.ds(lax.rem(i, per) * (tr * n_q) + lax.div(i, per), tr, stride=n_q)
    for m in range(n_tiles):
        s_scr[m, rows, :] = acc[:, m * LANES:(m + 1) * LANES]

    @pl.when(i == pl.num_programs(1) - 1)
    def _():
        bc = lambda r: jnp.broadcast_to(coef_ref[r:r + 1, :], (n_q, half))
        ar, ai = bc(0), bc(1)
        jr, ji = coef_ref[2:3, :], coef_ref[3:4, :]
        step_rows = lambda j: pl.ds(pl.multiple_of(j * n_q, n_q), n_q)

        def load_rows(j):
            return jnp.concatenate([s_scr[m, step_rows(j), :] for m in range(n_tiles)], axis=1)

        def store_rows(j, x):
            for m in range(n_tiles):
                s_scr[m, step_rows(j), :] = x[:, m * LANES:(m + 1) * LANES]

        def scan_zero(j, carry):
            xr, xi = carry
            s = load_rows(j)
            store_rows(j, jnp.concatenate([xr, xi], axis=1))
            return ar * xr - ai * xi + s[:, :half], ar * xi + ai * xr + s[:, half:]

        zero = jnp.zeros((n_q, half), F32)
        xr_end, xi_end = lax.fori_loop(0, n_steps, scan_zero, (zero, zero))

        h0 = h0_ref[...]
        if n_seg == 1:
            er, ei = h0[:, :half], h0[:, half:]
            hl_ref[...] = jnp.concatenate([jr * er - ji * ei + xr_end, jr * ei + ji * er + xi_end], axis=1)
        else:
            er_rows, ei_rows = [], []
            for b in range(n_q // n_seg):
                r_, i_ = h0[b:b + 1, :half], h0[b:b + 1, half:]
                for sg in range(n_seg):
                    q = b * n_seg + sg
                    er_rows.append(r_)
                    ei_rows.append(i_)
                    r_, i_ = (jr * r_ - ji * i_ + xr_end[q:q + 1], jr * i_ + ji * r_ + xi_end[q:q + 1])
                hl_ref[b:b + 1, :] = jnp.concatenate([r_, i_], axis=1)
            er = jnp.concatenate(er_rows, axis=0)
            ei = jnp.concatenate(ei_rows, axis=0)

        def scan_fix(j, carry):
            fr, fi = carry
            store_rows(j, load_rows(j) + jnp.concatenate([fr, fi], axis=1))
            return ar * fr - ai * fi, ar * fi + ai * fr

        lax.fori_loop(0, n_steps, scan_fix, (er, ei))
        for m in range(n_tiles):
            if n_steps == 1:
                hin_ref[:, m * LANES:(m + 1) * LANES] = s_scr[m].astype(BF16)
            else:
                for q in range(n_q):
                    hin_ref[q * n_steps:(q + 1) * n_steps, m * LANES:(m + 1) * LANES] = (
                        s_scr[m, pl.ds(q, n_steps, stride=n_q), :].astype(BF16))


def _s5_out_kernel(u_ref, hin_ref, bd_ref, bv_ref, y_ref, a_scr, acc_scr, *, lc):
    tc = hin_ref.shape[0]
    n_blk = u_ref.shape[0]
    for s in range(lc):
        for h in range(n_blk):
            a_scr[s * tc:(s + 1) * tc, h * LANES:(h + 1) * LANES] = (
                u_ref[h, pl.ds(s, tc, stride=lc), :].astype(BF16))
    acc_scr[...] = jnp.dot(a_scr[...], bd_ref[0], preferred_element_type=F32)
    for tau in range(1, lc):
        n = (lc - tau) * tc
        acc_scr[tau * tc:, :] += jnp.dot(a_scr[:n, :], bd_ref[tau], preferred_element_type=F32)
    hin = hin_ref[...]
    for t in range(lc):
        y_t = acc_scr[t * tc:(t + 1) * tc, :] + jnp.dot(hin, bv_ref[t], preferred_element_type=F32)
        for h in range(n_blk):
            y_ref[h, pl.ds(t, tc, stride=lc), :] = y_t[:, h * LANES:(h + 1) * LANES]


def _s5(u3, h0_re, h0_im, ops, bsz, lc, n_seg):
    bd, bw, bv, coef = ops
    t = u3.shape[1]
    ds = u3.shape[0] * LANES
    n_groups, p = h0_re.shape[1], h0_re.shape[2]
    gi, go = S5_IN_GROUPS, S5_OUT_GROUPS
    wi, wo = ds // (n_groups // gi), ds // (n_groups // go)
    ws = gi * 2 * p
    nc = t // lc
    n_q = bsz * n_seg
    assert nc % n_q == 0 and n_q % SUBLANES == 0 and wi == LANES
    tr = _tile(nc if nc == n_q else nc // n_q, 256)
    pack = lambda h: h.reshape(bsz, n_groups // gi, gi * p)
    h0 = jnp.swapaxes(jnp.concatenate([pack(h0_re), pack(h0_im)], axis=-1), 0, 1)
    hin, hl = pl.pallas_call(
        functools.partial(_s5_state_kernel, lc=lc, n_q=n_q, n_seg=n_seg),
        out_shape=(jax.ShapeDtypeStruct((nc, (n_groups // gi) * ws), BF16),
                   jax.ShapeDtypeStruct((n_groups // gi, bsz, ws), F32)),
        grid=(n_groups // gi, nc // tr),
        in_specs=[pl.BlockSpec((None, tr * lc, wi), lambda k, i: (k, i, 0)),
                  pl.BlockSpec((None, lc, wi, ws), lambda k, i: (k, 0, 0, 0)),
                  pl.BlockSpec((None, SUBLANES, ws // 2), lambda k, i: (k, 0, 0)),
                  pl.BlockSpec((None, bsz, ws), lambda k, i: (k, 0, 0))],
        out_specs=(pl.BlockSpec((nc, ws), lambda k, i: (0, k)),
                   pl.BlockSpec((None, bsz, ws), lambda k, i: (k, 0, 0))),
        scratch_shapes=[pltpu.VMEM((ws // LANES, nc, LANES), F32)],
        compiler_params=_cparams("arbitrary", "arbitrary"),
        name="s5_state",
    )(u3, bw, coef, h0)
    tc = _tile(nc, 256)
    wso = (go // gi) * ws
    y = pl.pallas_call(
        functools.partial(_s5_out_kernel, lc=lc),
        out_shape=jax.ShapeDtypeStruct(u3.shape, F32),
        grid=(n_groups // go, nc // tc),
        in_specs=[pl.BlockSpec((wo // LANES, tc * lc, LANES), lambda s, i: (s, i, 0)),
                  pl.BlockSpec((tc, wso), lambda s, i: (i, s)),
                  pl.BlockSpec((None, lc, wo, wo), lambda s, i: (s, 0, 0, 0)),
                  pl.BlockSpec((None, lc, wso, wo), lambda s, i: (s, 0, 0, 0), pipeline_mode=pl.Buffered(1))],
        out_specs=pl.BlockSpec((wo // LANES, tc * lc, LANES), lambda s, i: (s, i, 0)),
        scratch_shapes=[pltpu.VMEM((lc * tc, wo), BF16), pltpu.VMEM((lc * tc, wo), F32)],
        compiler_params=_cparams("arbitrary", "arbitrary"),
        name="s5_out",
    )(u3, hin, bd, bv)
    hl = jnp.swapaxes(hl, 0, 1)
    unpack = lambda h: h.reshape(bsz, n_groups, p)
    return y, unpack(hl[:, :, :ws // 2]), unpack(hl[:, :, ws // 2:])


def _glu_kernel(y_ref, u_ref, d_ref, w_ref, b_ref, g_ref, o_ref):
    lanes = lambda r: jnp.concatenate([r[k] for k in range(r.shape[0])], axis=1)
    z = jax.nn.gelu(lanes(y_ref) + d_ref[...] * lanes(u_ref))
    gate = jax.nn.sigmoid(jnp.dot(z.astype(BF16), w_ref[...], preferred_element_type=F32) + b_ref[...])
    o_ref[...] = _rms(z * gate, g_ref[...]).astype(BF16)


def _glu(y, u, d_skip, w_glu, b_glu, g_ssm):
    nblk, t, _ = y.shape
    ds = nblk * LANES
    tm = _tile(t, 512)
    vec = pl.BlockSpec((1, ds), lambda i: (0, 0))
    blk = pl.BlockSpec((nblk, tm, LANES), lambda i: (0, i, 0))
    return pl.pallas_call(
        _glu_kernel,
        out_shape=jax.ShapeDtypeStruct((t, ds), BF16),
        grid=(t // tm,),
        in_specs=[blk, blk, vec, pl.BlockSpec((ds, ds), lambda i: (0, 0)), vec, vec],
        out_specs=pl.BlockSpec((tm, ds), lambda i: (i, 0)),
        compiler_params=_cparams("arbitrary"),
        name="s5_glu",
    )(y, u, d_skip.reshape(1, ds), w_glu, b_glu.reshape(1, ds), g_ssm.reshape(1, ds))


def _outproj_kernel(att_ref, ssm_ref, x_ref, wa_ref, ws_ref, ga_ref, gt_ref, gf_ref, sh_ref, sc_ref, wr_ref,
                    *rest):
    x1_ref, h2_ref, lg_ref = rest[-3:]
    a = _rms(att_ref[...], ga_ref[...]).astype(BF16)
    mixed = (jnp.dot(a, wa_ref[...], preferred_element_type=F32)
             + jnp.dot(ssm_ref[...], ws_ref[...], preferred_element_type=F32))
    x1 = x_ref[...] + gt_ref[...] * mixed
    x1_ref[...] = x1
    h2 = _rms(x1, gf_ref[...]) * (1.0 + sc_ref[...]) + sh_ref[...]
    hi = h2.astype(BF16)
    h2_ref[...] = hi
    lo = (h2 - hi.astype(F32)).astype(BF16)
    r = (jnp.dot(hi, wr_ref[...], preferred_element_type=F32)
         + jnp.dot(lo, wr_ref[...], preferred_element_type=F32))
    lg_ref[...] = r[:, :LANES] + r[:, LANES:]


OUTPROJ_TM = 256


def _outproj(att, ssm_n, x, wa, ws, g_att, gt1, g_ffn, sh2, sc2, wr, seq_len, t_all, row_off, shared):
    t, d = x.shape
    da, ds = att.shape[1], ssm_n.shape[1]
    tm = OUTPROJ_TM
    assert t % tm == 0 and row_off % tm == 0 and t_all % tm == 0
    off = row_off // tm
    gt_op, gt_spec = _mod_operand(gt1, seq_len, tm)
    sh_op, sh_spec = _mod_operand(sh2, seq_len, tm)
    sc_op, sc_spec = _mod_operand(sc2, seq_len, tm)
    row = lambda n: pl.BlockSpec((tm, n), lambda i: (i, 0))
    row_shared = lambda n: pl.BlockSpec((tm, n), lambda i: (i + off, 0))
    const = lambda a, b: pl.BlockSpec((a, b), lambda i: (0, 0))
    operands = [att, ssm_n, x, wa, ws, g_att.reshape(1, da), gt_op, g_ffn.reshape(1, d), sh_op, sc_op, wr]
    in_specs = [row(da), row(ds), row(d), const(da, d), const(ds, d), const(1, da),
                gt_spec, const(1, d), sh_spec, sc_spec, const(d, 2 * LANES)]
    aliases = {}
    if shared is not None:
        aliases = {len(operands): 1, len(operands) + 1: 2}
        operands += list(shared)
        in_specs += [pl.BlockSpec(memory_space=pl.ANY)] * 2
    x1, h2, lg = pl.pallas_call(
        _outproj_kernel,
        out_shape=(jax.ShapeDtypeStruct((t, d), F32), jax.ShapeDtypeStruct((t_all, d), BF16),
                   jax.ShapeDtypeStruct((t_all, LANES), F32)),
        grid=(t // tm,),
        in_specs=in_specs,
        out_specs=(row(d), row_shared(d), row_shared(LANES)),
        input_output_aliases=aliases,
        compiler_params=_cparams("arbitrary"),
        name="out_proj",
    )(*operands)
    return x1, (h2, lg)


MOE_TM = 256
MOE_CHUNKS = 6
COMBINE_CHUNKS = 4


def _expert_kernel(te_ref, nu_ref, xs_ref, rw_ref, w1_ref, w3_ref, w2_ref, *rest):
    o_ref, w1_scr, w3_scr, w2_scr = rest[-4:]
    i = pl.program_id(0)
    live = i < nu_ref[0]

    @pl.when(live & ((i == 0) | (te_ref[i] != te_ref[jnp.maximum(i - 1, 0)])))
    def _():
        w1_scr[...] = w1_ref[...].astype(BF16)
        w3_scr[...] = w3_ref[...].astype(BF16)
        w2_scr[...] = w2_ref[...].astype(BF16)

    @pl.when(live)
    def _():
        x = xs_ref[...]
        h1 = jnp.dot(x, w1_scr[...], preferred_element_type=F32)
        h3 = jnp.dot(x, w3_scr[...], preferred_element_type=F32)
        hid = (h1 * jax.nn.sigmoid(h1) * h3).astype(BF16)
        o_ref[...] = jnp.dot(hid, w2_scr[...], preferred_element_type=F32) * rw_ref[...]

    @pl.when(i >= nu_ref[0])
    def _():
        o_ref[...] = jnp.zeros(o_ref.shape, F32)


def _route(logits, n_groups, n_experts, top_k, tm, n_chunks):
    t = logits.shape[0]
    epg = n_experts // n_groups
    tok = jnp.arange(t)
    g_logits = logits[:, :n_groups]
    p_group = jax.nn.softmax(g_logits, axis=-1)
    g_sel = jnp.argmax(g_logits, axis=-1)
    e_sel = logits[:, n_groups:n_groups + n_experts].reshape(t, n_groups, epg)[tok, g_sel]
    top_v, top_i = lax.top_k(e_sel, top_k)
    gate = p_group[tok, g_sel][:, None] * jax.nn.softmax(top_v, axis=-1)
    eid = (g_sel[:, None] * epg + top_i).reshape(-1).astype(jnp.int32)
    n_asg = t * top_k
    onehot = (eid[:, None] == jnp.arange(n_experts, dtype=jnp.int32)[None, :]).astype(jnp.int32)
    csum = jnp.cumsum(onehot, axis=0)
    rank = jnp.take_along_axis(csum, eid[:, None], axis=1)[:, 0] - 1
    counts = csum[-1]
    padded = (counts + tm - 1) // tm * tm
    pends = jnp.cumsum(padded)
    dest = (pends - padded)[eid] + rank
    n_pad = -(-(n_asg + n_experts * (tm - 1)) // (tm * n_chunks)) * (tm * n_chunks)
    row_asg = jnp.full((n_pad,), -1, jnp.int32).at[dest].set(
        jnp.arange(n_asg, dtype=jnp.int32), unique_indices=True, mode="promise_in_bounds")
    live = row_asg >= 0
    row_tok = jnp.where(live, row_asg // top_k, 0)
    row_w = jnp.where(live, gate.reshape(-1)[jnp.maximum(row_asg, 0)], 0.0)
    n_tiles = n_pad // tm
    tile_e = jnp.sum(pends[None, :] <= (jnp.arange(n_tiles, dtype=jnp.int32) * tm)[:, None], axis=1)
    tile_e = jnp.minimum(tile_e, n_experts - 1).astype(jnp.int32)
    n_used = (pends[-1] // tm).astype(jnp.int32).reshape(1)
    return row_tok, row_w, tile_e, n_used, dest.reshape(t, top_k)


def _experts(h2, logits, w1, w3, w2, n_groups, top_k):
    t, d = h2.shape
    n_experts, _, de = w1.shape
    tm = MOE_TM
    row_tok, row_w, tile_e, n_used, pos = _route(logits, n_groups, n_experts, top_k, tm, MOE_CHUNKS)
    n_pad = row_tok.shape[0]
    ct = n_pad // tm // MOE_CHUNKS
    rw = row_w.reshape(n_pad, 1)
    single = pl.Buffered(1)
    ys = None
    for c in range(MOE_CHUNKS):
        off = c * ct
        rows = slice(off * tm, (off + ct) * tm)
        xs = h2.at[row_tok[rows]].get(mode="promise_in_bounds")
        operands = [tile_e[off:off + ct], jnp.clip(n_used - off, 0, ct), xs, rw[rows], w1, w3, w2]
        in_specs = [pl.BlockSpec((tm, d), lambda i, te, nu: (i, 0)),
                    pl.BlockSpec((tm, 1), lambda i, te, nu: (i, 0)),
                    pl.BlockSpec((None, d, de), lambda i, te, nu: (te[i], 0, 0), pipeline_mode=single),
                    pl.BlockSpec((None, d, de), lambda i, te, nu: (te[i], 0, 0), pipeline_mode=single),
                    pl.BlockSpec((None, de, d), lambda i, te, nu: (te[i], 0, 0), pipeline_mode=single)]
        aliases = {}
        if ys is not None:
            aliases = {len(operands): 0}
            operands.append(ys)
            in_specs.append(pl.BlockSpec(memory_space=pl.ANY))
        ys = pl.pallas_call(
            _expert_kernel,
            out_shape=jax.ShapeDtypeStruct((n_pad, d), F32),
            grid_spec=pltpu.PrefetchScalarGridSpec(
                num_scalar_prefetch=2,
                grid=(ct,),
                in_specs=in_specs,
                out_specs=pl.BlockSpec((tm, d), lambda i, te, nu: (i + off, 0)),
                scratch_shapes=[pltpu.VMEM((d, de), BF16), pltpu.VMEM((d, de), BF16), pltpu.VMEM((de, d), BF16)]),
            input_output_aliases=aliases,
            compiler_params=_cparams("arbitrary"),
            name="moe_experts",
        )(*operands)
    return ys, pos


def _final_kernel(x_ref, gt_ref, g_ref, *rest, n_rows):
    y_refs, o_ref = rest[:n_rows], rest[-1]
    moe = y_refs[0][...]
    for r in y_refs[1:]:
        moe = moe + r[...]
    o_ref[...] = _rms(x_ref[...] + gt_ref[...] * moe, g_ref[...])


def _final(x1, gt2, g_final, ys, pos, seq_len, n_chunks):
    t, d = x1.shape
    tm = _tile(t, 256)
    ct = t // tm // n_chunks
    assert ct * n_chunks * tm == t
    y = None
    for c in range(n_chunks):
        off = c * ct
        gt_op, gt_spec = _mod_operand(gt2, seq_len, tm, off)
        rows = [ys.at[pos[off * tm:(off + ct) * tm, k]].get(mode="promise_in_bounds") for k in range(pos.shape[1])]
        local = pl.BlockSpec((tm, d), lambda i: (i, 0))
        shifted = pl.BlockSpec((tm, d), lambda i: (i + off, 0))
        operands = [x1, gt_op, g_final.reshape(1, d)] + rows
        in_specs = [shifted, gt_spec, pl.BlockSpec((1, d), lambda i: (0, 0))] + [local] * len(rows)
        aliases = {}
        if y is not None:
            aliases = {len(operands): 0}
            operands.append(y)
            in_specs.append(pl.BlockSpec(memory_space=pl.ANY))
        y = pl.pallas_call(
            functools.partial(_final_kernel, n_rows=len(rows)),
            out_shape=jax.ShapeDtypeStruct((t, d), F32),
            grid=(ct,),
            in_specs=in_specs,
            out_specs=shifted,
            input_output_aliases=aliases,
            compiler_params=_cparams("arbitrary"),
            name="moe_combine_norm",
        )(*operands)
    return y


S5_CHUNK = 16
S5_SEGMENTS = 4
TOP_K = 2


def _mixers(x, mod, cache, wts, s5_ops, t_all, row_off, shared):
    (g_mix, w4, wf, bfp, d_skip, w_glu, b_glu, g_att, g_ssm, wa, ws, g_ffn, wr,
     n_heads, hd, n_groups_ssm) = wts
    bsz, seq, d = x.shape
    da = n_heads * hd
    t = bsz * seq
    xt = x.reshape(t, d)
    sh1, sc1, gt1, sh2, sc2, gt2 = jnp.split(mod, 6, axis=-1)
    qb, kf, vf, kb, vb, u, lfp = _inproj(xt, g_mix, sh1, sc1, w4, wf, bfp, seq, hd ** -0.5 * LOG2E)
    logf = lfp[:, :n_heads].reshape(bsz, seq, n_heads)
    if cache is None:
        fcum = _cumsum_rows(jnp.swapaxes(logf, 1, 2).reshape(bsz * n_heads, seq))[:, :seq]
        att = _fox_prompt(qb, kb, vb, (fcum * LOG2E).reshape(bsz, n_heads, seq), bsz, seq, n_heads, hd)
        h0 = jnp.zeros((bsz, n_groups_ssm, s5_ops[3].shape[2] // S5_IN_GROUPS), F32)
        ssm_y, h_re, h_im = _s5(u, h0, h0, s5_ops, bsz, S5_CHUNK, S5_SEGMENTS)
    else:
        cache_k, cache_v, cache_logf, st_re, st_im = cache
        past = cache_k.shape[1]
        lf_all = jnp.concatenate([cache_logf.astype(F32), logf], axis=1)
        f_all = _cumsum_rows(jnp.swapaxes(lf_all, 1, 2).reshape(bsz * n_heads, past + seq))
        f_all = (f_all[:, :past + seq] * LOG2E).reshape(bsz, n_heads, past + seq)
        att = _fox_sample(qb.reshape(bsz, seq, n_heads, hd), kf.reshape(bsz, seq, n_heads, hd),
                          vf.reshape(bsz, seq, n_heads, hd), f_all, cache_k, cache_v)
        att = att.reshape(t, da)
        ssm_y, h_re, h_im = _s5(u, st_re.astype(F32), st_im.astype(F32), s5_ops, bsz, seq, 1)
    ssm_n = _glu(ssm_y, u, d_skip, w_glu, b_glu, g_ssm)
    x1, shared = _outproj(att, ssm_n, xt, wa, ws, g_att, gt1, g_ffn, sh2, sc2, wr, seq, t_all, row_off, shared)
    new_cache = (kf.reshape(bsz, seq, n_heads, hd), vf.reshape(bsz, seq, n_heads, hd), logf, h_re, h_im)
    return x1, gt2, shared, new_cache


def kernel(x_prompt, x_sample, cache_k, cache_v, cache_logf, state_ssm_re, state_ssm_im, c_prompt, c_sample, w_ada, b_ada, g_mix, w_in, b_f, lam_re, lam_im, log_dt, b_re, b_im, c_re, c_im, d_skip, w_glu, b_glu, g_att, g_ssm, w_out, g_ffn, w_rg, w_re, w1, w3, w2, g_final):
    depth = w_ada.shape[0]
    assert depth == 1, "the residual stream of a deeper stack would have to be threaded through the layers"
    n_heads, hd = cache_k.shape[3], cache_k.shape[4]
    da = n_heads * hd
    d = x_prompt.shape[-1]
    ds = d - da
    assert da == ds
    n_groups_ssm = state_ssm_re.shape[2]
    n_expert_groups = w_rg.shape[-1]
    n_experts = w_re.shape[-1]
    assert n_expert_groups + n_experts <= LANES and n_heads <= LANES
    l = 0
    bp = x_prompt.shape[0]
    mod = _ada(jnp.concatenate([c_prompt, c_sample], axis=0).astype(F32), w_ada[l], b_ada[l])
    wi = w_in[l]
    w4 = jnp.concatenate([wi[:, :3 * da], wi[:, 3 * da + n_heads:]], axis=1).astype(BF16)
    wf = jnp.pad(wi[:, 3 * da:3 * da + n_heads], ((0, 0), (0, LANES - n_heads))).astype(BF16)
    bfp = jnp.pad(b_f[l], (0, LANES - n_heads)).reshape(1, LANES).astype(F32)
    wr = jnp.pad(jnp.concatenate([w_rg[l], w_re[l]], axis=1),
                 ((0, 0), (0, LANES - n_expert_groups - n_experts))).astype(F32)
    wr_hi = wr.astype(BF16)
    wr_lo = (wr - wr_hi.astype(F32)).astype(BF16)
    wr2 = jnp.concatenate([wr_hi, wr_lo], axis=1)
    wo = _to_bf16(w_out[l])
    wts = (g_mix[l], w4, wf, bfp, d_skip[l], _to_bf16(w_glu[l]), b_glu[l], g_att[l], g_ssm[l],
           wo[:da], wo[da:], g_ffn[l], wr2, n_heads, hd, n_groups_ssm)
    s5_args = (lam_re[l].astype(F32), lam_im[l].astype(F32), log_dt[l], b_re[l].astype(F32), b_im[l].astype(F32),
               c_re[l].astype(F32), c_im[l].astype(F32))
    seq_p = x_prompt.shape[1]
    n_steps_p = seq_p // (S5_SEGMENTS * S5_CHUNK)
    if x_sample.shape[1] == S5_CHUNK:
        ops_p, ops_s = _s5_operators(*s5_args, S5_CHUNK, (n_steps_p, 1))
    else:
        (ops_p,) = _s5_operators(*s5_args, S5_CHUNK, (n_steps_p,))
        (ops_s,) = _s5_operators(*s5_args, x_sample.shape[1], (1,))
    bs, seq_s = x_sample.shape[:2]
    t_p, t_s = bp * seq_p, bs * seq_s
    x1p, gt2p, shared, (kp, vp, lfp, rep, imp) = _mixers(
        x_prompt.astype(F32), mod[:bp], None, wts, ops_p, t_p + t_s, 0, None)
    cache = (cache_k[l], cache_v[l], cache_logf[l], state_ssm_re[l], state_ssm_im[l])
    x1s, gt2s, (h2, logits), (ksm, vsm, lfs, res, ims) = _mixers(
        x_sample.astype(F32), mod[bp:], cache, wts, ops_s, t_p + t_s, t_p, shared)
    ys, pos = _experts(h2, logits, w1[l], w3[l], w2[l], n_expert_groups, TOP_K)
    yp = _final(x1p, gt2p, g_final, ys, pos[:t_p], seq_p, COMBINE_CHUNKS).reshape(x_prompt.shape)
    ysm = _final(x1s, gt2s, g_final, ys, pos[t_p:], seq_s, 1).reshape(x_sample.shape)
    return (yp, ysm, kp[None], vp[None], lfp[None], rep[None], imp[None],
            ksm[None], vsm[None], lfs[None], res[None], ims[None])
```

```python
import functools
import math

import jax
import jax.numpy as jnp
from jax import lax
from jax.experimental import pallas as pl
from jax.experimental.pallas import tpu as pltpu

F32 = jnp.float32
BF16 = jnp.bfloat16
EPS = 1e-6
NEG = -1e30
LOG2E = math.log2(math.e)
LANES = 128
SUBLANES = 8
VMEM_LIMIT = 56 * 1024 * 1024
HIGHEST = lax.Precision.HIGHEST
NT_DIMS = (((1,), (1,)), ((), ()))


def _cparams(*sem):
    return pltpu.CompilerParams(dimension_semantics=sem, vmem_limit_bytes=VMEM_LIMIT)


def _tile(n, pref):
    t = min(n, pref)
    assert n % t == 0, (n, pref)
    return t


def _rms(x, g):
    return x * lax.rsqrt(jnp.mean(x * x, axis=-1, keepdims=True) + EPS) * g


def _mod_operand(vec, seq_len, tm, off=0):
    n_seq, d = vec.shape
    if seq_len % tm == 0:
        per = seq_len // tm
        return vec[:, None, :], pl.BlockSpec((None, 1, d), lambda i, *_: ((i + off) // per, 0, 0))
    assert tm % seq_len == 0
    rows = jnp.repeat(vec, seq_len, axis=0).reshape(-1, tm, d)
    return rows, pl.BlockSpec((None, tm, d), lambda i, *_: (i + off, 0, 0))


def _cast_kernel(x_ref, o_ref):
    o_ref[...] = x_ref[...].astype(BF16)


def _to_bf16(w):
    n = w.shape[-1]
    w2 = w.reshape(-1, n)
    rows = w2.shape[0]
    tr = _tile(rows, max(SUBLANES, (1 << 20) // n))
    out = pl.pallas_call(
        _cast_kernel,
        out_shape=jax.ShapeDtypeStruct((rows, n), BF16),
        grid=(rows // tr,),
        in_specs=[pl.BlockSpec((tr, n), lambda i: (i, 0))],
        out_specs=pl.BlockSpec((tr, n), lambda i: (i, 0)),
        compiler_params=_cparams("arbitrary"),
        name="cast_bf16",
    )(w2)
    return out.reshape(w.shape)


def _ada_kernel(c_ref, w_ref, b_ref, o_ref):
    c = c_ref[...]
    a = (c * jax.nn.sigmoid(c)).astype(BF16)
    o_ref[...] = jnp.dot(a, w_ref[...].astype(BF16), preferred_element_type=F32) + b_ref[...]


def _ada(c, w, b):
    s, d = c.shape
    n = w.shape[1]
    tn = _tile(n, 1024)
    return pl.pallas_call(
        _ada_kernel,
        out_shape=jax.ShapeDtypeStruct((s, n), F32),
        grid=(n // tn,),
        in_specs=[pl.BlockSpec((s, d), lambda j: (0, 0)),
                  pl.BlockSpec((d, tn), lambda j: (0, j)),
                  pl.BlockSpec((1, tn), lambda j: (0, j))],
        out_specs=pl.BlockSpec((s, tn), lambda j: (0, j)),
        compiler_params=_cparams("arbitrary"),
        name="ada_mod",
    )(c, w, b.reshape(1, n))


def _inproj_kernel(x_ref, g_ref, sh_ref, sc_ref, w_ref, wf_ref, bf_ref,
                   q_ref, kf_ref, vf_ref, kb_ref, vb_ref, u_ref, lf_ref, *, qscale):
    da = q_ref.shape[1]
    h = _rms(x_ref[...], g_ref[...]) * (1.0 + sc_ref[...]) + sh_ref[...]
    hb = h.astype(BF16)
    fg = jnp.dot(hb, wf_ref[...], preferred_element_type=F32) + bf_ref[...]
    lf_ref[...] = jnp.minimum(fg, 0.0) - jnp.log1p(jnp.exp(-jnp.abs(fg)))
    proj = lambda j: jnp.dot(hb, w_ref[:, j * da:(j + 1) * da], preferred_element_type=F32)
    q_ref[...] = (proj(0) * qscale).astype(BF16)
    k = proj(1)
    kf_ref[...] = k
    kb_ref[...] = k.astype(BF16)
    v = proj(2)
    vf_ref[...] = v
    vb_ref[...] = v.astype(BF16)
    u = proj(3)
    for blk in range(u_ref.shape[0]):
        u_ref[blk] = u[:, blk * LANES:(blk + 1) * LANES]


def _inproj(x, g, sh, sc, w4, wf, bfp, seq_len, qscale):
    t, d = x.shape
    da = w4.shape[1] // 4
    tm = _tile(t, 512)
    sh_op, sh_spec = _mod_operand(sh, seq_len, tm)
    sc_op, sc_spec = _mod_operand(sc, seq_len, tm)
    row = lambda i: (i, 0)
    const = lambda i: (0, 0)
    outs = pl.pallas_call(
        functools.partial(_inproj_kernel, qscale=qscale),
        out_shape=(jax.ShapeDtypeStruct((t, da), BF16),
                   jax.ShapeDtypeStruct((t, da), F32), jax.ShapeDtypeStruct((t, da), F32),
                   jax.ShapeDtypeStruct((t, da), BF16), jax.ShapeDtypeStruct((t, da), BF16),
                   jax.ShapeDtypeStruct((da // LANES, t, LANES), F32),
                   jax.ShapeDtypeStruct((t, LANES), F32)),
        grid=(t // tm,),
        in_specs=[pl.BlockSpec((tm, d), row),
                  pl.BlockSpec((1, d), const),
                  sh_spec, sc_spec,
                  pl.BlockSpec((d, 4 * da), const, pipeline_mode=pl.Buffered(1)),
                  pl.BlockSpec((d, LANES), const),
                  pl.BlockSpec((1, LANES), const)],
        out_specs=(pl.BlockSpec((tm, da), row),) * 5
                  + (pl.BlockSpec((da // LANES, tm, LANES), lambda i: (0, i, 0)), pl.BlockSpec((tm, LANES), row)),
        compiler_params=_cparams("arbitrary"),
        name="in_proj",
    )(x, g.reshape(1, d), sh_op, sc_op, w4, wf, bfp)
    return outs


def _cumsum_kernel(x_ref, o_ref):
    sb, nb, _ = x_ref.shape
    li = lax.broadcasted_iota(jnp.int32, (LANES, LANES), 0)
    lj = lax.broadcasted_iota(jnp.int32, (LANES, LANES), 1)
    upper = (li <= lj).astype(F32)
    ri = lax.broadcasted_iota(jnp.int32, (nb, nb), 0)
    rj = lax.broadcasted_iota(jnp.int32, (nb, nb), 1)
    strict = (rj < ri).astype(F32)
    for s in range(sb):
        within = jnp.dot(x_ref[s], upper, precision=HIGHEST, preferred_element_type=F32)
        tot = jnp.broadcast_to(within[:, LANES - 1:LANES], (nb, LANES))
        off = jnp.dot(strict, tot, precision=HIGHEST, preferred_element_type=F32)
        o_ref[s] = within + off


def _cumsum_rows(x):
    n_rows, n = x.shape
    nb = -(-n // (LANES * SUBLANES)) * SUBLANES
    xp = jnp.pad(x, ((0, 0), (0, nb * LANES - n))).reshape(n_rows, nb, LANES)
    sb = _tile(n_rows, 16)
    out = pl.pallas_call(
        _cumsum_kernel,
        out_shape=jax.ShapeDtypeStruct((n_rows, nb, LANES), F32),
        grid=(n_rows // sb,),
        in_specs=[pl.BlockSpec((sb, nb, LANES), lambda i: (i, 0, 0))],
        out_specs=pl.BlockSpec((sb, nb, LANES), lambda i: (i, 0, 0)),
        compiler_params=_cparams("arbitrary"),
        name="logf_cumsum",
    )(xp)
    return out.reshape(n_rows, nb * LANES)


FOX_TQ = 2048
FOX_TK = 2048
FOX_SUB = 1024


def _col_from_row(row):
    n = row.shape[1]
    eye = lax.broadcasted_iota(jnp.int32, (n, n), 0) == lax.broadcasted_iota(jnp.int32, (n, n), 1)
    return jnp.sum(jnp.where(eye, jnp.broadcast_to(row, (n, n)), 0.0), axis=1, keepdims=True)


def _lane_tiles(x):
    return [x[:, j * LANES:(j + 1) * LANES] for j in range(x.shape[1] // LANES)]


def _fox_kernel(q_ref, k_ref, v_ref, f_ref, o_ref, m_scr, l_scr, acc_scr, *, tq, tk, sub):
    qi = pl.program_id(2)
    nsub = tq // sub
    fqb = [jnp.broadcast_to(_col_from_row(f_ref[qi * nsub + a]), (sub, LANES)) for a in range(nsub)]
    m_scr[...] = jnp.full(m_scr.shape, NEG, F32)
    l_scr[...] = jnp.zeros(l_scr.shape, F32)
    acc_scr[...] = jnp.zeros(acc_scr.shape, F32)

    def chain(a, k, v, fk, diagonal):
        rows = pl.ds(a * sub, sub)
        t1 = lax.dot_general(q_ref[rows, :], k, NT_DIMS, preferred_element_type=F32) - fk
        if diagonal:
            row = lax.broadcasted_iota(jnp.int32, t1.shape, 0)
            col = lax.broadcasted_iota(jnp.int32, t1.shape, 1)
            t1 = jnp.where(col <= row, t1, NEG)
        tiles = _lane_tiles(t1)
        part = functools.reduce(jnp.maximum, tiles)
        m_prev = m_scr[rows, :]
        m_new = jnp.maximum(m_prev, jnp.max(part, axis=-1, keepdims=True) + fqb[a])
        c = m_new - fqb[a]
        p = [jnp.exp2(t - c) for t in tiles]
        alpha = jnp.exp2(m_prev - m_new)
        l_scr[rows, :] = alpha * l_scr[rows, :] + functools.reduce(jnp.add, p)
        pv = jnp.dot(jnp.concatenate(p, axis=1).astype(BF16), v, preferred_element_type=F32)
        acc_scr[rows, :] = alpha * acc_scr[rows, :] + pv
        m_scr[rows, :] = m_new

    def full_step(kt, carry):
        ks = pl.multiple_of(kt * tk, tk)
        k = k_ref[pl.ds(ks, tk), :]
        v = v_ref[pl.ds(ks, tk), :]
        fk = jnp.concatenate([f_ref[kt * (tk // sub) + j] for j in range(tk // sub)], axis=1)
        for a in range(nsub):
            chain(a, k, v, fk, False)
        return carry

    lax.fori_loop(0, qi * (tq // tk), full_step, 0)
    for a in range(nsub):
        for j in range(a + 1):
            ks = pl.multiple_of((qi * nsub + j) * sub, sub)
            chain(a, k_ref[pl.ds(ks, sub), :], v_ref[pl.ds(ks, sub), :], f_ref[qi * nsub + j], j == a)
    o_ref[...] = acc_scr[...] / jnp.sum(l_scr[...], axis=-1, keepdims=True)


def _fox_prompt(qb, kb, vb, fcum2, bsz, seq, n_heads, hd):
    assert hd == LANES
    tq = _tile(seq, FOX_TQ)
    tk = _tile(tq, FOX_TK)
    sub = _tile(tk, FOX_SUB)
    nq = seq // tq
    f = fcum2.reshape(bsz, n_heads, seq // sub, 1, sub)
    return pl.pallas_call(
        functools.partial(_fox_kernel, tq=tq, tk=tk, sub=sub),
        out_shape=jax.ShapeDtypeStruct((bsz * seq, n_heads * hd), F32),
        grid=(bsz, n_heads, nq),
        in_specs=[pl.BlockSpec((tq, hd), lambda b, h, i: (b * nq + i, h)),
                  pl.BlockSpec((seq, hd), lambda b, h, i: (b, h)),
                  pl.BlockSpec((seq, hd), lambda b, h, i: (b, h)),
                  pl.BlockSpec((None, None, seq // sub, 1, sub), lambda b, h, i: (b, h, 0, 0, 0))],
        out_specs=pl.BlockSpec((tq, hd), lambda b, h, i: (b * nq + i, h)),
        scratch_shapes=[pltpu.VMEM((tq, LANES), F32), pltpu.VMEM((tq, LANES), F32), pltpu.VMEM((tq, hd), F32)],
        compiler_params=_cparams("arbitrary", "arbitrary", "arbitrary"),
        name="fox_prompt",
    )(qb, kb, vb, f)


def _fox_sample_kernel(q_ref, ck_ref, cv_ref, kn_ref, vn_ref, fq_ref, fkc_ref, fkn_ref,
                       rh_ref, ri_ref, lh_ref, lhn_ref, kin_ref, o_ref, m_scr, l_scr, acc_scr):
    kt = pl.program_id(1)

    @pl.when(kt == 0)
    def _():
        m_scr[...] = jnp.full(m_scr.shape, NEG, F32)
        l_scr[...] = jnp.zeros(l_scr.shape, F32)
        acc_scr[...] = jnp.zeros(acc_scr.shape, F32)

    q = q_ref[...]
    fq = fq_ref[...]
    n_heads = ck_ref.shape[1]
    s_new = q.shape[0] // n_heads

    def update(k4, v4, t1_of):
        n, h, d = k4.shape
        k2 = k4.reshape(n * h, d).astype(BF16)
        v2 = v4.reshape(n * h, d).astype(BF16)
        t1 = t1_of(lax.dot_general(q, k2, NT_DIMS, preferred_element_type=F32))
        m_prev = m_scr[...]
        m_new = jnp.maximum(m_prev, jnp.max(t1, axis=-1, keepdims=True) + fq)
        alpha = jnp.exp2(m_prev - m_new)
        p = jnp.exp2(t1 - (m_new - fq))
        l_scr[...] = alpha * l_scr[...] + jnp.sum(p, axis=-1, keepdims=True)
        acc_scr[...] = alpha * acc_scr[...] + jnp.dot(p.astype(BF16), v2, preferred_element_type=F32)
        m_scr[...] = m_new

    def cache_t1(s):
        head = lax.broadcasted_iota(jnp.int32, (n_heads, 1), 0)
        fkm = jnp.where(lh_ref[...] == head, fkc_ref[...], -NEG)
        return jnp.concatenate([s[h * s_new:(h + 1) * s_new] - fkm[h:h + 1] for h in range(n_heads)], axis=0)

    update(ck_ref[...], cv_ref[...], cache_t1)

    @pl.when(kt == pl.num_programs(1) - 1)
    def _():
        causal_head = jnp.where(kin_ref[...] <= ri_ref[...], lhn_ref[...], -1)
        update(kn_ref[...], vn_ref[...],
               lambda s: jnp.where(rh_ref[...] == causal_head, s - fkn_ref[...], NEG))
        o_ref[...] = acc_scr[...] / l_scr[...]


def _fox_sample(qb, k_new, v_new, f_all2, cache_k, cache_v):
    bsz, s_new, n_heads, hd = qb.shape
    past = cache_k.shape[1]
    tk = _tile(past, 1024)
    n_rows = n_heads * s_new
    assert n_rows % 16 == 0 and s_new % SUBLANES == 0
    q2 = jnp.swapaxes(qb, 1, 2).reshape(bsz, n_rows, hd)
    fq = f_all2[:, :, past:].reshape(bsz, n_rows, 1)
    fkc = jnp.swapaxes(f_all2[:, :, :past], 1, 2).reshape(bsz, 1, past * n_heads)
    fkn = jnp.swapaxes(f_all2[:, :, past:], 1, 2).reshape(bsz, 1, s_new * n_heads)
    r = jnp.arange(n_rows, dtype=jnp.int32).reshape(n_rows, 1)
    lane = lambda n: jnp.arange(n * n_heads, dtype=jnp.int32).reshape(1, n * n_heads)
    const = lambda a: pl.BlockSpec(a.shape, lambda b, j: (0,) * a.ndim)
    consts = (r // s_new, r % s_new, lane(tk) % n_heads, lane(s_new) % n_heads, lane(s_new) // n_heads)
    out = pl.pallas_call(
        _fox_sample_kernel,
        out_shape=jax.ShapeDtypeStruct((bsz, n_rows, hd), F32),
        grid=(bsz, past // tk),
        in_specs=[pl.BlockSpec((None, n_rows, hd), lambda b, j: (b, 0, 0)),
                  pl.BlockSpec((None, tk, n_heads, hd), lambda b, j: (b, j, 0, 0)),
                  pl.BlockSpec((None, tk, n_heads, hd), lambda b, j: (b, j, 0, 0)),
                  pl.BlockSpec((None, s_new, n_heads, hd), lambda b, j: (b, 0, 0, 0)),
                  pl.BlockSpec((None, s_new, n_heads, hd), lambda b, j: (b, 0, 0, 0)),
                  pl.BlockSpec((None, n_rows, 1), lambda b, j: (b, 0, 0)),
                  pl.BlockSpec((None, 1, tk * n_heads), lambda b, j: (b, 0, j)),
                  pl.BlockSpec((None, 1, s_new * n_heads), lambda b, j: (b, 0, 0))]
                 + [const(a) for a in consts],
        out_specs=pl.BlockSpec((None, n_rows, hd), lambda b, j: (b, 0, 0)),
        scratch_shapes=[pltpu.VMEM((n_rows, 1), F32), pltpu.VMEM((n_rows, 1), F32), pltpu.VMEM((n_rows, hd), F32)],
        compiler_params=_cparams("arbitrary", "arbitrary"),
        name="fox_sample",
    )(q2, cache_k, cache_v, k_new, v_new, fq, fkc, fkn, *consts)
    return jnp.swapaxes(out.reshape(bsz, n_heads, s_new, hd), 1, 2)


S5_IN_GROUPS = 8
S5_OUT_GROUPS = 16


def _s5_operators(lam_re, lam_im, log_dt, b_re, b_im, c_re, c_im, lc, n_steps):
    g, p = lam_re.shape
    hc = b_re.shape[2]
    gi, go = S5_IN_GROUPS, S5_OUT_GROUPS
    dt = jnp.exp(log_dt.astype(F32))[:, None]

    def power(k):
        mag = jnp.exp(lam_re * dt * k)
        return mag * jnp.cos(lam_im * dt * k), mag * jnp.sin(lam_im * dt * k)

    lbr, lbi = power(1.0)
    den = lam_re * lam_re + lam_im * lam_im
    fr = ((lbr - 1.0) * lam_re + lbi * lam_im) / den
    fi = (lbi * lam_re - (lbr - 1.0) * lam_im) / den
    bbr = fr[:, :, None] * b_re - fi[:, :, None] * b_im
    bbi = fr[:, :, None] * b_im + fi[:, :, None] * b_re
    ks = jnp.arange(lc + 1, dtype=F32)[None, :, None]
    mag = jnp.exp(lam_re[:, None, :] * dt[:, None, :] * ks)
    ang = lam_im[:, None, :] * dt[:, None, :] * ks
    pwr, pwi = mag * jnp.cos(ang), mag * jnp.sin(ang)
    cr, ci = jnp.swapaxes(c_re, 1, 2), jnp.swapaxes(c_im, 1, 2)
    d_r = bbr[:, :, :, None] * cr[:, :, None, :] - bbi[:, :, :, None] * ci[:, :, None, :]
    d_i = bbr[:, :, :, None] * ci[:, :, None, :] + bbi[:, :, :, None] * cr[:, :, None, :]
    kern = (jnp.einsum("gtp,gpab->gtab", pwr[:, :lc], d_r, precision=HIGHEST)
            - jnp.einsum("gtp,gpab->gtab", pwi[:, :lc], d_i, precision=HIGHEST))
    def block_diag(vals, row_block, n_blocks):
        w = vals.shape[-1]
        tiled = jnp.tile(vals, (1,) * (vals.ndim - 1) + (n_blocks,))
        col_block = jnp.arange(n_blocks * w) // w
        return jnp.where(row_block[:, None] == col_block[None, :], tiled, 0.0).astype(BF16)

    per_slab = lambda x, n: jnp.swapaxes(x.reshape((g // n, n) + x.shape[1:]), 1, 2)
    bd = block_diag(per_slab(kern, go).reshape(g // go, lc, go * hc, hc), jnp.arange(go * hc) // hc, go)
    rev_r, rev_i = pwr[:, lc - 1::-1][:, :lc], pwi[:, lc - 1::-1][:, :lc]
    bt_r, bt_i = jnp.swapaxes(bbr, 1, 2)[:, None], jnp.swapaxes(bbi, 1, 2)[:, None]
    w_r = rev_r[:, :, None, :] * bt_r - rev_i[:, :, None, :] * bt_i
    w_i = rev_r[:, :, None, :] * bt_i + rev_i[:, :, None, :] * bt_r
    place = lambda w: block_diag(per_slab(w, gi).reshape(g // gi, lc, gi * hc, p), jnp.arange(gi * hc) // hc, gi)
    bw = jnp.concatenate([place(w_r), place(w_i)], axis=-1)
    nr, ni = jnp.swapaxes(pwr[:, 1:], 1, 2), jnp.swapaxes(pwi[:, 1:], 1, 2)
    v_r = cr[:, :, None, :] * nr[:, :, :, None] - ci[:, :, None, :] * ni[:, :, :, None]
    v_i = cr[:, :, None, :] * ni[:, :, :, None] + ci[:, :, None, :] * nr[:, :, :, None]
    halves = go // gi
    split = lambda v: jnp.transpose(v.reshape(g // go, halves, gi, p, lc, hc), (0, 4, 1, 2, 3, 5))
    bv = jnp.stack([split(v_r), split(-v_i)], axis=3)
    r = jnp.arange(halves * 2 * gi * p)
    row_group = (r // (2 * gi * p)) * gi + (r // p) % gi
    bv = block_diag(bv.reshape(g // go, lc, halves * 2 * gi * p, hc), row_group, go)

    def coef(n_steps):
        jr, ji = power(float(lc * n_steps))
        c = jnp.stack([pwr[:, lc], pwi[:, lc], jr, ji], axis=1)
        c = jnp.swapaxes(c.reshape(g // gi, gi, 4, p), 1, 2).reshape(g // gi, 4, gi * p)
        return jnp.pad(c, ((0, 0), (0, SUBLANES - 4), (0, 0)))

    return [(bd, bw, bv, coef(n)) for n in n_steps]


def _s5_state_kernel(u_ref, bw_ref, coef_ref, h0_ref, hin_ref, hl_ref, s_scr, *, lc, n_q, n_seg):
    i = pl.program_id(1)
    tr = u_ref.shape[0] // lc
    n_tiles = s_scr.shape[0]
    nh = n_tiles // 2
    half = nh * LANES
    n_steps = s_scr.shape[1] // n_q
    acc = None
    for s in range(lc):
        a = u_ref[pl.ds(s, tr, stride=lc), :].astype(BF16)
        d = jnp.dot(a, bw_ref[s], preferred_element_type=F32)
        acc = d if acc is None else acc + d
    if n_steps == 1:
        rows = pl.ds(pl.multiple_of(i * tr, tr), tr)
    else:
        assert n_steps % tr == 0
        per = n_steps // tr
        rows = pl.ds(lax.rem(i, per) * (tr * n_q) + lax.div(i, per), tr, stride=n_q)
    for m in range(n_tiles):
        s_scr[m, rows, :] = acc[:, m * LANES:(m + 1) * LANES]

    @pl.when(i == pl.num_programs(1) - 1)
    def _():
        bc = lambda r: jnp.broadcast_to(coef_ref[r:r + 1, :], (n_q, half))
        ar, ai = bc(0), bc(1)
        jr, ji = coef_ref[2:3, :], coef_ref[3:4, :]
        step_rows = lambda j: pl.ds(pl.multiple_of(j * n_q, n_q), n_q)

        def load_rows(j):
            return jnp.concatenate([s_scr[m, step_rows(j), :] for m in range(n_tiles)], axis=1)

        def store_rows(j, x):
            for m in range(n_tiles):
                s_scr[m, step_rows(j), :] = x[:, m * LANES:(m + 1) * LANES]

        def scan_zero(j, carry):
            xr, xi = carry
            s = load_rows(j)
            store_rows(j, jnp.concatenate([xr, xi], axis=1))
            return ar * xr - ai * xi + s[:, :half], ar * xi + ai * xr + s[:, half:]

        zero = jnp.zeros((n_q, half), F32)
        xr_end, xi_end = lax.fori_loop(0, n_steps, scan_zero, (zero, zero))

        h0 = h0_ref[...]
        if n_seg == 1:
            er, ei = h0[:, :half], h0[:, half:]
            hl_ref[...] = jnp.concatenate([jr * er - ji * ei + xr_end, jr * ei + ji * er + xi_end], axis=1)
        else:
            er_rows, ei_rows = [], []
            for b in range(n_q // n_seg):
                r_, i_ = h0[b:b + 1, :half], h0[b:b + 1, half:]
                for sg in range(n_seg):
                    q = b * n_seg + sg
                    er_rows.append(r_)
                    ei_rows.append(i_)
                    r_, i_ = (jr * r_ - ji * i_ + xr_end[q:q + 1], jr * i_ + ji * r_ + xi_end[q:q + 1])
                hl_ref[b:b + 1, :] = jnp.concatenate([r_, i_], axis=1)
            er = jnp.concatenate(er_rows, axis=0)
            ei = jnp.concatenate(ei_rows, axis=0)

        def scan_fix(j, carry):
            fr, fi = carry
            store_rows(j, load_rows(j) + jnp.concatenate([fr, fi], axis=1))
            return ar * fr - ai * fi, ar * fi + ai * fr

        lax.fori_loop(0, n_steps, scan_fix, (er, ei))
        for m in range(n_tiles):
            if n_steps == 1:
                hin_ref[:, m * LANES:(m + 1) * LANES] = s_scr[m].astype(BF16)
            else:
                for q in range(n_q):
                    hin_ref[q * n_steps:(q + 1) * n_steps, m * LANES:(m + 1) * LANES] = (
                        s_scr[m, pl.ds(q, n_steps, stride=n_q), :].astype(BF16))


def _s5_out_kernel(u_ref, hin_ref, bd_ref, bv_ref, y_ref, a_scr, acc_scr, *, lc):
    tc = hin_ref.shape[0]
    n_blk = u_ref.shape[0]
    for s in range(lc):
        for h in range(n_blk):
            a_scr[s * tc:(s + 1) * tc, h * LANES:(h + 1) * LANES] = (
                u_ref[h, pl.ds(s, tc, stride=lc), :].astype(BF16))
    acc_scr[...] = jnp.dot(a_scr[...], bd_ref[0], preferred_element_type=F32)
    for tau in range(1, lc):
        n = (lc - tau) * tc
        acc_scr[tau * tc:, :] += jnp.dot(a_scr[:n, :], bd_ref[tau], preferred_element_type=F32)
    hin = hin_ref[...]
    for t in range(lc):
        y_t = acc_scr[t * tc:(t + 1) * tc, :] + jnp.dot(hin, bv_ref[t], preferred_element_type=F32)
        for h in range(n_blk):
            y_ref[h, pl.ds(t, tc, stride=lc), :] = y_t[:, h * LANES:(h + 1) * LANES]


def _s5(u3, h0_re, h0_im, ops, bsz, lc, n_seg):
    bd, bw, bv, coef = ops
    t = u3.shape[1]
    ds = u3.shape[0] * LANES
    n_groups, p = h0_re.shape[1], h0_re.shape[2]
    gi, go = S5_IN_GROUPS, S5_OUT_GROUPS
    wi, wo = ds // (n_groups // gi), ds // (n_groups // go)
    ws = gi * 2 * p
    nc = t // lc
    n_q = bsz * n_seg
    assert nc % n_q == 0 and n_q % SUBLANES == 0 and wi == LANES
    tr = _tile(nc if nc == n_q else nc // n_q, 256)
    pack = lambda h: h.reshape(bsz, n_groups // gi, gi * p)
    h0 = jnp.swapaxes(jnp.concatenate([pack(h0_re), pack(h0_im)], axis=-1), 0, 1)
    hin, hl = pl.pallas_call(
        functools.partial(_s5_state_kernel, lc=lc, n_q=n_q, n_seg=n_seg),
        out_shape=(jax.ShapeDtypeStruct((nc, (n_groups // gi) * ws), BF16),
                   jax.ShapeDtypeStruct((n_groups // gi, bsz, ws), F32)),
        grid=(n_groups // gi, nc // tr),
        in_specs=[pl.BlockSpec((None, tr * lc, wi), lambda k, i: (k, i, 0)),
                  pl.BlockSpec((None, lc, wi, ws), lambda k, i: (k, 0, 0, 0)),
                  pl.BlockSpec((None, SUBLANES, ws // 2), lambda k, i: (k, 0, 0)),
                  pl.BlockSpec((None, bsz, ws), lambda k, i: (k, 0, 0))],
        out_specs=(pl.BlockSpec((nc, ws), lambda k, i: (0, k)),
                   pl.BlockSpec((None, bsz, ws), lambda k, i: (k, 0, 0))),
        scratch_shapes=[pltpu.VMEM((ws // LANES, nc, LANES), F32)],
        compiler_params=_cparams("arbitrary", "arbitrary"),
        name="s5_state",
    )(u3, bw, coef, h0)
    tc = _tile(nc, 256)
    wso = (go // gi) * ws
    y = pl.pallas_call(
        functools.partial(_s5_out_kernel, lc=lc),
        out_shape=jax.ShapeDtypeStruct(u3.shape, F32),
        grid=(n_groups // go, nc // tc),
        in_specs=[pl.BlockSpec((wo // LANES, tc * lc, LANES), lambda s, i: (s, i, 0)),
                  pl.BlockSpec((tc, wso), lambda s, i: (i, s)),
                  pl.BlockSpec((None, lc, wo, wo), lambda s, i: (s, 0, 0, 0)),
                  pl.BlockSpec((None, lc, wso, wo), lambda s, i: (s, 0, 0, 0), pipeline_mode=pl.Buffered(1))],
        out_specs=pl.BlockSpec((wo // LANES, tc * lc, LANES), lambda s, i: (s, i, 0)),
        scratch_shapes=[pltpu.VMEM((lc * tc, wo), BF16), pltpu.VMEM((lc * tc, wo), F32)],
        compiler_params=_cparams("arbitrary", "arbitrary"),
        name="s5_out",
    )(u3, hin, bd, bv)
    hl = jnp.swapaxes(hl, 0, 1)
    unpack = lambda h: h.reshape(bsz, n_groups, p)
    return y, unpack(hl[:, :, :ws // 2]), unpack(hl[:, :, ws // 2:])


def _glu_kernel(y_ref, u_ref, d_ref, w_ref, b_ref, g_ref, o_ref):
    lanes = lambda r: jnp.concatenate([r[k] for k in range(r.shape[0])], axis=1)
    z = jax.nn.gelu(lanes(y_ref) + d_ref[...] * lanes(u_ref))
    gate = jax.nn.sigmoid(jnp.dot(z.astype(BF16), w_ref[...], preferred_element_type=F32) + b_ref[...])
    o_ref[...] = _rms(z * gate, g_ref[...]).astype(BF16)


def _glu(y, u, d_skip, w_glu, b_glu, g_ssm):
    nblk, t, _ = y.shape
    ds = nblk * LANES
    tm = _tile(t, 512)
    vec = pl.BlockSpec((1, ds), lambda i: (0, 0))
    blk = pl.BlockSpec((nblk, tm, LANES), lambda i: (0, i, 0))
    return pl.pallas_call(
        _glu_kernel,
        out_shape=jax.ShapeDtypeStruct((t, ds), BF16),
        grid=(t // tm,),
        in_specs=[blk, blk, vec, pl.BlockSpec((ds, ds), lambda i: (0, 0)), vec, vec],
        out_specs=pl.BlockSpec((tm, ds), lambda i: (i, 0)),
        compiler_params=_cparams("arbitrary"),
        name="s5_glu",
    )(y, u, d_skip.reshape(1, ds), w_glu, b_glu.reshape(1, ds), g_ssm.reshape(1, ds))


def _outproj_kernel(att_ref, ssm_ref, x_ref, wa_ref, ws_ref, ga_ref, gt_ref, gf_ref, sh_ref, sc_ref, wr_ref,
                    *rest):
    x1_ref, h2_ref, lg_ref = rest[-3:]
    a = _rms(att_ref[...], ga_ref[...]).astype(BF16)
    mixed = (jnp.dot(a, wa_ref[...], preferred_element_type=F32)
             + jnp.dot(ssm_ref[...], ws_ref[...], preferred_element_type=F32))
    x1 = x_ref[...] + gt_ref[...] * mixed
    x1_ref[...] = x1
    h2 = _rms(x1, gf_ref[...]) * (1.0 + sc_ref[...]) + sh_ref[...]
    hi = h2.astype(BF16)
    h2_ref[...] = hi
    lo = (h2 - hi.astype(F32)).astype(BF16)
    r = (jnp.dot(hi, wr_ref[...], preferred_element_type=F32)
         + jnp.dot(lo, wr_ref[...], preferred_element_type=F32))
    lg_ref[...] = r[:, :LANES] + r[:, LANES:]


OUTPROJ_TM = 512


def _outproj(att, ssm_n, x, wa, ws, g_att, gt1, g_ffn, sh2, sc2, wr, seq_len, t_all, row_off, shared):
    t, d = x.shape
    da, ds = att.shape[1], ssm_n.shape[1]
    tm = OUTPROJ_TM
    assert t % tm == 0 and row_off % tm == 0 and t_all % tm == 0
    off = row_off // tm
    gt_op, gt_spec = _mod_operand(gt1, seq_len, tm)
    sh_op, sh_spec = _mod_operand(sh2, seq_len, tm)
    sc_op, sc_spec = _mod_operand(sc2, seq_len, tm)
    row = lambda n: pl.BlockSpec((tm, n), lambda i: (i, 0))
    row_shared = lambda n: pl.BlockSpec((tm, n), lambda i: (i + off, 0))
    const = lambda a, b: pl.BlockSpec((a, b), lambda i: (0, 0))
    operands = [att, ssm_n, x, wa, ws, g_att.reshape(1, da), gt_op, g_ffn.reshape(1, d), sh_op, sc_op, wr]
    resident = lambda a, b: pl.BlockSpec((a, b), lambda i: (0, 0), pipeline_mode=pl.Buffered(1))
    in_specs = [row(da), row(ds), row(d), resident(da, d), resident(ds, d), const(1, da),
                gt_spec, const(1, d), sh_spec, sc_spec, const(d, 2 * LANES)]
    aliases = {}
    if shared is not None:
        aliases = {len(operands): 1, len(operands) + 1: 2}
        operands += list(shared)
        in_specs += [pl.BlockSpec(memory_space=pl.ANY)] * 2
    x1, h2, lg = pl.pallas_call(
        _outproj_kernel,
        out_shape=(jax.ShapeDtypeStruct((t, d), F32), jax.ShapeDtypeStruct((t_all, d), BF16),
                   jax.ShapeDtypeStruct((t_all, LANES), F32)),
        grid=(t // tm,),
        in_specs=in_specs,
        out_specs=(row(d), row_shared(d), row_shared(LANES)),
        input_output_aliases=aliases,
        compiler_params=_cparams("arbitrary"),
        name="out_proj",
    )(*operands)
    return x1, (h2, lg)


MOE_TM = 256
MOE_CHUNKS = 6
COMBINE_CHUNKS = 4


def _expert_kernel(te_ref, nu_ref, xs_ref, rw_ref, w1_ref, w3_ref, w2_ref, *rest):
    o_ref, w1_scr, w3_scr, w2_scr = rest[-4:]
    i = pl.program_id(0)
    live = i < nu_ref[0]

    @pl.when(live & ((i == 0) | (te_ref[i] != te_ref[jnp.maximum(i - 1, 0)])))
    def _():
        w1_scr[...] = w1_ref[...].astype(BF16)
        w3_scr[...] = w3_ref[...].astype(BF16)
        w2_scr[...] = w2_ref[...].astype(BF16)

    @pl.when(live)
    def _():
        x = xs_ref[...]
        h1 = jnp.dot(x, w1_scr[...], preferred_element_type=F32)
        h3 = jnp.dot(x, w3_scr[...], preferred_element_type=F32)
        hid = (h1 * jax.nn.sigmoid(h1) * h3).astype(BF16)
        o_ref[...] = jnp.dot(hid, w2_scr[...], preferred_element_type=F32) * rw_ref[...]

    @pl.when(i >= nu_ref[0])
    def _():
        o_ref[...] = jnp.zeros(o_ref.shape, F32)


def _route(logits, n_groups, n_experts, top_k, tm, n_chunks):
    t = logits.shape[0]
    epg = n_experts // n_groups
    tok = jnp.arange(t)
    g_logits = logits[:, :n_groups]
    p_group = jax.nn.softmax(g_logits, axis=-1)
    g_sel = jnp.argmax(g_logits, axis=-1)
    e_sel = logits[:, n_groups:n_groups + n_experts].reshape(t, n_groups, epg)[tok, g_sel]
    top_v, top_i = lax.top_k(e_sel, top_k)
    gate = p_group[tok, g_sel][:, None] * jax.nn.softmax(top_v, axis=-1)
    eid = (g_sel[:, None] * epg + top_i).reshape(-1).astype(jnp.int32)
    n_asg = t * top_k
    onehot = (eid[:, None] == jnp.arange(n_experts, dtype=jnp.int32)[None, :]).astype(jnp.int32)
    csum = jnp.cumsum(onehot, axis=0)
    rank = jnp.take_along_axis(csum, eid[:, None], axis=1)[:, 0] - 1
    counts = csum[-1]
    padded = (counts + tm - 1) // tm * tm
    pends = jnp.cumsum(padded)
    dest = (pends - padded)[eid] + rank
    n_pad = -(-(n_asg + n_experts * (tm - 1)) // (tm * n_chunks)) * (tm * n_chunks)
    row_asg = jnp.full((n_pad,), -1, jnp.int32).at[dest].set(
        jnp.arange(n_asg, dtype=jnp.int32), unique_indices=True, mode="promise_in_bounds")
    live = row_asg >= 0
    row_tok = jnp.where(live, row_asg // top_k, 0)
    row_w = jnp.where(live, gate.reshape(-1)[jnp.maximum(row_asg, 0)], 0.0)
    n_tiles = n_pad // tm
    tile_e = jnp.sum(pends[None, :] <= (jnp.arange(n_tiles, dtype=jnp.int32) * tm)[:, None], axis=1)
    tile_e = jnp.minimum(tile_e, n_experts - 1).astype(jnp.int32)
    n_used = (pends[-1] // tm).astype(jnp.int32).reshape(1)
    return row_tok, row_w, tile_e, n_used, dest.reshape(t, top_k)


def _experts(h2, logits, w1, w3, w2, n_groups, top_k):
    t, d = h2.shape
    n_experts, _, de = w1.shape
    tm = MOE_TM
    row_tok, row_w, tile_e, n_used, pos = _route(logits, n_groups, n_experts, top_k, tm, MOE_CHUNKS)
    n_pad = row_tok.shape[0]
    ct = n_pad // tm // MOE_CHUNKS
    rw = row_w.reshape(n_pad, 1)
    single = pl.Buffered(1)
    ys = None
    for c in range(MOE_CHUNKS):
        off = c * ct
        rows = slice(off * tm, (off + ct) * tm)
        xs = h2.at[row_tok[rows]].get(mode="promise_in_bounds")
        operands = [tile_e[off:off + ct], jnp.clip(n_used - off, 0, ct), xs, rw[rows], w1, w3, w2]
        in_specs = [pl.BlockSpec((tm, d), lambda i, te, nu: (i, 0)),
                    pl.BlockSpec((tm, 1), lambda i, te, nu: (i, 0)),
                    pl.BlockSpec((None, d, de), lambda i, te, nu: (te[i], 0, 0), pipeline_mode=single),
                    pl.BlockSpec((None, d, de), lambda i, te, nu: (te[i], 0, 0), pipeline_mode=single),
                    pl.BlockSpec((None, de, d), lambda i, te, nu: (te[i], 0, 0), pipeline_mode=single)]
        aliases = {}
        if ys is not None:
            aliases = {len(operands): 0}
            operands.append(ys)
            in_specs.append(pl.BlockSpec(memory_space=pl.ANY))
        ys = pl.pallas_call(
            _expert_kernel,
            out_shape=jax.ShapeDtypeStruct((n_pad, d), F32),
            grid_spec=pltpu.PrefetchScalarGridSpec(
                num_scalar_prefetch=2,
                grid=(ct,),
                in_specs=in_specs,
                out_specs=pl.BlockSpec((tm, d), lambda i, te, nu: (i + off, 0)),
                scratch_shapes=[pltpu.VMEM((d, de), BF16), pltpu.VMEM((d, de), BF16), pltpu.VMEM((de, d), BF16)]),
            input_output_aliases=aliases,
            compiler_params=_cparams("arbitrary"),
            name="moe_experts",
        )(*operands)
    return ys, pos


def _final_kernel(x_ref, gt_ref, g_ref, *rest, n_rows):
    y_refs, o_ref = rest[:n_rows], rest[-1]
    moe = y_refs[0][...]
    for r in y_refs[1:]:
        moe = moe + r[...]
    o_ref[...] = _rms(x_ref[...] + gt_ref[...] * moe, g_ref[...])


def _final(x1, gt2, g_final, ys, pos, seq_len, n_chunks):
    t, d = x1.shape
    tm = _tile(t, 256)
    ct = t // tm // n_chunks
    assert ct * n_chunks * tm == t
    y = None
    for c in range(n_chunks):
        off = c * ct
        gt_op, gt_spec = _mod_operand(gt2, seq_len, tm, off)
        rows = [ys.at[pos[off * tm:(off + ct) * tm, k]].get(mode="promise_in_bounds") for k in range(pos.shape[1])]
        local = pl.BlockSpec((tm, d), lambda i: (i, 0))
        shifted = pl.BlockSpec((tm, d), lambda i: (i + off, 0))
        operands = [x1, gt_op, g_final.reshape(1, d)] + rows
        in_specs = [shifted, gt_spec, pl.BlockSpec((1, d), lambda i: (0, 0))] + [local] * len(rows)
        aliases = {}
        if y is not None:
            aliases = {len(operands): 0}
            operands.append(y)
            in_specs.append(pl.BlockSpec(memory_space=pl.ANY))
        y = pl.pallas_call(
            functools.partial(_final_kernel, n_rows=len(rows)),
            out_shape=jax.ShapeDtypeStruct((t, d), F32),
            grid=(ct,),
            in_specs=in_specs,
            out_specs=shifted,
            input_output_aliases=aliases,
            compiler_params=_cparams("arbitrary"),
            name="moe_combine_norm",
        )(*operands)
    return y


S5_CHUNK = 16
S5_SEGMENTS = 4
TOP_K = 2


def _mixers(x, mod, cache, wts, s5_ops, t_all, row_off, shared):
    (g_mix, w4, wf, bfp, d_skip, w_glu, b_glu, g_att, g_ssm, wa, ws, g_ffn, wr,
     n_heads, hd, n_groups_ssm) = wts
    bsz, seq, d = x.shape
    da = n_heads * hd
    t = bsz * seq
    xt = x.reshape(t, d)
    sh1, sc1, gt1, sh2, sc2, gt2 = jnp.split(mod, 6, axis=-1)
    qb, kf, vf, kb, vb, u, lfp = _inproj(xt, g_mix, sh1, sc1, w4, wf, bfp, seq, hd ** -0.5 * LOG2E)
    logf = lfp[:, :n_heads].reshape(bsz, seq, n_heads)
    if cache is None:
        fcum = _cumsum_rows(jnp.swapaxes(logf, 1, 2).reshape(bsz * n_heads, seq))[:, :seq]
        att = _fox_prompt(qb, kb, vb, (fcum * LOG2E).reshape(bsz, n_heads, seq), bsz, seq, n_heads, hd)
        h0 = jnp.zeros((bsz, n_groups_ssm, s5_ops[3].shape[2] // S5_IN_GROUPS), F32)
        ssm_y, h_re, h_im = _s5(u, h0, h0, s5_ops, bsz, S5_CHUNK, S5_SEGMENTS)
    else:
        cache_k, cache_v, cache_logf, st_re, st_im = cache
        past = cache_k.shape[1]
        lf_all = jnp.concatenate([cache_logf.astype(F32), logf], axis=1)
        f_all = _cumsum_rows(jnp.swapaxes(lf_all, 1, 2).reshape(bsz * n_heads, past + seq))
        f_all = (f_all[:, :past + seq] * LOG2E).reshape(bsz, n_heads, past + seq)
        att = _fox_sample(qb.reshape(bsz, seq, n_heads, hd), kf.reshape(bsz, seq, n_heads, hd),
                          vf.reshape(bsz, seq, n_heads, hd), f_all, cache_k, cache_v)
        att = att.reshape(t, da)
        ssm_y, h_re, h_im = _s5(u, st_re.astype(F32), st_im.astype(F32), s5_ops, bsz, seq, 1)
    ssm_n = _glu(ssm_y, u, d_skip, w_glu, b_glu, g_ssm)
    x1, shared = _outproj(att, ssm_n, xt, wa, ws, g_att, gt1, g_ffn, sh2, sc2, wr, seq, t_all, row_off, shared)
    new_cache = (kf.reshape(bsz, seq, n_heads, hd), vf.reshape(bsz, seq, n_heads, hd), logf, h_re, h_im)
    return x1, gt2, shared, new_cache


def kernel(x_prompt, x_sample, cache_k, cache_v, cache_logf, state_ssm_re, state_ssm_im, c_prompt, c_sample, w_ada, b_ada, g_mix, w_in, b_f, lam_re, lam_im, log_dt, b_re, b_im, c_re, c_im, d_skip, w_glu, b_glu, g_att, g_ssm, w_out, g_ffn, w_rg, w_re, w1, w3, w2, g_final):
    depth = w_ada.shape[0]
    assert depth == 1, "the residual stream of a deeper stack would have to be threaded through the layers"
    n_heads, hd = cache_k.shape[3], cache_k.shape[4]
    da = n_heads * hd
    d = x_prompt.shape[-1]
    ds = d - da
    assert da == ds
    n_groups_ssm = state_ssm_re.shape[2]
    n_expert_groups = w_rg.shape[-1]
    n_experts = w_re.shape[-1]
    assert n_expert_groups + n_experts <= LANES and n_heads <= LANES
    l = 0
    bp = x_prompt.shape[0]
    mod = _ada(jnp.concatenate([c_prompt, c_sample], axis=0).astype(F32), w_ada[l], b_ada[l])
    wi = w_in[l]
    w4 = jnp.concatenate([wi[:, :3 * da], wi[:, 3 * da + n_heads:]], axis=1).astype(BF16)
    wf = jnp.pad(wi[:, 3 * da:3 * da + n_heads], ((0, 0), (0, LANES - n_heads))).astype(BF16)
    bfp = jnp.pad(b_f[l], (0, LANES - n_heads)).reshape(1, LANES).astype(F32)
    wr = jnp.pad(jnp.concatenate([w_rg[l], w_re[l]], axis=1),
                 ((0, 0), (0, LANES - n_expert_groups - n_experts))).astype(F32)
    wr_hi = wr.astype(BF16)
    wr_lo = (wr - wr_hi.astype(F32)).astype(BF16)
    wr2 = jnp.concatenate([wr_hi, wr_lo], axis=1)
    wo = _to_bf16(w_out[l])
    wts = (g_mix[l], w4, wf, bfp, d_skip[l], _to_bf16(w_glu[l]), b_glu[l], g_att[l], g_ssm[l],
           wo[:da], wo[da:], g_ffn[l], wr2, n_heads, hd, n_groups_ssm)
    s5_args = (lam_re[l].astype(F32), lam_im[l].astype(F32), log_dt[l], b_re[l].astype(F32), b_im[l].astype(F32),
               c_re[l].astype(F32), c_im[l].astype(F32))
    seq_p = x_prompt.shape[1]
    n_steps_p = seq_p // (S5_SEGMENTS * S5_CHUNK)
    if x_sample.shape[1] == S5_CHUNK:
        ops_p, ops_s = _s5_operators(*s5_args, S5_CHUNK, (n_steps_p, 1))
    else:
        (ops_p,) = _s5_operators(*s5_args, S5_CHUNK, (n_steps_p,))
        (ops_s,) = _s5_operators(*s5_args, x_sample.shape[1], (1,))
    bs, seq_s = x_sample.shape[:2]
    t_p, t_s = bp * seq_p, bs * seq_s
    x1p, gt2p, shared, (kp, vp, lfp, rep, imp) = _mixers(
        x_prompt.astype(F32), mod[:bp], None, wts, ops_p, t_p + t_s, 0, None)
    cache = (cache_k[l], cache_v[l], cache_logf[l], state_ssm_re[l], state_ssm_im[l])
    x1s, gt2s, (h2, logits), (ksm, vsm, lfs, res, ims) = _mixers(
        x_sample.astype(F32), mod[bp:], cache, wts, ops_s, t_p + t_s, t_p, shared)
    ys, pos = _experts(h2, logits, w1[l], w3[l], w2[l], n_expert_groups, TOP_K)
    yp = _final(x1p, gt2p, g_final, ys, pos[:t_p], seq_p, COMBINE_CHUNKS).reshape(x_prompt.shape)
    ysm = _final(x1s, gt2s, g_final, ys, pos[t_p:], seq_s, 1).reshape(x_sample.shape)
    return (yp, ysm, kp[None], vp[None], lfp[None], rep[None], imp[None],
            ksm[None], vsm[None], lfs[None], res[None], ims[None])
```
